```python
import jax, jax.numpy as jnp
from jax import lax
import numpy as np

D_MODEL = 2048
BATCH = 4
SEQ = 4096
DEPTH = 4

GRID_W = 64
CTX_LEN = 256
N_MIXERS = 2
N_ATT_LAYERS = (DEPTH + N_MIXERS - 1) // N_MIXERS
N_GLA_LAYERS = DEPTH // N_MIXERS
ATT_HEADS = 16
ATT_KV_HEADS = 4
HEAD_DIM = D_MODEL // ATT_HEADS
WINDOW = 128
ATT_BLOCK = 128
ROPE_BASE = 10000.0
ROPE_AXIS_DIM = HEAD_DIM // 2
GLA_HEADS = 4
GLA_DK = D_MODEL // 2
GLA_DV = D_MODEL
GLA_DK_HEAD = GLA_DK // GLA_HEADS
GLA_DV_HEAD = GLA_DV // GLA_HEADS
GLA_GATE_RANK = 16
GLA_GATE_NORM = 16.0
GLA_CHUNK = 16
D_FF = 5632
CONV_W = 3
EPS = 1e-6
N_MOD = 6

kernel_name = "hybrid_swa_gla_convffn_dit"


def rmsnorm(x, g):
    xf = x.astype(jnp.float32)
    y = xf * lax.rsqrt(jnp.mean(xf * xf, axis=-1, keepdims=True) + EPS)
    return (y * g.astype(jnp.float32)).astype(x.dtype)


def modulate(x, g, shift, scale):
    return rmsnorm(x, g) * (1 + scale) + shift


def axial_rope_tables(n):
    rows = n // GRID_W
    row = jnp.repeat(jnp.arange(rows, dtype=jnp.float32), GRID_W)
    col = jnp.tile(jnp.arange(GRID_W, dtype=jnp.float32), rows)
    inv = ROPE_BASE ** (-jnp.arange(0, ROPE_AXIS_DIM, 2, dtype=jnp.float32) / ROPE_AXIS_DIM)
    ang_r = row[:, None] * inv[None, :]
    ang_c = col[:, None] * inv[None, :]
    return (jnp.cos(ang_r), jnp.sin(ang_r), jnp.cos(ang_c), jnp.sin(ang_c))


def rope_1d(x, cos, sin):
    shp = (cos.shape[0],) + (1,) * (x.ndim - 3) + (cos.shape[1],)
    c = cos.reshape(shp)
    s = sin.reshape(shp)
    x1, x2 = jnp.split(x, 2, axis=-1)
    return jnp.concatenate([x1 * c - x2 * s, x2 * c + x1 * s], axis=-1)


def apply_axial_rope(x, rope):
    cos_r, sin_r, cos_c, sin_c = rope
    xr, xc = jnp.split(x.astype(jnp.float32), 2, axis=-1)
    out = jnp.concatenate([rope_1d(xr, cos_r, sin_r), rope_1d(xc, cos_c, sin_c)], axis=-1)
    return out.astype(x.dtype)


def sink_softmax(logits, sink_logit):
    m = jnp.maximum(jnp.max(logits, axis=-1, keepdims=True), sink_logit)
    e = jnp.exp(logits - m)
    return e / (jnp.sum(e, axis=-1, keepdims=True) + jnp.exp(sink_logit - m))


def attn_mixer(h_ctx, h_lat, w_qkv, sink, w_o, rope, ctx_out):
    B, N, _ = h_lat.shape
    G = ATT_HEADS // ATT_KV_HEADS
    nq = ATT_HEADS * HEAD_DIM
    nkv = ATT_KV_HEADS * HEAD_DIM
    scale = HEAD_DIM ** -0.5

    def proj(h):
        n = h.shape[1]
        q, k, v = jnp.split(h @ w_qkv, [nq, nq + nkv], axis=-1)
        return (q.reshape(B, n, ATT_KV_HEADS, G, HEAD_DIM),
                k.reshape(B, n, ATT_KV_HEADS, HEAD_DIM),
                v.reshape(B, n, ATT_KV_HEADS, HEAD_DIM))

    qc, kc, vc = proj(h_ctx)
    ql, kl, vl = proj(h_lat)
    ql = apply_axial_rope(ql, rope)
    kl = apply_axial_rope(kl, rope)
    sink_b = sink.astype(jnp.float32).reshape(ATT_KV_HEADS, G)[None, :, :, None, None]

    oc = None
    if ctx_out:
        s = jnp.einsum('blkgd,bmkd->bkglm', qc, kc, preferred_element_type=jnp.float32) * scale
        p = sink_softmax(s, sink_b).astype(vc.dtype)
        oc = jnp.einsum('bkglm,bmkd->blkgd', p, vc).reshape(B, -1, nq) @ w_o

    nb = N // ATT_BLOCK

    def windows(t):
        tp = jnp.pad(t, ((0, 0), (ATT_BLOCK, ATT_BLOCK), (0, 0), (0, 0)))
        tp = tp.reshape(B, nb + 2, ATT_BLOCK, ATT_KV_HEADS, HEAD_DIM)
        w = jnp.concatenate([tp[:, :-2], tp[:, 1:-1], tp[:, 2:]], axis=2)
        return jnp.moveaxis(w, 1, 0)

    qb = jnp.moveaxis(ql.reshape(B, nb, ATT_BLOCK, ATT_KV_HEADS, G, HEAD_DIM), 1, 0)
    kw = windows(kl)
    vw = windows(vl)
    qi = jnp.arange(ATT_BLOCK)[:, None]
    kj = jnp.arange(3 * ATT_BLOCK)[None, :]
    band = jnp.abs(qi - kj + ATT_BLOCK) <= WINDOW

    def block(args):
        q, k, v, blk = args
        kpos = blk * ATT_BLOCK - ATT_BLOCK + kj
        mask = band & (kpos >= 0) & (kpos < N)
        s_w = jnp.einsum('bqkgd,bjkd->bkgqj', q, k, preferred_element_type=jnp.float32) * scale
        s_w = jnp.where(mask, s_w, -jnp.inf)
        s_c = jnp.einsum('bqkgd,bjkd->bkgqj', q, kc, preferred_element_type=jnp.float32) * scale
        p = sink_softmax(jnp.concatenate([s_w, s_c], axis=-1), sink_b).astype(v.dtype)
        o = (jnp.einsum('bkgqj,bjkd->bqkgd', p[..., :3 * ATT_BLOCK], v)
             + jnp.einsum('bkgqj,bjkd->bqkgd', p[..., 3 * ATT_BLOCK:], vc))
        return o.reshape(B, ATT_BLOCK, nq)

    ol = lax.map(block, (qb, kw, vw, jnp.arange(nb)))
    ol = jnp.moveaxis(ol, 0, 1).reshape(B, N, nq) @ w_o
    return oc, ol


def gla_chunk_scan(q, k, v, g, s0):
    B, H, n, dk = q.shape
    dv = v.shape[-1]
    nc = n // GLA_CHUNK
    f32 = jnp.float32
    qf = q.astype(f32).reshape(B, H, nc, GLA_CHUNK, dk)
    kf = k.astype(f32).reshape(B, H, nc, GLA_CHUNK, dk)
    vf = v.astype(f32).reshape(B, H, nc, GLA_CHUNK, dv)
    b = jnp.cumsum(g.astype(f32).reshape(B, H, nc, GLA_CHUNK, dk), axis=3)
    b_last = b[:, :, :, -1]
    q_in = qf * jnp.exp(b)
    k_in = kf * jnp.exp(-b)
    k_st = kf * jnp.exp(b_last[:, :, :, None, :] - b)
    causal = jnp.tril(jnp.ones((GLA_CHUNK, GLA_CHUNK), dtype=bool))
    a = jnp.where(causal, jnp.einsum('bhcid,bhcjd->bhcij', q_in, k_in), 0.0)
    o_intra = jnp.einsum('bhcij,bhcjv->bhciv', a, vf)

    def step(S, xs):
        qi, ks, vi, dl = xs
        o = jnp.einsum('bhcd,bhdv->bhcv', qi, S)
        S = S * jnp.exp(dl)[..., None] + jnp.einsum('bhcd,bhcv->bhdv', ks, vi)
        return S, o

    xs = (jnp.moveaxis(q_in, 2, 0), jnp.moveaxis(k_st, 2, 0),
          jnp.moveaxis(vf, 2, 0), jnp.moveaxis(b_last, 2, 0))
    S, o_inter = lax.scan(step, s0.astype(f32), xs)
    o = o_intra + jnp.moveaxis(o_inter, 0, 2)
    return o.reshape(B, H, n, dv), S


def gla_mixer(h_ctx, h_lat, w_in, gf_w1, gf_w2, gf_b, gb_w1, gb_w2, gb_b, onorm_g, w_o, ctx_out):
    B = h_lat.shape[0]

    def heads(t, d):
        return t.reshape(B, t.shape[1], GLA_HEADS, d).transpose(0, 2, 1, 3)

    def gate(h, w1, w2, bias):
        z = ((h @ w1) @ w2 + bias).astype(jnp.float32)
        return heads(jax.nn.log_sigmoid(z) / GLA_GATE_NORM, GLA_DK_HEAD)

    def proj(h):
        q, k, v, r = jnp.split(h @ w_in, [GLA_DK, 2 * GLA_DK, 2 * GLA_DK + GLA_DV], axis=-1)
        q = heads(q, GLA_DK_HEAD) * (GLA_DK_HEAD ** -0.5)
        return (q, heads(k, GLA_DK_HEAD), heads(v, GLA_DV_HEAD), r,
                gate(h, gf_w1, gf_w2, gf_b), gate(h, gb_w1, gb_w2, gb_b))

    def flip(t):
        return t[:, :, ::-1]

    def bidir(q, k, v, gf, gb, sf0, sb0):
        of, sf = gla_chunk_scan(q, k, v, gf, sf0)
        ob, sb = gla_chunk_scan(flip(q), flip(k), flip(v), flip(gb), sb0)
        return of + flip(ob), sf, sb

    def out(o, r):
        o = rmsnorm(o, onorm_g)
        o = o.transpose(0, 2, 1, 3).reshape(B, o.shape[2], GLA_DV).astype(r.dtype)
        return (o * jax.nn.silu(r)) @ w_o

    s0 = jnp.zeros((B, GLA_HEADS, GLA_DK_HEAD, GLA_DV_HEAD), jnp.float32)
    qc, kc, vc, rc, gfc, gbc = proj(h_ctx)
    o_c, sf, sb = bidir(qc, kc, vc, gfc, gbc, s0, s0)
    ql, kl, vl, rl, gfl, gbl = proj(h_lat)
    o_l, _, _ = bidir(ql, kl, vl, gfl, gbl, sf, sb)
    oc = out(o_c, rc) if ctx_out else None
    return oc, out(o_l, rl)


def conv_ffn(h, w_up, conv_w, conv_b, w_down):
    u = h @ w_up
    n = u.shape[1]
    up = jnp.pad(u, ((0, 0), (CONV_W // 2, CONV_W // 2), (0, 0)))
    u = conv_b + sum(conv_w[j] * up[:, j:j + n] for j in range(CONV_W))
    gate, val = jnp.split(u, 2, axis=-1)
    return (jax.nn.silu(gate) * val) @ w_down


def setup_inputs(seed: int = 0) -> dict:
    key = jax.random.key(seed)
    ks = jax.random.split(key, 32)
    D = D_MODEL
    nrm = jax.random.normal
    f = jnp.float32
    qkv_w = ATT_HEADS * HEAD_DIM + 2 * ATT_KV_HEADS * HEAD_DIM
    gla_in_w = 2 * GLA_DK + 2 * GLA_DV
    return {
        "x": nrm(ks[0], (BATCH, SEQ, D), f),
        "c": nrm(ks[1], (BATCH, D), f),
        "ctx": nrm(ks[2], (BATCH, CTX_LEN, D), f),
        "c_ctx": nrm(ks[3], (D,), f),
        "ada_w": nrm(ks[4], (DEPTH, D, N_MOD * D), f) * (0.5 * D ** -0.5),
        "ada_b": nrm(ks[5], (DEPTH, N_MOD * D), f) * 0.02,
        "norm_mix_g": 1.0 + 0.02 * nrm(ks[6], (DEPTH, D), f),
        "norm_ffn_g": 1.0 + 0.02 * nrm(ks[7], (DEPTH, D), f),
        "ffn_w_up": nrm(ks[8], (DEPTH, D, 2 * D_FF), f) * D ** -0.5,
        "ffn_conv_w": nrm(ks[9], (DEPTH, CONV_W, 2 * D_FF), f) * CONV_W ** -0.5,
        "ffn_conv_b": nrm(ks[10], (DEPTH, 2 * D_FF), f) * 0.02,
        "ffn_w_down": nrm(ks[11], (DEPTH, D_FF, D), f) * D_FF ** -0.5,
        "attn_w_qkv": nrm(ks[12], (N_ATT_LAYERS, D, qkv_w), f) * D ** -0.5,
        "attn_sink": nrm(ks[13], (N_ATT_LAYERS, ATT_HEADS), f) * 0.5,
        "attn_w_o": nrm(ks[14], (N_ATT_LAYERS, ATT_HEADS * HEAD_DIM, D), f) * (ATT_HEADS * HEAD_DIM) ** -0.5,
        "gla_w_in": nrm(ks[15], (N_GLA_LAYERS, D, gla_in_w), f) * D ** -0.5,
        "gla_gf_w1": nrm(ks[16], (N_GLA_LAYERS, D, GLA_GATE_RANK), f) * D ** -0.5,
        "gla_gf_w2": nrm(ks[17], (N_GLA_LAYERS, GLA_GATE_RANK, GLA_DK), f) * GLA_GATE_RANK ** -0.5,
        "gla_gf_b": nrm(ks[18], (N_GLA_LAYERS, GLA_DK), f) * 0.1,
        "gla_gb_w1": nrm(ks[19], (N_GLA_LAYERS, D, GLA_GATE_RANK), f) * D ** -0.5,
        "gla_gb_w2": nrm(ks[20], (N_GLA_LAYERS, GLA_GATE_RANK, GLA_DK), f) * GLA_GATE_RANK ** -0.5,
        "gla_gb_b": nrm(ks[21], (N_GLA_LAYERS, GLA_DK), f) * 0.1,
        "gla_onorm_g": 1.0 + 0.02 * nrm(ks[22], (N_GLA_LAYERS, GLA_DV_HEAD), f),
        "gla_w_o": nrm(ks[23], (N_GLA_LAYERS, GLA_DV, D), f) * GLA_DV ** -0.5,
        "final_norm_g": 1.0 + 0.02 * nrm(ks[24], (D,), f),
    }


def reference(x, c, ctx, c_ctx, ada_w, ada_b, norm_mix_g, norm_ffn_g, ffn_w_up, ffn_conv_w,
              ffn_conv_b, ffn_w_down, attn_w_qkv, attn_sink, attn_w_o, gla_w_in, gla_gf_w1,
              gla_gf_w2, gla_gf_b, gla_gb_w1, gla_gb_w2, gla_gb_b, gla_onorm_g, gla_w_o,
              final_norm_g):
    n_lat = x.shape[1]
    rope = axial_rope_tables(n_lat)
    sc = jax.nn.silu(c)[:, None, :]
    scc = jax.nn.silu(c_ctx)[None, None, :]
    xc = ctx
    for i in range(DEPTH):
        last = i == DEPTH - 1
        mod_l = jnp.split(sc @ ada_w[i] + ada_b[i], N_MOD, axis=-1)
        mod_c = jnp.split(scc @ ada_w[i] + ada_b[i], N_MOD, axis=-1)
        hl = modulate(x, norm_mix_g[i], mod_l[0], mod_l[1])
        hc = modulate(xc, norm_mix_g[i], mod_c[0], mod_c[1])
        j = i // N_MIXERS
        if i % N_MIXERS == 0:
            oc, ol = attn_mixer(hc, hl, attn_w_qkv[j], attn_sink[j], attn_w_o[j], rope, not last)
        else:
            oc, ol = gla_mixer(hc, hl, gla_w_in[j], gla_gf_w1[j], gla_gf_w2[j], gla_gf_b[j],
                               gla_gb_w1[j], gla_gb_w2[j], gla_gb_b[j], gla_onorm_g[j],
                               gla_w_o[j], not last)
        x = x + mod_l[2] * ol
        x = x + mod_l[5] * conv_ffn(modulate(x, norm_ffn_g[i], mod_l[3], mod_l[4]),
                                    ffn_w_up[i], ffn_conv_w[i], ffn_conv_b[i], ffn_w_down[i])
        if not last:
            xc = xc + mod_c[2] * oc
            xc = xc + mod_c[5] * conv_ffn(modulate(xc, norm_ffn_g[i], mod_c[3], mod_c[4]),
                                          ffn_w_up[i], ffn_conv_w[i], ffn_conv_b[i], ffn_w_down[i])
    return rmsnorm(x, final_norm_g)
```

```python
import functools

import jax
import jax.numpy as jnp
from jax import lax
from jax.experimental import pallas as pl
from jax.experimental.pallas import tpu as pltpu

GRID_W = 64
N_MIXERS = 2
ATT_HEADS = 16
ATT_KV_HEADS = 4
ATT_GROUP = ATT_HEADS // ATT_KV_HEADS
HEAD_DIM = 128
WINDOW = 128
ROPE_BASE = 10000.0
GLA_HEADS = 4
GLA_GATE_RANK = 16
GLA_GATE_NORM = 16.0
CONV_W = 3
EPS = 1e-6
N_MOD = 6
MOD_ROWS = 8

LANES = 128
SUBLANES = 8
VMEM_LIMIT_CAP = 56 * 1024 * 1024

BF16 = jnp.bfloat16
F32 = jnp.float32


def _dot(a, b):
    return jnp.dot(a, b, preferred_element_type=F32)


def _dot_nt(a, b):
    return lax.dot_general(a, b, (((1,), (1,)), ((), ())), preferred_element_type=F32)


def _dot_tn(a, b):
    return lax.dot_general(a, b, (((0,), (0,)), ((), ())), preferred_element_type=F32)


def _params(semantics, vmem_bytes):
    limit = min(VMEM_LIMIT_CAP, int(vmem_bytes * 1.25) + (4 << 20))
    return pltpu.CompilerParams(dimension_semantics=semantics, vmem_limit_bytes=limit)


def _sigmoid(x):
    return 1.0 / (1.0 + jnp.exp(-x))


def _silu(x):
    return x * _sigmoid(x)


def _norm_mod(x, g, shift, scale):
    y = x * lax.rsqrt(jnp.mean(x * x, axis=-1, keepdims=True) + EPS)
    return (y * g) * (1.0 + scale) + shift


def _mod_kernel(c_ref, w_ref, b_ref, o_ref):
    s = _silu(c_ref[...]).astype(BF16)
    o_ref[0] = _dot(s, w_ref[0].astype(BF16)) + b_ref[0]


def _modulations(cc, ada_w, ada_b):
    depth, d, n = ada_w.shape
    r = cc.shape[0]
    tn = 1024
    vmem = 2 * (d * tn * 4 + r * tn * 4 + tn * 4) + r * d * 4 + d * tn * 2
    return pl.pallas_call(
        _mod_kernel,
        out_shape=jax.ShapeDtypeStruct((depth, r, n), F32),
        grid=(depth, n // tn),
        in_specs=[
            pl.BlockSpec((r, d), lambda l, j: (0, 0)),
            pl.BlockSpec((1, d, tn), lambda l, j: (l, 0, j)),
            pl.BlockSpec((1, 1, tn), lambda l, j: (l, 0, j)),
        ],
        out_specs=pl.BlockSpec((1, r, tn), lambda l, j: (l, 0, j)),
        compiler_params=_params(("arbitrary", "arbitrary"), vmem),
        name="adaln_mod",
    )(cc, ada_w, ada_b.reshape(depth, 1, n))


def _proj_kernel(*refs, mode, n_rope_tiles):
    if mode == "attn":
        x_ref, mod_ref, g_ref, w_ref, cos_ref, sa_ref, sb_ref, y_ref, h_scr = refs
    elif mode == "gla":
        (x_ref, mod_ref, g_ref, w_ref, w1_ref, w2f_ref, w2b_ref, bf_ref, bb_ref,
         y_ref, gf_ref, gb_ref, h_scr) = refs
    else:
        x_ref, mod_ref, g_ref, w_ref, y_ref, h_scr = refs
    j = pl.program_id(1)

    @pl.when(j == 0)
    def _():
        h = _norm_mod(x_ref[...], g_ref[...], mod_ref[0, 0:1, :], mod_ref[0, 1:2, :]).astype(BF16)
        h_scr[...] = h
        if mode == "gla":
            t = _dot(h, w1_ref[...]).astype(BF16)
            for w2_ref, b_ref, o_ref in ((w2f_ref, bf_ref, gf_ref), (w2b_ref, bb_ref, gb_ref)):
                z = _dot(t, w2_ref[...]) + b_ref[...]
                o_ref[...] = -(jnp.maximum(-z, 0.0) + jnp.log1p(jnp.exp(-jnp.abs(z)))) / GLA_GATE_NORM

    y = _dot(h_scr[...], w_ref[...])

    if mode == "attn":
        @pl.when(j < n_rope_tiles)
        def _():
            cos, sa, sb = cos_ref[...], sa_ref[...], sb_ref[...]
            parts = []
            for s in range(y.shape[1] // HEAD_DIM):
                yh = y[:, s * HEAD_DIM:(s + 1) * HEAD_DIM]
                up = pltpu.roll(yh, HEAD_DIM - HEAD_DIM // 4, axis=1)
                dn = pltpu.roll(yh, HEAD_DIM // 4, axis=1)
                parts.append(yh * cos + up * sa + dn * sb)
            y_ref[...] = jnp.concatenate(parts, axis=1).astype(y_ref.dtype)

        @pl.when(j >= n_rope_tiles)
        def _():
            y_ref[...] = y.astype(y_ref.dtype)
    else:
        y_ref[...] = y.astype(y_ref.dtype)


def _project(x, mod, g, w, *, rows_per_mod, tm, tn, out_dtype, mode="plain", rope=None,
             seq=None, n_rope_tiles=0, gate=None):
    n, d = x.shape
    m = w.shape[1]
    grid = (n // tm, m // tn)
    in_specs = [
        pl.BlockSpec((tm, d), lambda i, j: (i, 0)),
        pl.BlockSpec((1, MOD_ROWS, d), lambda i, j: ((i * tm) // rows_per_mod, 0, 0)),
        pl.BlockSpec((1, d), lambda i, j: (0, 0)),
        pl.BlockSpec((d, tn), lambda i, j: (0, j)),
    ]
    args = [x, mod, g.reshape(1, d), w]
    out_shape = [jax.ShapeDtypeStruct((n, m), out_dtype)]
    out_specs = [pl.BlockSpec((tm, tn), lambda i, j: (i, j))]
    osz = jnp.dtype(out_dtype).itemsize
    vmem = 2 * (tm * d * 4 + d * tn * 2 + tm * tn * osz) + tm * d * 2 + tm * tn * 4 + tm * d * 4
    if mode == "attn":
        tiles_per_seq = seq // tm
        for tab in rope:
            in_specs.append(pl.BlockSpec((tm, HEAD_DIM), lambda i, j: (i % tiles_per_seq, 0)))
            args.append(tab)
        vmem += 6 * tm * HEAD_DIM * 4
    elif mode == "gla":
        w1, w2f, w2b, bf, bb = gate
        dk = w2f.shape[1]
        in_specs += [
            pl.BlockSpec(w1.shape, lambda i, j: (0, 0)),
            pl.BlockSpec(w2f.shape, lambda i, j: (0, 0)),
            pl.BlockSpec(w2b.shape, lambda i, j: (0, 0)),
            pl.BlockSpec((1, dk), lambda i, j: (0, 0)),
            pl.BlockSpec((1, dk), lambda i, j: (0, 0)),
        ]
        args += [w1, w2f, w2b, bf.reshape(1, dk), bb.reshape(1, dk)]
        out_shape += [jax.ShapeDtypeStruct((n, dk), F32)] * 2
        out_specs += [pl.BlockSpec((tm, dk), lambda i, j: (i, 0))] * 2
        vmem += 4 * tm * dk * 4 + 2 * tm * dk * 4 + 2 * (d * LANES * 2 + 2 * LANES * dk * 2)
    out = pl.pallas_call(
        functools.partial(_proj_kernel, mode=mode, n_rope_tiles=n_rope_tiles),
        out_shape=out_shape,
        grid=grid,
        in_specs=in_specs,
        out_specs=out_specs,
        scratch_shapes=[pltpu.VMEM((tm, d), BF16)],
        compiler_params=_params(("arbitrary", "arbitrary"), vmem),
        name="proj_" + mode,
    )(*args)
    return out[0] if mode != "gla" else out


def _attend(qg, kall, vall, mask, sink_col):
    s = _dot_nt(qg, kall) * (HEAD_DIM ** -0.5)
    if mask is not None:
        s = jnp.where(mask, s, -jnp.inf)
    mx = jnp.maximum(jnp.max(s, axis=-1, keepdims=True), sink_col)
    e = jnp.exp(s - mx)
    denom = jnp.sum(e, axis=-1, keepdims=True) + jnp.exp(sink_col - mx)
    return _dot(e.astype(BF16), vall) / denom


def _sink_column(sink_ref, kv, rows):
    cols = [jnp.broadcast_to(sink_ref[0:1, kv * ATT_GROUP + h:kv * ATT_GROUP + h + 1], (rows, 1))
            for h in range(ATT_GROUP)]
    return jnp.concatenate(cols, axis=0)


def _attn_kernel(*refs, band, tq, seq):
    if band:
        (q_ref, kp_ref, kc_ref, kn_ref, vp_ref, vc_ref, vn_ref, kx_ref, vx_ref,
         x_ref, mod_ref, sink_ref, wo_ref, o_ref) = refs
    else:
        q_ref, kx_ref, vx_ref, x_ref, mod_ref, sink_ref, wo_ref, o_ref = refs
    blk = WINDOW
    nsub = tq // blk if band else 1
    rows = blk if band else tq
    if band:
        kwin = jnp.concatenate([kp_ref[...], kc_ref[...], kn_ref[...]], axis=0)
        vwin = jnp.concatenate([vp_ref[...], vc_ref[...], vn_ref[...]], axis=0)
        t0 = (pl.program_id(0) * tq) % seq
        nctx = kx_ref.shape[0]
        ncol = 3 * blk + nctx
        qi = lax.broadcasted_iota(jnp.int32, (ATT_GROUP * blk, ncol), 0) & (blk - 1)
        kj = lax.broadcasted_iota(jnp.int32, (ATT_GROUP * blk, ncol), 1)
        in_band = jnp.abs(qi - kj + blk) <= WINDOW
    outs = []
    for sb in range(nsub):
        heads = [None] * ATT_HEADS
        for kv in range(ATT_KV_HEADS):
            lo = kv * HEAD_DIM
            qg = jnp.concatenate(
                [q_ref[sb * rows:(sb + 1) * rows, (kv * ATT_GROUP + h) * HEAD_DIM:(kv * ATT_GROUP + h + 1) * HEAD_DIM]
                 for h in range(ATT_GROUP)], axis=0)
            if band:
                kall = jnp.concatenate([kwin[sb * blk:sb * blk + 3 * blk, lo:lo + HEAD_DIM],
                                        kx_ref[:, lo:lo + HEAD_DIM]], axis=0)
                vall = jnp.concatenate([vwin[sb * blk:sb * blk + 3 * blk, lo:lo + HEAD_DIM],
                                        vx_ref[:, lo:lo + HEAD_DIM]], axis=0)
                kpos = t0 + (sb - 1) * blk + kj
                mask = (kj >= 3 * blk) | (in_band & (kpos >= 0) & (kpos < seq))
            else:
                kall = kx_ref[:, lo:lo + HEAD_DIM]
                vall = vx_ref[:, lo:lo + HEAD_DIM]
                mask = None
            og = _attend(qg, kall, vall, mask, _sink_column(sink_ref, kv, rows))
            for h in range(ATT_GROUP):
                heads[kv * ATT_GROUP + h] = og[h * rows:(h + 1) * rows].astype(BF16)
        outs.append(jnp.concatenate(heads, axis=1))
    o = outs[0] if len(outs) == 1 else jnp.concatenate(outs, axis=0)
    o_ref[...] = x_ref[...] + mod_ref[0, 2:3, :] * _dot(o, wo_ref[...])


def _attention(qkv, qkv_ctx, x, mod, sink, w_o, *, band, tq, seq, nctx, rows_per_mod):
    n, d = x.shape
    nq = ATT_HEADS * HEAD_DIM
    nkv = ATT_KV_HEADS * HEAD_DIM
    kcol, vcol = nq // nkv, nq // nkv + 1
    blk = WINDOW
    nblk = n // blk
    r = tq // blk
    q_spec = pl.BlockSpec((tq, nq), lambda i: (i, 0))
    tail_specs = [
        pl.BlockSpec((tq, d), lambda i: (i, 0)),
        pl.BlockSpec((1, MOD_ROWS, d), lambda i: ((i * tq) // rows_per_mod, 0, 0)),
        pl.BlockSpec((1, ATT_HEADS), lambda i: (0, 0)),
        pl.BlockSpec((nq, d), lambda i: (0, 0)),
    ]
    tail_args = [x, mod, sink.reshape(1, ATT_HEADS), w_o]
    if band:
        def ctx_idx(col):
            return lambda i: ((i * tq) // seq, col)
        in_specs = [
            q_spec,
            pl.BlockSpec((blk, nkv), lambda i: (jnp.maximum(i * r - 1, 0), kcol)),
            pl.BlockSpec((tq, nkv), lambda i: (i, kcol)),
            pl.BlockSpec((blk, nkv), lambda i: (jnp.minimum((i + 1) * r, nblk - 1), kcol)),
            pl.BlockSpec((blk, nkv), lambda i: (jnp.maximum(i * r - 1, 0), vcol)),
            pl.BlockSpec((tq, nkv), lambda i: (i, vcol)),
            pl.BlockSpec((blk, nkv), lambda i: (jnp.minimum((i + 1) * r, nblk - 1), vcol)),
            pl.BlockSpec((nctx, nkv), ctx_idx(kcol)),
            pl.BlockSpec((nctx, nkv), ctx_idx(vcol)),
        ] + tail_specs
        args = [qkv] * 7 + [qkv_ctx] * 2 + tail_args
        ncol = 3 * blk + nctx
    else:
        in_specs = [
            q_spec,
            pl.BlockSpec((tq, nkv), lambda i: (i, kcol)),
            pl.BlockSpec((tq, nkv), lambda i: (i, vcol)),
        ] + tail_specs
        args = [qkv] * 3 + tail_args
        ncol = tq
    vmem = (2 * (tq * nq * 2 + 4 * (tq + 2 * blk + nctx) * nkv * 2 + 2 * tq * d * 4 + nq * d * 2)
            + 6 * ATT_GROUP * blk * ncol * 4 + 2 * tq * d * 4)
    return pl.pallas_call(
        functools.partial(_attn_kernel, band=band, tq=tq, seq=seq),
        out_shape=jax.ShapeDtypeStruct((n, d), F32),
        grid=(n // tq,),
        in_specs=in_specs,
        out_specs=pl.BlockSpec((tq, d), lambda i: (i, 0)),
        compiler_params=_params(("arbitrary",), vmem),
        name="attn_band" if band else "attn_ctx",
    )(*args)


def _gla_scan_kernel(q_ref, k_ref, v_ref, g_ref, s0_ref, o_ref, sout_ref, s_scr, *, chunk, reverse, q_scale):
    c = pl.program_id(2)

    @pl.when(c == 0)
    def _():
        s_scr[...] = s0_ref[0, 0]

    row = lax.broadcasted_iota(jnp.int32, (chunk, chunk), 0)
    col = lax.broadcasted_iota(jnp.int32, (chunk, chunk), 1)
    tri = (col >= row) if reverse else (col <= row)
    tri_b = tri.astype(BF16)

    g = g_ref[...]
    g_hi = g.astype(BF16)
    r1 = g - g_hi.astype(F32)
    g_mid = r1.astype(BF16)
    g_lo = (r1 - g_mid.astype(F32)).astype(BF16)
    b = _dot(tri_b, g_hi) + _dot(tri_b, g_mid) + _dot(tri_b, g_lo)
    b_tot = b[0:1] if reverse else b[chunk - 1:chunk]
    b_mid = b[chunk // 2:chunk // 2 + 1]

    q = q_ref[...] * q_scale
    k = k_ref[...]
    vb = v_ref[...].astype(BF16)
    qa = (q * jnp.exp(b - b_mid)).astype(BF16)
    ka = (k * jnp.exp(b_mid - b)).astype(BF16)
    qs = (q * jnp.exp(b)).astype(BF16)
    ks = (k * jnp.exp(b_tot - b)).astype(BF16)

    a = jnp.where(tri, _dot_nt(qa, ka), 0.0).astype(BF16)
    s = s_scr[...]
    o_ref[...] = _dot(a, vb) + _dot(qs, s.astype(BF16))

    upd = _dot_tn(ks, vb)
    dk = s.shape[0]
    decay = jnp.exp(jnp.broadcast_to(b_tot, (LANES, dk))).T
    s_new = jnp.concatenate(
        [s[:, t * LANES:(t + 1) * LANES] * decay for t in range(s.shape[1] // LANES)], axis=1) + upd
    s_scr[...] = s_new

    @pl.when(c == pl.num_programs(2) - 1)
    def _():
        sout_ref[0, 0] = s_new


def _gla_scan(y, g, s0, *, batch, length, chunk, reverse, dk, dv):
    nc = length // chunk
    dk_total = GLA_HEADS * dk

    def rows(b, c):
        return b * nc + (nc - 1 - c if reverse else c)

    vmem = 2 * (3 * chunk * dk * 4 + 2 * chunk * dv * 4 + 2 * dk * dv * 4) + 6 * dk * dv * 4 + 12 * chunk * dk * 4
    return pl.pallas_call(
        functools.partial(_gla_scan_kernel, chunk=chunk, reverse=reverse, q_scale=dk ** -0.5),
        out_shape=[jax.ShapeDtypeStruct((batch * length, GLA_HEADS * dv), F32),
                   jax.ShapeDtypeStruct(s0.shape, F32)],
        grid=(batch, GLA_HEADS, nc),
        in_specs=[
            pl.BlockSpec((chunk, dk), lambda b, h, c: (rows(b, c), h)),
            pl.BlockSpec((chunk, dk), lambda b, h, c: (rows(b, c), dk_total // dk + h)),
            pl.BlockSpec((chunk, dv), lambda b, h, c: (rows(b, c), 2 * dk_total // dv + h)),
            pl.BlockSpec((chunk, dk), lambda b, h, c: (rows(b, c), h)),
            pl.BlockSpec((1, 1, dk, dv), lambda b, h, c: (b, h, 0, 0)),
        ],
        out_specs=[
            pl.BlockSpec((chunk, dv), lambda b, h, c: (rows(b, c), h)),
            pl.BlockSpec((1, 1, dk, dv), lambda b, h, c: (b, h, 0, 0)),
        ],
        scratch_shapes=[pltpu.VMEM((dk, dv), F32)],
        compiler_params=_params(("arbitrary", "arbitrary", "arbitrary"), vmem),
        name="gla_scan_rev" if reverse else "gla_scan_fwd",
    )(y, y, y, g, s0)


def _gla_out_kernel(of_ref, ob_ref, r_ref, x_ref, mod_ref, g_ref, wo_ref, o_ref, *, dv):
    o = of_ref[...] + ob_ref[...]
    parts = []
    for h in range(o.shape[1] // dv):
        oh = o[:, h * dv:(h + 1) * dv]
        parts.append((oh * lax.rsqrt(jnp.mean(oh * oh, axis=-1, keepdims=True) + EPS)) * g_ref[...])
    gated = (jnp.concatenate(parts, axis=1) * _silu(r_ref[...])).astype(BF16)
    o_ref[...] = x_ref[...] + mod_ref[0, 2:3, :] * _dot(gated, wo_ref[...])


def _gla_output(o_f, o_b, y, x, mod, onorm_g, w_o, *, tm, rows_per_mod, dv):
    n, d = x.shape
    dvt = o_f.shape[1]
    rcol = (y.shape[1] - dvt) // dvt
    vmem = 2 * (5 * tm * dvt * 4 + dvt * d * 2) + 4 * tm * dvt * 4
    return pl.pallas_call(
        functools.partial(_gla_out_kernel, dv=dv),
        out_shape=jax.ShapeDtypeStruct((n, d), F32),
        grid=(n // tm,),
        in_specs=[
            pl.BlockSpec((tm, dvt), lambda i: (i, 0)),
            pl.BlockSpec((tm, dvt), lambda i: (i, 0)),
            pl.BlockSpec((tm, dvt), lambda i: (i, rcol)),
            pl.BlockSpec((tm, d), lambda i: (i, 0)),
            pl.BlockSpec((1, MOD_ROWS, d), lambda i: ((i * tm) // rows_per_mod, 0, 0)),
            pl.BlockSpec((1, dv), lambda i: (0, 0)),
            pl.BlockSpec((dvt, d), lambda i: (0, 0)),
        ],
        out_specs=pl.BlockSpec((tm, d), lambda i: (i, 0)),
        compiler_params=_params(("arbitrary",), vmem),
        name="gla_out",
    )(o_f, o_b, y, x, mod, onorm_g.reshape(1, dv), w_o)


def _ffn_kernel(xp_ref, x_ref, xn_ref, mod_ref, g_ref, wg_ref, wv_ref, cwg_ref, cwv_ref, cbg_ref, cbv_ref,
                wd_ref, o_ref, h_scr, ug_scr, uv_scr, *, tm, seq):
    i = pl.program_id(0)
    j = pl.program_id(1)
    halo = SUBLANES

    @pl.when(j == 0)
    def _():
        g, shift, scale = g_ref[...], mod_ref[0, 3:4, :], mod_ref[0, 4:5, :]
        first = (i * tm) % seq == 0
        last = ((i + 1) * tm) % seq == 0
        hp = _norm_mod(xp_ref[...], g, shift, scale)
        hn = _norm_mod(xn_ref[...], g, shift, scale)
        h_scr[0:halo, :] = jnp.where(first, 0.0, hp).astype(BF16)
        h_scr[halo:halo + tm, :] = _norm_mod(x_ref[...], g, shift, scale).astype(BF16)
        h_scr[halo + tm:2 * halo + tm, :] = jnp.where(last, 0.0, hn).astype(BF16)
        o_ref[...] = jnp.zeros_like(o_ref)

    h = h_scr[...]
    ug_scr[...] = _dot(h, wg_ref[...])
    uv_scr[...] = _dot(h, wv_ref[...])

    def conv(u_scr, cw_ref, cb_ref):
        return (cb_ref[...]
                + cw_ref[0:1, :] * u_scr[halo - 1:halo - 1 + tm, :]
                + cw_ref[1:2, :] * u_scr[halo:halo + tm, :]
                + cw_ref[2:3, :] * u_scr[halo + 1:halo + 1 + tm, :])

    act = (_silu(conv(ug_scr, cwg_ref, cbg_ref)) * conv(uv_scr, cwv_ref, cbv_ref)).astype(BF16)
    o_ref[...] += _dot(act, wd_ref[...])

    @pl.when(j == pl.num_programs(1) - 1)
    def _():
        o_ref[...] = x_ref[...] + mod_ref[0, 5:6, :] * o_ref[...]


def _conv_ffn(x, mod, g, w_up, conv_w, conv_b, w_down, *, tm, tf, seq, rows_per_mod):
    n, d = x.shape
    dff = w_down.shape[0]
    nf = dff // tf
    halo = SUBLANES
    r = tm // halo
    nhalo = n // halo
    conv_b = conv_b.reshape(1, 2 * dff)
    vmem = (2 * (2 * tm * d * 4 + 2 * d * tf * 2 + tf * d * 2) + (tm + 2 * halo) * d * 2
            + 2 * (tm + 2 * halo) * tf * 4 + 6 * tm * tf * 4)
    return pl.pallas_call(
        functools.partial(_ffn_kernel, tm=tm, seq=seq),
        out_shape=jax.ShapeDtypeStruct((n, d), F32),
        grid=(n // tm, nf),
        in_specs=[
            pl.BlockSpec((halo, d), lambda i, j: (jnp.maximum(i * r - 1, 0), 0)),
            pl.BlockSpec((tm, d), lambda i, j: (i, 0)),
            pl.BlockSpec((halo, d), lambda i, j: (jnp.minimum((i + 1) * r, nhalo - 1), 0)),
            pl.BlockSpec((1, MOD_ROWS, d), lambda i, j: ((i * tm) // rows_per_mod, 0, 0)),
            pl.BlockSpec((1, d), lambda i, j: (0, 0)),
            pl.BlockSpec((d, tf), lambda i, j: (0, j)),
            pl.BlockSpec((d, tf), lambda i, j: (0, nf + j)),
            pl.BlockSpec((CONV_W, tf), lambda i, j: (0, j)),
            pl.BlockSpec((CONV_W, tf), lambda i, j: (0, nf + j)),
            pl.BlockSpec((1, tf), lambda i, j: (0, j)),
            pl.BlockSpec((1, tf), lambda i, j: (0, nf + j)),
            pl.BlockSpec((tf, d), lambda i, j: (j, 0)),
        ],
        out_specs=pl.BlockSpec((tm, d), lambda i, j: (i, 0)),
        scratch_shapes=[
            pltpu.VMEM((tm + 2 * halo, d), BF16),
            pltpu.VMEM((tm + 2 * halo, tf), F32),
            pltpu.VMEM((tm + 2 * halo, tf), F32),
        ],
        compiler_params=_params(("arbitrary", "arbitrary"), vmem),
        name="conv_ffn",
    )(x, x, x, mod, g.reshape(1, d), w_up, w_up, conv_w, conv_w, conv_b, conv_b, w_down)


def _final_norm_kernel(x_ref, g_ref, o_ref):
    x = x_ref[...]
    o_ref[...] = (x * lax.rsqrt(jnp.mean(x * x, axis=-1, keepdims=True) + EPS)) * g_ref[...]


def _final_norm(x, g, *, tm):
    n, d = x.shape
    return pl.pallas_call(
        _final_norm_kernel,
        out_shape=jax.ShapeDtypeStruct((n, d), F32),
        grid=(n // tm,),
        in_specs=[pl.BlockSpec((tm, d), lambda i: (i, 0)), pl.BlockSpec((1, d), lambda i: (0, 0))],
        out_specs=pl.BlockSpec((tm, d), lambda i: (i, 0)),
        compiler_params=_params(("arbitrary",), 4 * tm * d * 4),
        name="final_norm",
    )(x, g.reshape(1, d))


def _rope_tables(seq):
    half = HEAD_DIM // 2
    rows = seq // GRID_W
    row = jnp.repeat(jnp.arange(rows, dtype=F32), GRID_W)
    col = jnp.tile(jnp.arange(GRID_W, dtype=F32), rows)
    inv = ROPE_BASE ** (-jnp.arange(0, half, 2, dtype=F32) / half)
    ang_r = row[:, None] * inv[None, :]
    ang_c = col[:, None] * inv[None, :]
    zero = jnp.zeros_like(ang_r)
    cos = jnp.concatenate([jnp.cos(ang_r)] * 2 + [jnp.cos(ang_c)] * 2, axis=1)
    sa = jnp.concatenate([-jnp.sin(ang_r), zero, -jnp.sin(ang_c), zero], axis=1)
    sb = jnp.concatenate([zero, jnp.sin(ang_r), zero, jnp.sin(ang_c)], axis=1)
    return cos, sa, sb


def _tile(n, target):
    t = min(n, target)
    while n % t:
        t //= 2
    return t


def kernel(x, c, ctx, c_ctx, ada_w, ada_b, norm_mix_g, norm_ffn_g, ffn_w_up, ffn_conv_w, ffn_conv_b, ffn_w_down,
           attn_w_qkv, attn_sink, attn_w_o, gla_w_in, gla_gf_w1, gla_gf_w2, gla_gf_b, gla_gb_w1, gla_gb_w2,
           gla_gb_b, gla_onorm_g, gla_w_o, final_norm_g):
    batch, seq, d = x.shape
    nctx = ctx.shape[1]
    depth = ada_w.shape[0]
    n_lat, n_ctx = batch * seq, batch * nctx
    dk = gla_gf_w2.shape[2] // GLA_HEADS
    dv = gla_onorm_g.shape[1]
    nq = ATT_HEADS * HEAD_DIM
    assert seq % WINDOW == 0 and nctx % WINDOW == 0 and d % LANES == 0

    n_cond = batch + 1
    pad = (-n_cond) % SUBLANES
    cc = jnp.concatenate([c, c_ctx[None, :], jnp.zeros((pad, d), F32)], axis=0)
    mods = _modulations(cc, ada_w, ada_b)
    mods = mods.reshape(depth, n_cond + pad, N_MOD, d)
    mods = jnp.pad(mods, ((0, 0), (0, 0), (0, MOD_ROWS - N_MOD), (0, 0)))

    rope = _rope_tables(seq)
    xl = x.reshape(n_lat, d)
    xc = ctx.reshape(n_ctx, d)

    tm_l, tm_c = _tile(seq, 512), _tile(nctx, 512)
    tq_l = _tile(seq, 128)
    gla_chunk = _tile(nctx, 64)
    tf = 512

    for i in range(depth):
        last = i == depth - 1
        mod_l = mods[i, :batch]
        mod_c = mods[i, batch:batch + 1]
        j = i // N_MIXERS
        if i % N_MIXERS == 0:
            w_qkv = attn_w_qkv[j].astype(BF16)
            w_o = attn_w_o[j].astype(BF16)
            qkv_l = _project(xl, mod_l, norm_mix_g[i], w_qkv, rows_per_mod=seq, tm=tm_l, tn=512, out_dtype=BF16,
                             mode="attn", rope=rope, seq=seq, n_rope_tiles=(nq + ATT_KV_HEADS * HEAD_DIM) // 512)
            qkv_c = _project(xc, mod_c, norm_mix_g[i], w_qkv, rows_per_mod=n_ctx, tm=tm_c, tn=512, out_dtype=BF16)
            xl = _attention(qkv_l, qkv_c, xl, mod_l, attn_sink[j], w_o, band=True, tq=tq_l, seq=seq, nctx=nctx,
                            rows_per_mod=seq)
            if not last:
                xc = _attention(qkv_c, None, xc, mod_c, attn_sink[j], w_o, band=False, tq=nctx, seq=nctx, nctx=nctx,
                                rows_per_mod=n_ctx)
        else:
            w_in = gla_w_in[j].astype(BF16)
            w_o = gla_w_o[j].astype(BF16)
            rank = gla_gf_w1.shape[2]
            w1 = jnp.zeros((d, LANES), F32).at[:, :rank].set(gla_gf_w1[j]).at[:, rank:2 * rank].set(gla_gb_w1[j])
            w2f = jnp.zeros((LANES, GLA_HEADS * dk), F32).at[:rank].set(gla_gf_w2[j])
            w2b = jnp.zeros((LANES, GLA_HEADS * dk), F32).at[rank:2 * rank].set(gla_gb_w2[j])
            gate = (w1.astype(BF16), w2f.astype(BF16), w2b.astype(BF16), gla_gf_b[j], gla_gb_b[j])
            y_l, gf_l, gb_l = _project(xl, mod_l, norm_mix_g[i], w_in, rows_per_mod=seq, tm=tm_l, tn=512,
                                       out_dtype=F32, mode="gla", gate=gate)
            y_c, gf_c, gb_c = _project(xc, mod_c, norm_mix_g[i], w_in, rows_per_mod=n_ctx, tm=tm_c, tn=512,
                                       out_dtype=F32, mode="gla", gate=gate)
            s0 = jnp.zeros((batch, GLA_HEADS, dk, dv), F32)
            scan = functools.partial(_gla_scan, batch=batch, chunk=gla_chunk, dk=dk, dv=dv)
            of_c, sf = scan(y_c, gf_c, s0, length=nctx, reverse=False)
            ob_c, sb = scan(y_c, gb_c, s0, length=nctx, reverse=True)
            of_l, _ = scan(y_l, gf_l, sf, length=seq, reverse=False)
            ob_l, _ = scan(y_l, gb_l, sb, length=seq, reverse=True)
            xl = _gla_output(of_l, ob_l, y_l, xl, mod_l, gla_onorm_g[j], w_o, tm=tm_l, rows_per_mod=seq, dv=dv)
            if not last:
                xc = _gla_output(of_c, ob_c, y_c, xc, mod_c, gla_onorm_g[j], w_o, tm=tm_c, rows_per_mod=n_ctx, dv=dv)
        w_up = ffn_w_up[i].astype(BF16)
        w_down = ffn_w_down[i].astype(BF16)
        xl = _conv_ffn(xl, mod_l, norm_ffn_g[i], w_up, ffn_conv_w[i], ffn_conv_b[i], w_down,
                       tm=tm_l, tf=tf, seq=seq, rows_per_mod=seq)
        if not last:
            xc = _conv_ffn(xc, mod_c, norm_ffn_g[i], w_up, ffn_conv_w[i], ffn_conv_b[i], w_down,
                           tm=tm_c, tf=tf, seq=nctx, rows_per_mod=n_ctx)
    return _final_norm(xl, final_norm_g, tm=tm_l).reshape(batch, seq, d)
```

```python
import functools

import jax
import jax.numpy as jnp
from jax import lax
from jax.experimental import pallas as pl
from jax.experimental.pallas import tpu as pltpu

GRID_W = 64
N_MIXERS = 2
ATT_HEADS = 16
ATT_KV_HEADS = 4
ATT_GROUP = ATT_HEADS // ATT_KV_HEADS
HEAD_DIM = 128
WINDOW = 128
ROPE_BASE = 10000.0
GLA_HEADS = 4
GLA_GATE_RANK = 16
GLA_GATE_NORM = 16.0
CONV_W = 3
EPS = 1e-6
N_MOD = 6
MOD_ROWS = 8

LANES = 128
SUBLANES = 8
VMEM_LIMIT_CAP = 56 * 1024 * 1024

BF16 = jnp.bfloat16
F32 = jnp.float32


def _dot(a, b):
    return jnp.dot(a, b, preferred_element_type=F32)


def _dot_nt(a, b):
    return lax.dot_general(a, b, (((1,), (1,)), ((), ())), preferred_element_type=F32)


def _dot_tn(a, b):
    return lax.dot_general(a, b, (((0,), (0,)), ((), ())), preferred_element_type=F32)


def _params(semantics, vmem_bytes):
    limit = min(VMEM_LIMIT_CAP, int(vmem_bytes * 1.25) + (4 << 20))
    return pltpu.CompilerParams(dimension_semantics=semantics, vmem_limit_bytes=limit)


def _sigmoid(x):
    return 1.0 / (1.0 + jnp.exp(-x))


def _silu(x):
    return x * _sigmoid(x)


def _norm_mod(x, g, shift, scale):
    y = x * lax.rsqrt(jnp.mean(x * x, axis=-1, keepdims=True) + EPS)
    return (y * g) * (1.0 + scale) + shift


def _mod_kernel(c_ref, w_ref, b_ref, o_ref):
    s = _silu(c_ref[...]).astype(BF16)
    o_ref[0] = _dot(s, w_ref[0].astype(BF16)) + b_ref[0]


def _modulations(cc, ada_w, ada_b):
    depth, d, n = ada_w.shape
    r = cc.shape[0]
    tn = 1024
    vmem = 2 * (d * tn * 4 + r * tn * 4 + tn * 4) + r * d * 4 + d * tn * 2
    return pl.pallas_call(
        _mod_kernel,
        out_shape=jax.ShapeDtypeStruct((depth, r, n), F32),
        grid=(depth, n // tn),
        in_specs=[
            pl.BlockSpec((r, d), lambda l, j: (0, 0)),
            pl.BlockSpec((1, d, tn), lambda l, j: (l, 0, j)),
            pl.BlockSpec((1, 1, tn), lambda l, j: (l, 0, j)),
        ],
        out_specs=pl.BlockSpec((1, r, tn), lambda l, j: (l, 0, j)),
        compiler_params=_params(("arbitrary", "arbitrary"), vmem),
        name="adaln_mod",
    )(cc, ada_w, ada_b.reshape(depth, 1, n))


def _log_sigmoid(z):
    return -(jnp.maximum(-z, 0.0) + jnp.log1p(jnp.exp(-jnp.abs(z))))


def _proj_kernel(*refs, segments, rope_cols, gated, col_chunk):
    n_out = len(segments) + (2 if gated else 0)
    x_ref, mod_ref, g_ref, w_ref = refs[:4]
    extra = refs[4:len(refs) - n_out - 2]
    outs = refs[len(refs) - n_out - 2:len(refs) - 2]
    h_even, h_odd = refs[-2:]
    s = pl.program_id(0)

    @pl.when(s == 0)
    def _():
        h_odd[...] = jnp.zeros_like(h_odd)

    def body(h_new, h_old):
        h = h_old[...]
        if gated:
            w1_ref, w2f_ref, w2b_ref, bf_ref, bb_ref = extra
            t = _dot(h, w1_ref[...]).astype(BF16)
            for w2_ref, b_ref, o_ref in ((w2f_ref, bf_ref, outs[-2]), (w2b_ref, bb_ref, outs[-1])):
                o_ref[...] = _log_sigmoid(_dot(t, w2_ref[...]) + b_ref[...]) / GLA_GATE_NORM
        if rope_cols:
            cos, sa, sb = extra[0][...], extra[1][...], extra[2][...]
        col = 0
        for (width, _), y_ref in zip(segments, outs):
            for c0 in range(0, width, col_chunk):
                y = _dot(h, w_ref[:, col + c0:col + c0 + col_chunk])
                if col + c0 < rope_cols:
                    parts = []
                    for p in range(col_chunk // HEAD_DIM):
                        yh = y[:, p * HEAD_DIM:(p + 1) * HEAD_DIM]
                        if col + c0 + p * HEAD_DIM < rope_cols:
                            up = pltpu.roll(yh, HEAD_DIM - HEAD_DIM // 4, axis=1)
                            dn = pltpu.roll(yh, HEAD_DIM // 4, axis=1)
                            yh = yh * cos + up * sa + dn * sb
                        parts.append(yh)
                    y = jnp.concatenate(parts, axis=1)
                y_ref[:, c0:c0 + col_chunk] = y.astype(y_ref.dtype)
            col += width
        x = x_ref[...]
        gs = g_ref[...] * (1.0 + mod_ref[0, 1:2, :])
        r = lax.rsqrt(jnp.mean(x * x, axis=-1, keepdims=True) + EPS)
        h_new[...] = ((x * r) * gs + mod_ref[0, 0:1, :]).astype(BF16)

    @pl.when(s % 2 == 0)
    def _():
        body(h_even, h_odd)

    @pl.when(s % 2 == 1)
    def _():
        body(h_odd, h_even)


def _project(x, mod, g, w, segments, *, rows_per_mod, tm, rope=None, seq=None, rope_cols=0, gate=None):
    n, d = x.shape
    m = w.shape[1]
    nrow = n // tm
    col_chunk = 1024 if all(wd % 1024 == 0 for wd, _ in segments) else 512

    def cur(s):
        return jnp.minimum(s, nrow - 1)

    def prev(s):
        return jnp.maximum(s - 1, 0)

    in_specs = [
        pl.BlockSpec((tm, d), lambda s: (cur(s), 0)),
        pl.BlockSpec((1, MOD_ROWS, d), lambda s: ((cur(s) * tm) // rows_per_mod, 0, 0)),
        pl.BlockSpec((1, d), lambda s: (0, 0)),
        pl.BlockSpec((d, m), lambda s: (0, 0), pipeline_mode=pl.Buffered(1)),
    ]
    args = [x, mod, g.reshape(1, d), w]
    out_shape = [jax.ShapeDtypeStruct((n, wd), dt) for wd, dt in segments]
    out_specs = [pl.BlockSpec((tm, wd), lambda s: (prev(s), 0)) for wd, _ in segments]
    vmem = (2 * tm * d * 4 + d * m * 2 + 2 * sum(tm * wd * jnp.dtype(dt).itemsize for wd, dt in segments)
            + 2 * tm * d * 2 + 3 * tm * col_chunk * 4 + 2 * tm * d * 4)
    if rope is not None:
        tiles_per_seq = seq // tm
        for tab in rope:
            in_specs.append(pl.BlockSpec((tm, HEAD_DIM), lambda s: (prev(s) % tiles_per_seq, 0)))
            args.append(tab)
        vmem += 6 * tm * HEAD_DIM * 4
    if gate is not None:
        w1, w2f, w2b, bf, bb = gate
        dk = w2f.shape[1]
        in_specs += [
            pl.BlockSpec(w1.shape, lambda s: (0, 0)),
            pl.BlockSpec(w2f.shape, lambda s: (0, 0)),
            pl.BlockSpec(w2b.shape, lambda s: (0, 0)),
            pl.BlockSpec((1, dk), lambda s: (0, 0)),
            pl.BlockSpec((1, dk), lambda s: (0, 0)),
        ]
        args += [w1, w2f, w2b, bf.reshape(1, dk), bb.reshape(1, dk)]
        out_shape += [jax.ShapeDtypeStruct((n, dk), F32)] * 2
        out_specs += [pl.BlockSpec((tm, dk), lambda s: (prev(s), 0))] * 2
        vmem += 6 * tm * dk * 4 + 2 * (d * LANES * 2 + 2 * LANES * dk * 2)
    return pl.pallas_call(
        functools.partial(_proj_kernel, segments=tuple(segments), rope_cols=rope_cols, gated=gate is not None,
                          col_chunk=col_chunk),
        out_shape=out_shape,
        grid=(nrow + 1,),
        in_specs=in_specs,
        out_specs=out_specs,
        scratch_shapes=[pltpu.VMEM((tm, d), BF16), pltpu.VMEM((tm, d), BF16)],
        compiler_params=_params(("arbitrary",), vmem),
        name="proj",
    )(*args)


def _attend(qg, kall, vall, mask, sink_col):
    s = _dot_nt(qg, kall) * (HEAD_DIM ** -0.5)
    if mask is not None:
        s = jnp.where(mask, s, -jnp.inf)
    mx = jnp.maximum(jnp.max(s, axis=-1, keepdims=True), sink_col)
    e = jnp.exp(s - mx)
    denom = jnp.sum(e, axis=-1, keepdims=True) + jnp.exp(sink_col - mx)
    return _dot(e.astype(BF16), vall) / denom


def _sink_column(sink_ref, kv, rows):
    cols = [jnp.broadcast_to(sink_ref[0:1, kv * ATT_GROUP + h:kv * ATT_GROUP + h + 1], (rows, 1))
            for h in range(ATT_GROUP)]
    return jnp.concatenate(cols, axis=0)


def _attn_kernel(*refs, band, tq, seq):
    if band:
        (q_ref, kp_ref, kc_ref, kn_ref, vp_ref, vc_ref, vn_ref, kx_ref, vx_ref,
         x_ref, mod_ref, sink_ref, wo_ref, o_ref) = refs
    else:
        q_ref, kx_ref, vx_ref, x_ref, mod_ref, sink_ref, wo_ref, o_ref = refs
    blk = WINDOW
    nsub = tq // blk if band else 1
    rows = blk if band else tq
    if band:
        kwin = jnp.concatenate([kp_ref[...], kc_ref[...], kn_ref[...]], axis=0)
        vwin = jnp.concatenate([vp_ref[...], vc_ref[...], vn_ref[...]], axis=0)
        t0 = (pl.program_id(0) * tq) % seq
        nctx = kx_ref.shape[0]
        ncol = 3 * blk + nctx
        qi = lax.broadcasted_iota(jnp.int32, (ATT_GROUP * blk, ncol), 0) & (blk - 1)
        kj = lax.broadcasted_iota(jnp.int32, (ATT_GROUP * blk, ncol), 1)
        in_band = jnp.abs(qi - kj + blk) <= WINDOW
    outs = []
    for sb in range(nsub):
        heads = [None] * ATT_HEADS
        for kv in range(ATT_KV_HEADS):
            lo = kv * HEAD_DIM
            qg = jnp.concatenate(
                [q_ref[sb * rows:(sb + 1) * rows, (kv * ATT_GROUP + h) * HEAD_DIM:(kv * ATT_GROUP + h + 1) * HEAD_DIM]
                 for h in range(ATT_GROUP)], axis=0)
            if band:
                kall = jnp.concatenate([kwin[sb * blk:sb * blk + 3 * blk, lo:lo + HEAD_DIM],
                                        kx_ref[:, lo:lo + HEAD_DIM]], axis=0)
                vall = jnp.concatenate([vwin[sb * blk:sb * blk + 3 * blk, lo:lo + HEAD_DIM],
                                        vx_ref[:, lo:lo + HEAD_DIM]], axis=0)
                kpos = t0 + (sb - 1) * blk + kj
                mask = (kj >= 3 * blk) | (in_band & (kpos >= 0) & (kpos < seq))
            else:
                kall = kx_ref[:, lo:lo + HEAD_DIM]
                vall = vx_ref[:, lo:lo + HEAD_DIM]
                mask = None
            og = _attend(qg, kall, vall, mask, _sink_column(sink_ref, kv, rows))
            for h in range(ATT_GROUP):
                heads[kv * ATT_GROUP + h] = og[h * rows:(h + 1) * rows].astype(BF16)
        outs.append(jnp.concatenate(heads, axis=1))
    o = outs[0] if len(outs) == 1 else jnp.concatenate(outs, axis=0)
    o_ref[...] = x_ref[...] + mod_ref[0, 2:3, :] * _dot(o, wo_ref[...])


def _attention(qkv, qkv_ctx, x, mod, sink, w_o, *, band, tq, seq, nctx, rows_per_mod):
    n, d = x.shape
    nq = ATT_HEADS * HEAD_DIM
    nkv = ATT_KV_HEADS * HEAD_DIM
    kcol, vcol = nq // nkv, nq // nkv + 1
    blk = WINDOW
    nblk = n // blk
    r = tq // blk
    q_spec = pl.BlockSpec((tq, nq), lambda i: (i, 0))
    tail_specs = [
        pl.BlockSpec((tq, d), lambda i: (i, 0)),
        pl.BlockSpec((1, MOD_ROWS, d), lambda i: ((i * tq) // rows_per_mod, 0, 0)),
        pl.BlockSpec((1, ATT_HEADS), lambda i: (0, 0)),
        pl.BlockSpec((nq, d), lambda i: (0, 0)),
    ]
    tail_args = [x, mod, sink.reshape(1, ATT_HEADS), w_o]
    if band:
        def ctx_idx(col):
            return lambda i: ((i * tq) // seq, col)
        in_specs = [
            q_spec,
            pl.BlockSpec((blk, nkv), lambda i: (jnp.maximum(i * r - 1, 0), kcol)),
            pl.BlockSpec((tq, nkv), lambda i: (i, kcol)),
            pl.BlockSpec((blk, nkv), lambda i: (jnp.minimum((i + 1) * r, nblk - 1), kcol)),
            pl.BlockSpec((blk, nkv), lambda i: (jnp.maximum(i * r - 1, 0), vcol)),
            pl.BlockSpec((tq, nkv), lambda i: (i, vcol)),
            pl.BlockSpec((blk, nkv), lambda i: (jnp.minimum((i + 1) * r, nblk - 1), vcol)),
            pl.BlockSpec((nctx, nkv), ctx_idx(kcol)),
            pl.BlockSpec((nctx, nkv), ctx_idx(vcol)),
        ] + tail_specs
        args = [qkv] * 7 + [qkv_ctx] * 2 + tail_args
        ncol = 3 * blk + nctx
    else:
        in_specs = [
            q_spec,
            pl.BlockSpec((tq, nkv), lambda i: (i, kcol)),
            pl.BlockSpec((tq, nkv), lambda i: (i, vcol)),
        ] + tail_specs
        args = [qkv] * 3 + tail_args
        ncol = tq
    vmem = (2 * (tq * nq * 2 + 4 * (tq + 2 * blk + nctx) * nkv * 2 + 2 * tq * d * 4 + nq * d * 2)
            + 6 * ATT_GROUP * blk * ncol * 4 + 2 * tq * d * 4)
    return pl.pallas_call(
        functools.partial(_attn_kernel, band=band, tq=tq, seq=seq),
        out_shape=jax.ShapeDtypeStruct((n, d), F32),
        grid=(n // tq,),
        in_specs=in_specs,
        out_specs=pl.BlockSpec((tq, d), lambda i: (i, 0)),
        compiler_params=_params(("arbitrary",), vmem),
        name="attn_band" if band else "attn_ctx",
    )(*args)


def _gla_scan_kernel(qf_ref, kf_ref, vf_ref, gf_ref, qb_ref, kb_ref, vb_ref, gb_ref, s0f_ref, s0b_ref,
                     of_ref, ob_ref, sf_ref, sb_ref, *, chunk, dk, dv, q_scale):
    c = pl.program_id(1)

    @pl.when(c == 0)
    def _():
        sf_ref[...] = s0f_ref[...]
        sb_ref[...] = s0b_ref[...]

    row = lax.broadcasted_iota(jnp.int32, (chunk, chunk), 0)
    col = lax.broadcasted_iota(jnp.int32, (chunk, chunk), 1)
    directions = (
        (qf_ref, kf_ref, vf_ref, gf_ref, of_ref, sf_ref, col <= row, chunk - 1),
        (qb_ref, kb_ref, vb_ref, gb_ref, ob_ref, sb_ref, col >= row, 0),
    )
    for q_ref, k_ref, v_ref, g_ref, o_ref, s_ref, tri, last in directions:
        tri_b = tri.astype(BF16)
        g = g_ref[...]
        g_hi = g.astype(BF16)
        g_lo = (g - g_hi.astype(F32)).astype(BF16)
        b = _dot(tri_b, g_hi) + _dot(tri_b, g_lo)
        b_tot = b[last:last + 1]
        b_mid = b[chunk // 2:chunk // 2 + 1]

        q = q_ref[...] * q_scale
        k = k_ref[...]
        v = v_ref[...]
        qa = (q * jnp.exp(b - b_mid)).astype(BF16)
        ka = (k * jnp.exp(b_mid - b)).astype(BF16)
        qs = (q * jnp.exp(b)).astype(BF16)
        ks = (k * jnp.exp(b_tot - b)).astype(BF16)
        decay = jnp.exp(jnp.broadcast_to(b_tot, (LANES, b.shape[1]))).T

        for h in range(GLA_HEADS):
            kc = slice(h * dk, (h + 1) * dk)
            vc = slice(h * dv, (h + 1) * dv)
            a = jnp.where(tri, _dot_nt(qa[:, kc], ka[:, kc]), 0.0).astype(BF16)
            s = s_ref[0, h]
            o_ref[:, vc] = _dot(a, v[:, vc]) + _dot(qs[:, kc], s.astype(BF16))
            upd = _dot_tn(ks[:, kc], v[:, vc])
            s_ref[0, h] = jnp.concatenate(
                [s[:, t * LANES:(t + 1) * LANES] * decay[kc] for t in range(dv // LANES)], axis=1) + upd


def _gla_scan(qk, v, g_f, g_b, s0_f, s0_b, *, batch, length, chunk, dk, dv):
    nc = length // chunk
    hk, hv = GLA_HEADS * dk, GLA_HEADS * dv

    def fwd(b, c):
        return b * nc + c

    def bwd(b, c):
        return b * nc + nc - 1 - c

    state_spec = pl.BlockSpec((1, GLA_HEADS, dk, dv), lambda b, c: (b, 0, 0, 0))
    in_specs = []
    for rows in (fwd, bwd):
        in_specs += [
            pl.BlockSpec((chunk, hk), lambda b, c, rows=rows: (rows(b, c), 0)),
            pl.BlockSpec((chunk, hk), lambda b, c, rows=rows: (rows(b, c), 1)),
            pl.BlockSpec((chunk, hv), lambda b, c, rows=rows: (rows(b, c), 0)),
            pl.BlockSpec((chunk, hk), lambda b, c, rows=rows: (rows(b, c), 0)),
        ]
    in_specs += [state_spec, state_spec]
    state_bytes = GLA_HEADS * dk * dv * 4
    vmem = (2 * 2 * (3 * chunk * hk * 4 + chunk * hv * 2 + chunk * hv * 4) + 8 * state_bytes
            + 16 * chunk * hk * 4 + 4 * dk * dv * 4)
    return pl.pallas_call(
        functools.partial(_gla_scan_kernel, chunk=chunk, dk=dk, dv=dv, q_scale=dk ** -0.5),
        out_shape=[jax.ShapeDtypeStruct((batch * length, hv), F32)] * 2
        + [jax.ShapeDtypeStruct(s0_f.shape, F32)] * 2,
        grid=(batch, nc),
        in_specs=in_specs,
        out_specs=[
            pl.BlockSpec((chunk, hv), lambda b, c: (fwd(b, c), 0)),
            pl.BlockSpec((chunk, hv), lambda b, c: (bwd(b, c), 0)),
            state_spec,
            state_spec,
        ],
        compiler_params=_params(("arbitrary", "arbitrary"), vmem),
        name="gla_scan",
    )(qk, qk, v, g_f, qk, qk, v, g_b, s0_f, s0_b)


def _gla_out_kernel(of_ref, ob_ref, r_ref, x_ref, mod_ref, g_ref, wo_ref, o_ref, *, dv):
    o = of_ref[...] + ob_ref[...]
    parts = []
    for h in range(o.shape[1] // dv):
        oh = o[:, h * dv:(h + 1) * dv]
        parts.append((oh * lax.rsqrt(jnp.mean(oh * oh, axis=-1, keepdims=True) + EPS)) * g_ref[...])
    gated = (jnp.concatenate(parts, axis=1) * _silu(r_ref[...])).astype(BF16)
    o_ref[...] = x_ref[...] + mod_ref[0, 2:3, :] * _dot(gated, wo_ref[...])


def _gla_output(o_f, o_b, r, x, mod, onorm_g, w_o, *, tm, rows_per_mod, dv):
    n, d = x.shape
    dvt = o_f.shape[1]
    vmem = 2 * (5 * tm * dvt * 4 + dvt * d * 2) + 4 * tm * dvt * 4
    return pl.pallas_call(
        functools.partial(_gla_out_kernel, dv=dv),
        out_shape=jax.ShapeDtypeStruct((n, d), F32),
        grid=(n // tm,),
        in_specs=[
            pl.BlockSpec((tm, dvt), lambda i: (i, 0)),
            pl.BlockSpec((tm, dvt), lambda i: (i, 0)),
            pl.BlockSpec((tm, dvt), lambda i: (i, 0)),
            pl.BlockSpec((tm, d), lambda i: (i, 0)),
            pl.BlockSpec((1, MOD_ROWS, d), lambda i: ((i * tm) // rows_per_mod, 0, 0)),
            pl.BlockSpec((1, dv), lambda i: (0, 0)),
            pl.BlockSpec((dvt, d), lambda i: (0, 0)),
        ],
        out_specs=pl.BlockSpec((tm, d), lambda i: (i, 0)),
        compiler_params=_params(("arbitrary",), vmem),
        name="gla_out",
    )(o_f, o_b, r, x, mod, onorm_g.reshape(1, dv), w_o)


def _ffn_kernel(xp_ref, x_ref, xn_ref, mod_ref, g_ref, wg_ref, wv_ref, cwg_ref, cwv_ref, cbg_ref, cbv_ref,
                wd_ref, o_ref, h_scr, ug_scr, uv_scr, *, tm, seq):
    i = pl.program_id(0)
    j = pl.program_id(1)
    halo = SUBLANES

    @pl.when(j == 0)
    def _():
        g, shift, scale = g_ref[...], mod_ref[0, 3:4, :], mod_ref[0, 4:5, :]
        first = (i * tm) % seq == 0
        last = ((i + 1) * tm) % seq == 0
        hp = _norm_mod(xp_ref[...], g, shift, scale)
        hn = _norm_mod(xn_ref[...], g, shift, scale)
        h_scr[0:halo, :] = jnp.where(first, 0.0, hp).astype(BF16)
        h_scr[halo:halo + tm, :] = _norm_mod(x_ref[...], g, shift, scale).astype(BF16)
        h_scr[halo + tm:2 * halo + tm, :] = jnp.where(last, 0.0, hn).astype(BF16)
        o_ref[...] = jnp.zeros_like(o_ref)

    h = h_scr[...]
    ug_scr[...] = _dot(h, wg_ref[...])
    uv_scr[...] = _dot(h, wv_ref[...])

    def conv(u_scr, cw_ref, cb_ref):
        return (cb_ref[...]
                + cw_ref[0:1, :] * u_scr[halo - 1:halo - 1 + tm, :]
                + cw_ref[1:2, :] * u_scr[halo:halo + tm, :]
                + cw_ref[2:3, :] * u_scr[halo + 1:halo + 1 + tm, :])

    act = (_silu(conv(ug_scr, cwg_ref, cbg_ref)) * conv(uv_scr, cwv_ref, cbv_ref)).astype(BF16)
    o_ref[...] += _dot(act, wd_ref[...])

    @pl.when(j == pl.num_programs(1) - 1)
    def _():
        o_ref[...] = x_ref[...] + mod_ref[0, 5:6, :] * o_ref[...]


def _conv_ffn(x, mod, g, w_up, conv_w, conv_b, w_down, *, tm, tf, seq, rows_per_mod):
    n, d = x.shape
    dff = w_down.shape[0]
    nf = dff // tf
    halo = SUBLANES
    r = tm // halo
    nhalo = n // halo
    conv_b = conv_b.reshape(1, 2 * dff)
    vmem = (2 * (2 * tm * d * 4 + 2 * d * tf * 2 + tf * d * 2) + (tm + 2 * halo) * d * 2
            + 2 * (tm + 2 * halo) * tf * 4 + 6 * tm * tf * 4)
    return pl.pallas_call(
        functools.partial(_ffn_kernel, tm=tm, seq=seq),
        out_shape=jax.ShapeDtypeStruct((n, d), F32),
        grid=(n // tm, nf),
        in_specs=[
            pl.BlockSpec((halo, d), lambda i, j: (jnp.maximum(i * r - 1, 0), 0)),
            pl.BlockSpec((tm, d), lambda i, j: (i, 0)),
            pl.BlockSpec((halo, d), lambda i, j: (jnp.minimum((i + 1) * r, nhalo - 1), 0)),
            pl.BlockSpec((1, MOD_ROWS, d), lambda i, j: ((i * tm) // rows_per_mod, 0, 0)),
            pl.BlockSpec((1, d), lambda i, j: (0, 0)),
            pl.BlockSpec((d, tf), lambda i, j: (0, j)),
            pl.BlockSpec((d, tf), lambda i, j: (0, nf + j)),
            pl.BlockSpec((CONV_W, tf), lambda i, j: (0, j)),
            pl.BlockSpec((CONV_W, tf), lambda i, j: (0, nf + j)),
            pl.BlockSpec((1, tf), lambda i, j: (0, j)),
            pl.BlockSpec((1, tf), lambda i, j: (0, nf + j)),
            pl.BlockSpec((tf, d), lambda i, j: (j, 0)),
        ],
        out_specs=pl.BlockSpec((tm, d), lambda i, j: (i, 0)),
        scratch_shapes=[
            pltpu.VMEM((tm + 2 * halo, d), BF16),
            pltpu.VMEM((tm + 2 * halo, tf), F32),
            pltpu.VMEM((tm + 2 * halo, tf), F32),
        ],
        compiler_params=_params(("arbitrary", "arbitrary"), vmem),
        name="conv_ffn",
    )(x, x, x, mod, g.reshape(1, d), w_up, w_up, conv_w, conv_w, conv_b, conv_b, w_down)


def _final_norm_kernel(x_ref, g_ref, o_ref):
    x = x_ref[...]
    o_ref[...] = (x * lax.rsqrt(jnp.mean(x * x, axis=-1, keepdims=True) + EPS)) * g_ref[...]


def _final_norm(x, g, *, tm):
    n, d = x.shape
    return pl.pallas_call(
        _final_norm_kernel,
        out_shape=jax.ShapeDtypeStruct((n, d), F32),
        grid=(n // tm,),
        in_specs=[pl.BlockSpec((tm, d), lambda i: (i, 0)), pl.BlockSpec((1, d), lambda i: (0, 0))],
        out_specs=pl.BlockSpec((tm, d), lambda i: (i, 0)),
        compiler_params=_params(("arbitrary",), 4 * tm * d * 4),
        name="final_norm",
    )(x, g.reshape(1, d))


def _rope_tables(seq):
    half = HEAD_DIM // 2
    rows = seq // GRID_W
    row = jnp.repeat(jnp.arange(rows, dtype=F32), GRID_W)
    col = jnp.tile(jnp.arange(GRID_W, dtype=F32), rows)
    inv = ROPE_BASE ** (-jnp.arange(0, half, 2, dtype=F32) / half)
    ang_r = row[:, None] * inv[None, :]
    ang_c = col[:, None] * inv[None, :]
    zero = jnp.zeros_like(ang_r)
    cos = jnp.concatenate([jnp.cos(ang_r)] * 2 + [jnp.cos(ang_c)] * 2, axis=1)
    sa = jnp.concatenate([-jnp.sin(ang_r), zero, -jnp.sin(ang_c), zero], axis=1)
    sb = jnp.concatenate([zero, jnp.sin(ang_r), zero, jnp.sin(ang_c)], axis=1)
    return cos, sa, sb


def _tile(n, target):
    t = min(n, target)
    while n % t:
        t //= 2
    return t


def kernel(x, c, ctx, c_ctx, ada_w, ada_b, norm_mix_g, norm_ffn_g, ffn_w_up, ffn_conv_w, ffn_conv_b, ffn_w_down,
           attn_w_qkv, attn_sink, attn_w_o, gla_w_in, gla_gf_w1, gla_gf_w2, gla_gf_b, gla_gb_w1, gla_gb_w2,
           gla_gb_b, gla_onorm_g, gla_w_o, final_norm_g):
    batch, seq, d = x.shape
    nctx = ctx.shape[1]
    depth = ada_w.shape[0]
    n_lat, n_ctx = batch * seq, batch * nctx
    dk = gla_gf_w2.shape[2] // GLA_HEADS
    dv = gla_onorm_g.shape[1]
    nq = ATT_HEADS * HEAD_DIM
    assert seq % WINDOW == 0 and nctx % WINDOW == 0 and d % LANES == 0

    n_cond = batch + 1
    pad = (-n_cond) % SUBLANES
    cc = jnp.concatenate([c, c_ctx[None, :], jnp.zeros((pad, d), F32)], axis=0)
    mods = _modulations(cc, ada_w, ada_b)
    mods = mods.reshape(depth, n_cond + pad, N_MOD, d)
    mods = jnp.pad(mods, ((0, 0), (0, 0), (0, MOD_ROWS - N_MOD), (0, 0)))

    rope = _rope_tables(seq)
    xl = x.reshape(n_lat, d)
    xc = ctx.reshape(n_ctx, d)

    tm_l, tm_c = _tile(seq, 512), _tile(nctx, 512)
    tm_pc = _tile(n_ctx, 512)
    tq_l = _tile(seq, 128)
    gla_chunk = _tile(nctx, 128)
    tf = 512
    nkv = ATT_KV_HEADS * HEAD_DIM
    hk, hv = GLA_HEADS * dk, GLA_HEADS * dv

    for i in range(depth):
        last = i == depth - 1
        mod_l = mods[i, :batch]
        mod_c = mods[i, batch:batch + 1]
        j = i // N_MIXERS
        if i % N_MIXERS == 0:
            w_qkv = attn_w_qkv[j].astype(BF16)
            w_o = attn_w_o[j].astype(BF16)
            seg = [(nq + 2 * nkv, BF16)]
            qkv_l, = _project(xl, mod_l, norm_mix_g[i], w_qkv, seg, rows_per_mod=seq, tm=tm_l,
                              rope=rope, seq=seq, rope_cols=nq + nkv)
            qkv_c, = _project(xc, mod_c, norm_mix_g[i], w_qkv, seg, rows_per_mod=n_ctx, tm=tm_pc)
            xl = _attention(qkv_l, qkv_c, xl, mod_l, attn_sink[j], w_o, band=True, tq=tq_l, seq=seq, nctx=nctx,
                            rows_per_mod=seq)
            if not last:
                xc = _attention(qkv_c, None, xc, mod_c, attn_sink[j], w_o, band=False, tq=nctx, seq=nctx, nctx=nctx,
                                rows_per_mod=n_ctx)
        else:
            w_qk = gla_w_in[j, :, :2 * hk].astype(BF16)
            w_vr = gla_w_in[j, :, 2 * hk:].astype(BF16)
            w_o = gla_w_o[j].astype(BF16)
            rank = gla_gf_w1.shape[2]
            w1 = jnp.zeros((d, LANES), F32).at[:, :rank].set(gla_gf_w1[j]).at[:, rank:2 * rank].set(gla_gb_w1[j])
            w2f = jnp.zeros((LANES, hk), F32).at[:rank].set(gla_gf_w2[j])
            w2b = jnp.zeros((LANES, hk), F32).at[rank:2 * rank].set(gla_gb_w2[j])
            gate = (w1.astype(BF16), w2f.astype(BF16), w2b.astype(BF16), gla_gf_b[j], gla_gb_b[j])
            streams = []
            for xs, mod, rpm, tm in ((xc, mod_c, n_ctx, tm_pc), (xl, mod_l, seq, tm_l)):
                qk, gf, gb = _project(xs, mod, norm_mix_g[i], w_qk, [(2 * hk, F32)], rows_per_mod=rpm, tm=tm,
                                      gate=gate)
                v, r = _project(xs, mod, norm_mix_g[i], w_vr, [(hv, BF16), (hv, F32)], rows_per_mod=rpm, tm=tm)
                streams.append((qk, v, r, gf, gb))
            s0 = jnp.zeros((batch, GLA_HEADS, dk, dv), F32)
            scan = functools.partial(_gla_scan, batch=batch, chunk=gla_chunk, dk=dk, dv=dv)
            (qk_c, v_c, r_c, gf_c, gb_c), (qk_l, v_l, r_l, gf_l, gb_l) = streams
            of_c, ob_c, sf, sb = scan(qk_c, v_c, gf_c, gb_c, s0, s0, length=nctx)
            of_l, ob_l, _, _ = scan(qk_l, v_l, gf_l, gb_l, sf, sb, length=seq)
            xl = _gla_output(of_l, ob_l, r_l, xl, mod_l, gla_onorm_g[j], w_o, tm=tm_l, rows_per_mod=seq, dv=dv)
            if not last:
                xc = _gla_output(of_c, ob_c, r_c, xc, mod_c, gla_onorm_g[j], w_o, tm=tm_c, rows_per_mod=n_ctx, dv=dv)
        w_up = ffn_w_up[i].astype(BF16)
        w_down = ffn_w_down[i].astype(BF16)
        xl = _conv_ffn(xl, mod_l, norm_ffn_g[i], w_up, ffn_conv_w[i], ffn_conv_b[i], w_down,
                       tm=tm_l, tf=tf, seq=seq, rows_per_mod=seq)
        if not last:
            xc = _conv_ffn(xc, mod_c, norm_ffn_g[i], w_up, ffn_conv_w[i], ffn_conv_b[i], w_down,
                           tm=tm_c, tf=tf, seq=nctx, rows_per_mod=n_ctx)
    return _final_norm(xl, final_norm_g, tm=tm_l).reshape(batch, seq, d)
```

```python
import functools

import jax
import jax.numpy as jnp
from jax import lax
from jax.experimental import pallas as pl
from jax.experimental.pallas import tpu as pltpu

GRID_W = 64
N_MIXERS = 2
ATT_HEADS = 16
ATT_KV_HEADS = 4
ATT_GROUP = ATT_HEADS // ATT_KV_HEADS
HEAD_DIM = 128
WINDOW = 128
ROPE_BASE = 10000.0
GLA_HEADS = 4
GLA_GATE_RANK = 16
GLA_GATE_NORM = 16.0
CONV_W = 3
EPS = 1e-6
N_MOD = 6
MOD_ROWS = 8

LANES = 128
SUBLANES = 8
VMEM_LIMIT_CAP = 60 * 1024 * 1024

BF16 = jnp.bfloat16
F32 = jnp.float32


def _dot(a, b):
    return jnp.dot(a, b, preferred_element_type=F32)


def _dot_nt(a, b):
    return lax.dot_general(a, b, (((1,), (1,)), ((), ())), preferred_element_type=F32)


def _dot_tn(a, b):
    return lax.dot_general(a, b, (((0,), (0,)), ((), ())), preferred_element_type=F32)


def _params(semantics, vmem_bytes):
    limit = min(VMEM_LIMIT_CAP, int(vmem_bytes * 1.25) + (4 << 20))
    return pltpu.CompilerParams(dimension_semantics=semantics, vmem_limit_bytes=limit)


def _sigmoid(x):
    return 1.0 / (1.0 + jnp.exp(-x))


def _silu(x):
    return x * _sigmoid(x)


def _norm_mod(x, g, shift, scale):
    y = x * lax.rsqrt(jnp.mean(x * x, axis=-1, keepdims=True) + EPS)
    return (y * g) * (1.0 + scale) + shift


def _mod_kernel(c_ref, w_ref, b_ref, o_ref):
    s = _silu(c_ref[...]).astype(BF16)
    o_ref[0] = _dot(s, w_ref[0].astype(BF16)) + b_ref[0]


def _modulations(cc, ada_w, ada_b):
    depth, d, n = ada_w.shape
    r = cc.shape[0]
    tn = 1024
    vmem = 2 * (d * tn * 4 + r * tn * 4 + tn * 4) + r * d * 4 + d * tn * 2
    return pl.pallas_call(
        _mod_kernel,
        out_shape=jax.ShapeDtypeStruct((depth, r, n), F32),
        grid=(depth, n // tn),
        in_specs=[
            pl.BlockSpec((r, d), lambda l, j: (0, 0)),
            pl.BlockSpec((1, d, tn), lambda l, j: (l, 0, j)),
            pl.BlockSpec((1, 1, tn), lambda l, j: (l, 0, j)),
        ],
        out_specs=pl.BlockSpec((1, r, tn), lambda l, j: (l, 0, j)),
        compiler_params=_params(("arbitrary", "arbitrary"), vmem),
        name="adaln_mod",
    )(cc, ada_w, ada_b.reshape(depth, 1, n))


def _log_sigmoid(z):
    return -(jnp.maximum(-z, 0.0) + jnp.log1p(jnp.exp(-jnp.abs(z))))


def _proj_kernel(*refs, segments, rope_cols, gated, col_chunk):
    n_out = len(segments) + (2 if gated else 0)
    x_ref, mod_ref, g_ref, w_ref = refs[:4]
    extra = refs[4:len(refs) - n_out - 2]
    outs = refs[len(refs) - n_out - 2:len(refs) - 2]
    h_even, h_odd = refs[-2:]
    s = pl.program_id(0)

    @pl.when(s == 0)
    def _():
        h_odd[...] = jnp.zeros_like(h_odd)

    def body(h_new, h_old):
        h = h_old[...]
        if gated:
            w1_ref, w2f_ref, w2b_ref, bf_ref, bb_ref = extra
            t = _dot(h, w1_ref[...]).astype(BF16)
            for w2_ref, b_ref, o_ref in ((w2f_ref, bf_ref, outs[-2]), (w2b_ref, bb_ref, outs[-1])):
                o_ref[...] = _log_sigmoid(_dot(t, w2_ref[...]) + b_ref[...]) / GLA_GATE_NORM
        if rope_cols:
            cos, sa, sb = extra[0][...], extra[1][...], extra[2][...]
        col = 0
        for (width, _), y_ref in zip(segments, outs):
            for c0 in range(0, width, col_chunk):
                y = _dot(h, w_ref[:, col + c0:col + c0 + col_chunk])
                if col + c0 < rope_cols:
                    parts = []
                    for p in range(col_chunk // HEAD_DIM):
                        yh = y[:, p * HEAD_DIM:(p + 1) * HEAD_DIM]
                        if col + c0 + p * HEAD_DIM < rope_cols:
                            up = pltpu.roll(yh, HEAD_DIM - HEAD_DIM // 4, axis=1)
                            dn = pltpu.roll(yh, HEAD_DIM // 4, axis=1)
                            yh = yh * cos + up * sa + dn * sb
                        parts.append(yh)
                    y = jnp.concatenate(parts, axis=1)
                y_ref[:, c0:c0 + col_chunk] = y.astype(y_ref.dtype)
            col += width
        x = x_ref[...]
        gs = g_ref[...] * (1.0 + mod_ref[0, 1:2, :])
        r = lax.rsqrt(jnp.mean(x * x, axis=-1, keepdims=True) + EPS)
        h_new[...] = ((x * r) * gs + mod_ref[0, 0:1, :]).astype(BF16)

    @pl.when(s % 2 == 0)
    def _():
        body(h_even, h_odd)

    @pl.when(s % 2 == 1)
    def _():
        body(h_odd, h_even)


def _project(x, mod, g, w, segments, *, rows_per_mod, tm, rope=None, seq=None, rope_cols=0, gate=None):
    n, d = x.shape
    w_stack, layer = w
    m = w_stack.shape[2]
    nrow = n // tm
    col_chunk = 1024 if all(wd % 1024 == 0 for wd, _ in segments) else 512

    def cur(s):
        return jnp.minimum(s, nrow - 1)

    def prev(s):
        return jnp.maximum(s - 1, 0)

    in_specs = [
        pl.BlockSpec((tm, d), lambda s: (cur(s), 0)),
        pl.BlockSpec((1, MOD_ROWS, d), lambda s: ((cur(s) * tm) // rows_per_mod, 0, 0)),
        pl.BlockSpec((1, d), lambda s: (0, 0)),
        pl.BlockSpec((None, d, m), lambda s: (layer, 0, 0), pipeline_mode=pl.Buffered(1)),
    ]
    args = [x, mod, g.reshape(1, d), w_stack]
    out_shape = [jax.ShapeDtypeStruct((n, wd), dt) for wd, dt in segments]
    out_specs = [pl.BlockSpec((tm, wd), lambda s: (prev(s), 0)) for wd, _ in segments]
    vmem = (2 * tm * d * 4 + d * m * 2 + 2 * sum(tm * wd * jnp.dtype(dt).itemsize for wd, dt in segments)
            + 2 * tm * d * 2 + 3 * tm * col_chunk * 4 + 2 * tm * d * 4)
    if rope is not None:
        tiles_per_seq = seq // tm
        for tab in rope:
            in_specs.append(pl.BlockSpec((tm, HEAD_DIM), lambda s: (prev(s) % tiles_per_seq, 0)))
            args.append(tab)
        vmem += 6 * tm * HEAD_DIM * 4
    if gate is not None:
        w1, w2f, w2b, bf, bb = gate
        dk = w2f.shape[1]
        in_specs += [
            pl.BlockSpec(w1.shape, lambda s: (0, 0)),
            pl.BlockSpec(w2f.shape, lambda s: (0, 0)),
            pl.BlockSpec(w2b.shape, lambda s: (0, 0)),
            pl.BlockSpec((1, dk), lambda s: (0, 0)),
            pl.BlockSpec((1, dk), lambda s: (0, 0)),
        ]
        args += [w1, w2f, w2b, bf.reshape(1, dk), bb.reshape(1, dk)]
        out_shape += [jax.ShapeDtypeStruct((n, dk), F32)] * 2
        out_specs += [pl.BlockSpec((tm, dk), lambda s: (prev(s), 0))] * 2
        vmem += 6 * tm * dk * 4 + 2 * (d * LANES * 2 + 2 * LANES * dk * 2)
    return pl.pallas_call(
        functools.partial(_proj_kernel, segments=tuple(segments), rope_cols=rope_cols, gated=gate is not None,
                          col_chunk=col_chunk),
        out_shape=out_shape,
        grid=(nrow + 1,),
        in_specs=in_specs,
        out_specs=out_specs,
        scratch_shapes=[pltpu.VMEM((tm, d), BF16), pltpu.VMEM((tm, d), BF16)],
        compiler_params=_params(("arbitrary",), vmem),
        name="proj",
    )(*args)


def _attend(qg, kall, vall, mask, sink_col):
    s = _dot_nt(qg, kall) * (HEAD_DIM ** -0.5)
    if mask is not None:
        s = jnp.where(mask, s, -jnp.inf)
    mx = jnp.maximum(jnp.max(s, axis=-1, keepdims=True), sink_col)
    e = jnp.exp(s - mx)
    denom = jnp.sum(e, axis=-1, keepdims=True) + jnp.exp(sink_col - mx)
    return _dot(e.astype(BF16), vall) / denom


def _sink_column(sink_ref, kv, rows):
    cols = [jnp.broadcast_to(sink_ref[0:1, kv * ATT_GROUP + h:kv * ATT_GROUP + h + 1], (rows, 1))
            for h in range(ATT_GROUP)]
    return jnp.concatenate(cols, axis=0)


def _attn_kernel(*refs, band, tq, seq):
    if band:
        (q_ref, kp_ref, kc_ref, kn_ref, vp_ref, vc_ref, vn_ref, kx_ref, vx_ref,
         x_ref, mod_ref, sink_ref, wo_ref, o_ref) = refs
    else:
        q_ref, kx_ref, vx_ref, x_ref, mod_ref, sink_ref, wo_ref, o_ref = refs
    blk = WINDOW
    nsub = tq // blk if band else 1
    rows = blk if band else tq
    if band:
        kwin = jnp.concatenate([kp_ref[...], kc_ref[...], kn_ref[...]], axis=0)
        vwin = jnp.concatenate([vp_ref[...], vc_ref[...], vn_ref[...]], axis=0)
        t0 = (pl.program_id(0) * tq) % seq
        nctx = kx_ref.shape[0]
        ncol = 3 * blk + nctx
        qi = lax.broadcasted_iota(jnp.int32, (ATT_GROUP * blk, ncol), 0) & (blk - 1)
        kj = lax.broadcasted_iota(jnp.int32, (ATT_GROUP * blk, ncol), 1)
        in_band = jnp.abs(qi - kj + blk) <= WINDOW
    outs = []
    for sb in range(nsub):
        heads = [None] * ATT_HEADS
        for kv in range(ATT_KV_HEADS):
            lo = kv * HEAD_DIM
            qg = jnp.concatenate(
                [q_ref[sb * rows:(sb + 1) * rows, (kv * ATT_GROUP + h) * HEAD_DIM:(kv * ATT_GROUP + h + 1) * HEAD_DIM]
                 for h in range(ATT_GROUP)], axis=0)
            if band:
                kall = jnp.concatenate([kwin[sb * blk:sb * blk + 3 * blk, lo:lo + HEAD_DIM],
                                        kx_ref[:, lo:lo + HEAD_DIM]], axis=0)
                vall = jnp.concatenate([vwin[sb * blk:sb * blk + 3 * blk, lo:lo + HEAD_DIM],
                                        vx_ref[:, lo:lo + HEAD_DIM]], axis=0)
                kpos = t0 + (sb - 1) * blk + kj
                mask = (kj >= 3 * blk) | (in_band & (kpos >= 0) & (kpos < seq))
            else:
                kall = kx_ref[:, lo:lo + HEAD_DIM]
                vall = vx_ref[:, lo:lo + HEAD_DIM]
                mask = None
            og = _attend(qg, kall, vall, mask, _sink_column(sink_ref, kv, rows))
            for h in range(ATT_GROUP):
                heads[kv * ATT_GROUP + h] = og[h * rows:(h + 1) * rows].astype(BF16)
        outs.append(jnp.concatenate(heads, axis=1))
    o = outs[0] if len(outs) == 1 else jnp.concatenate(outs, axis=0)
    o_ref[...] = x_ref[...] + mod_ref[0, 2:3, :] * _dot(o, wo_ref[...])


def _attention(qkv, qkv_ctx, x, mod, sink, w_o, *, band, tq, seq, nctx, rows_per_mod):
    n, d = x.shape
    wo_stack, layer = w_o
    nq = ATT_HEADS * HEAD_DIM
    nkv = ATT_KV_HEADS * HEAD_DIM
    kcol, vcol = nq // nkv, nq // nkv + 1
    blk = WINDOW
    nblk = n // blk
    r = tq // blk
    q_spec = pl.BlockSpec((tq, nq), lambda i: (i, 0))
    tail_specs = [
        pl.BlockSpec((tq, d), lambda i: (i, 0)),
        pl.BlockSpec((1, MOD_ROWS, d), lambda i: ((i * tq) // rows_per_mod, 0, 0)),
        pl.BlockSpec((1, ATT_HEADS), lambda i: (0, 0)),
        pl.BlockSpec((None, nq, d), lambda i: (layer, 0, 0)),
    ]
    tail_args = [x, mod, sink.reshape(1, ATT_HEADS), wo_stack]
    if band:
        def ctx_idx(col):
            return lambda i: ((i * tq) // seq, col)
        in_specs = [
            q_spec,
            pl.BlockSpec((blk, nkv), lambda i: (jnp.maximum(i * r - 1, 0), kcol)),
            pl.BlockSpec((tq, nkv), lambda i: (i, kcol)),
            pl.BlockSpec((blk, nkv), lambda i: (jnp.minimum((i + 1) * r, nblk - 1), kcol)),
            pl.BlockSpec((blk, nkv), lambda i: (jnp.maximum(i * r - 1, 0), vcol)),
            pl.BlockSpec((tq, nkv), lambda i: (i, vcol)),
            pl.BlockSpec((blk, nkv), lambda i: (jnp.minimum((i + 1) * r, nblk - 1), vcol)),
            pl.BlockSpec((nctx, nkv), ctx_idx(kcol)),
            pl.BlockSpec((nctx, nkv), ctx_idx(vcol)),
        ] + tail_specs
        args = [qkv] * 7 + [qkv_ctx] * 2 + tail_args
        ncol = 3 * blk + nctx
    else:
        in_specs = [
            q_spec,
            pl.BlockSpec((tq, nkv), lambda i: (i, kcol)),
            pl.BlockSpec((tq, nkv), lambda i: (i, vcol)),
        ] + tail_specs
        args = [qkv] * 3 + tail_args
        ncol = tq
    vmem = (2 * (tq * nq * 2 + 4 * (tq + 2 * blk + nctx) * nkv * 2 + 2 * tq * d * 4 + nq * d * 2)
            + 6 * ATT_GROUP * blk * ncol * 4 + 2 * tq * d * 4)
    return pl.pallas_call(
        functools.partial(_attn_kernel, band=band, tq=tq, seq=seq),
        out_shape=jax.ShapeDtypeStruct((n, d), F32),
        grid=(n // tq,),
        in_specs=in_specs,
        out_specs=pl.BlockSpec((tq, d), lambda i: (i, 0)),
        compiler_params=_params(("arbitrary",), vmem),
        name="attn_band" if band else "attn_ctx",
    )(*args)


def _gla_scan_kernel(qf_ref, kf_ref, vf_ref, gf_ref, qb_ref, kb_ref, vb_ref, gb_ref, s0f_ref, s0b_ref,
                     of_ref, ob_ref, sf_ref, sb_ref, *, chunk, dk, dv, q_scale):
    c = pl.program_id(1)

    @pl.when(c == 0)
    def _():
        sf_ref[...] = s0f_ref[...]
        sb_ref[...] = s0b_ref[...]

    row = lax.broadcasted_iota(jnp.int32, (chunk, chunk), 0)
    col = lax.broadcasted_iota(jnp.int32, (chunk, chunk), 1)
    directions = (
        (qf_ref, kf_ref, vf_ref, gf_ref, of_ref, sf_ref, col <= row, chunk - 1),
        (qb_ref, kb_ref, vb_ref, gb_ref, ob_ref, sb_ref, col >= row, 0),
    )
    for q_ref, k_ref, v_ref, g_ref, o_ref, s_ref, tri, last in directions:
        tri_b = tri.astype(BF16)
        g = g_ref[...]
        g_hi = g.astype(BF16)
        g_lo = (g - g_hi.astype(F32)).astype(BF16)
        b = _dot(tri_b, g_hi) + _dot(tri_b, g_lo)
        b_tot = b[last:last + 1]
        b_mid = b[chunk // 2:chunk // 2 + 1]

        q = q_ref[...] * q_scale
        k = k_ref[...]
        v = v_ref[...]
        qa = (q * jnp.exp(b - b_mid)).astype(BF16)
        ka = (k * jnp.exp(b_mid - b)).astype(BF16)
        qs = (q * jnp.exp(b)).astype(BF16)
        ks = (k * jnp.exp(b_tot - b)).astype(BF16)
        decay = jnp.exp(jnp.broadcast_to(b_tot, (LANES, b.shape[1]))).T

        for h in range(GLA_HEADS):
            kc = slice(h * dk, (h + 1) * dk)
            vc = slice(h * dv, (h + 1) * dv)
            a = jnp.where(tri, _dot_nt(qa[:, kc], ka[:, kc]), 0.0).astype(BF16)
            s = s_ref[0, h]
            o_ref[:, vc] = _dot(a, v[:, vc]) + _dot(qs[:, kc], s.astype(BF16))
            upd = _dot_tn(ks[:, kc], v[:, vc])
            s_ref[0, h] = jnp.concatenate(
                [s[:, t * LANES:(t + 1) * LANES] * decay[kc] for t in range(dv // LANES)], axis=1) + upd


def _gla_scan(qk, v, g_f, g_b, s0_f, s0_b, *, batch, length, chunk, dk, dv):
    nc = length // chunk
    hk, hv = GLA_HEADS * dk, GLA_HEADS * dv

    def fwd(b, c):
        return b * nc + c

    def bwd(b, c):
        return b * nc + nc - 1 - c

    state_spec = pl.BlockSpec((1, GLA_HEADS, dk, dv), lambda b, c: (b, 0, 0, 0))
    in_specs = []
    for rows in (fwd, bwd):
        in_specs += [
            pl.BlockSpec((chunk, hk), lambda b, c, rows=rows: (rows(b, c), 0)),
            pl.BlockSpec((chunk, hk), lambda b, c, rows=rows: (rows(b, c), 1)),
            pl.BlockSpec((chunk, hv), lambda b, c, rows=rows: (rows(b, c), 0)),
            pl.BlockSpec((chunk, hk), lambda b, c, rows=rows: (rows(b, c), 0)),
        ]
    in_specs += [state_spec, state_spec]
    state_bytes = GLA_HEADS * dk * dv * 4
    vmem = (2 * 2 * (3 * chunk * hk * 4 + chunk * hv * 2 + chunk * hv * 4) + 8 * state_bytes
            + 16 * chunk * hk * 4 + 4 * dk * dv * 4)
    return pl.pallas_call(
        functools.partial(_gla_scan_kernel, chunk=chunk, dk=dk, dv=dv, q_scale=dk ** -0.5),
        out_shape=[jax.ShapeDtypeStruct((batch * length, hv), F32)] * 2
        + [jax.ShapeDtypeStruct(s0_f.shape, F32)] * 2,
        grid=(batch, nc),
        in_specs=in_specs,
        out_specs=[
            pl.BlockSpec((chunk, hv), lambda b, c: (fwd(b, c), 0)),
            pl.BlockSpec((chunk, hv), lambda b, c: (bwd(b, c), 0)),
            state_spec,
            state_spec,
        ],
        compiler_params=_params(("arbitrary", "arbitrary"), vmem),
        name="gla_scan",
    )(qk, qk, v, g_f, qk, qk, v, g_b, s0_f, s0_b)


def _gla_out_kernel(of_ref, ob_ref, r_ref, x_ref, mod_ref, g_ref, wo_ref, o_ref, *, dv):
    o = of_ref[...] + ob_ref[...]
    parts = []
    for h in range(o.shape[1] // dv):
        oh = o[:, h * dv:(h + 1) * dv]
        parts.append((oh * lax.rsqrt(jnp.mean(oh * oh, axis=-1, keepdims=True) + EPS)) * g_ref[...])
    gated = (jnp.concatenate(parts, axis=1) * _silu(r_ref[...])).astype(BF16)
    o_ref[...] = x_ref[...] + mod_ref[0, 2:3, :] * _dot(gated, wo_ref[...])


def _gla_output(o_f, o_b, r, x, mod, onorm_g, w_o, *, tm, rows_per_mod, dv):
    n, d = x.shape
    wo_stack, layer = w_o
    dvt = o_f.shape[1]
    vmem = 2 * (5 * tm * dvt * 4 + dvt * d * 2) + 4 * tm * dvt * 4
    return pl.pallas_call(
        functools.partial(_gla_out_kernel, dv=dv),
        out_shape=jax.ShapeDtypeStruct((n, d), F32),
        grid=(n // tm,),
        in_specs=[
            pl.BlockSpec((tm, dvt), lambda i: (i, 0)),
            pl.BlockSpec((tm, dvt), lambda i: (i, 0)),
            pl.BlockSpec((tm, dvt), lambda i: (i, 0)),
            pl.BlockSpec((tm, d), lambda i: (i, 0)),
            pl.BlockSpec((1, MOD_ROWS, d), lambda i: ((i * tm) // rows_per_mod, 0, 0)),
            pl.BlockSpec((1, dv), lambda i: (0, 0)),
            pl.BlockSpec((None, dvt, d), lambda i: (layer, 0, 0)),
        ],
        out_specs=pl.BlockSpec((tm, d), lambda i: (i, 0)),
        compiler_params=_params(("arbitrary",), vmem),
        name="gla_out",
    )(o_f, o_b, r, x, mod, onorm_g.reshape(1, dv), wo_stack)


def _ffn_kernel(xp_ref, x_ref, xn_ref, mod_ref, g_ref, fg_ref, wg_ref, wv_ref, cwg_ref, cwv_ref, cbg_ref, cbv_ref,
                wd_ref, o_ref, h_scr, ug_scr, uv_scr, act_scr, *, tm, seq, final_norm):
    i = pl.program_id(0)
    j = pl.program_id(1)
    halo = SUBLANES
    rows = min(tm, 256)

    @pl.when(j == 0)
    def _():
        g, shift, scale = g_ref[...], mod_ref[0, 3:4, :], mod_ref[0, 4:5, :]
        first = (i * tm) % seq == 0
        last = ((i + 1) * tm) % seq == 0
        hp = _norm_mod(xp_ref[...], g, shift, scale)
        hn = _norm_mod(xn_ref[...], g, shift, scale)
        h_scr[0:halo, :] = jnp.where(first, 0.0, hp).astype(BF16)
        for r0 in range(0, tm, rows):
            h_scr[halo + r0:halo + r0 + rows, :] = _norm_mod(x_ref[r0:r0 + rows, :], g, shift, scale).astype(BF16)
        h_scr[halo + tm:2 * halo + tm, :] = jnp.where(last, 0.0, hn).astype(BF16)
        o_ref[...] = jnp.zeros_like(o_ref)

    h = h_scr[...]
    ug_scr[...] = _dot(h, wg_ref[...])
    uv_scr[...] = _dot(h, wv_ref[...])

    def conv(u_scr, cw_ref, cb_ref, r0):
        return (cb_ref[...]
                + cw_ref[0:1, :] * u_scr[halo - 1 + r0:halo - 1 + r0 + rows, :]
                + cw_ref[1:2, :] * u_scr[halo + r0:halo + r0 + rows, :]
                + cw_ref[2:3, :] * u_scr[halo + 1 + r0:halo + 1 + r0 + rows, :])

    for r0 in range(0, tm, rows):
        act_scr[r0:r0 + rows, :] = (_silu(conv(ug_scr, cwg_ref, cbg_ref, r0))
                                    * conv(uv_scr, cwv_ref, cbv_ref, r0)).astype(BF16)
    act = act_scr[...]
    for c0 in range(0, o_ref.shape[1], 512):
        o_ref[:, c0:c0 + 512] += _dot(act, wd_ref[:, c0:c0 + 512])

    @pl.when(j == pl.num_programs(1) - 1)
    def _():
        er = min(tm, 128)

        def residual(c, carry):
            sl = pl.ds(pl.multiple_of(c * er, er), er)
            y = x_ref[sl, :] + mod_ref[0, 5:6, :] * o_ref[sl, :]
            if final_norm:
                y = (y * lax.rsqrt(jnp.mean(y * y, axis=-1, keepdims=True) + EPS)) * fg_ref[...]
            o_ref[sl, :] = y
            return carry

        lax.fori_loop(0, tm // er, residual, 0)


def _conv_ffn(x, mod, g, w_up, conv_w, conv_b, w_down, *, tm, tf, seq, rows_per_mod, final_g=None):
    n, d = x.shape
    (up_stack, layer), (down_stack, _) = w_up, w_down
    dff = down_stack.shape[1]
    nf = dff // tf
    halo = SUBLANES
    r = tm // halo
    nhalo = n // halo
    conv_b = conv_b.reshape(1, 2 * dff)
    fg = (g if final_g is None else final_g).reshape(1, d)
    vmem = (2 * (2 * tm * d * 4 + 2 * d * tf * 2 + tf * d * 2) + (tm + 2 * halo) * d * 2
            + 2 * (tm + 2 * halo) * tf * 4 + tm * tf * 2 + 8 * min(tm, 256) * max(tf, d) * 4)
    return pl.pallas_call(
        functools.partial(_ffn_kernel, tm=tm, seq=seq, final_norm=final_g is not None),
        out_shape=jax.ShapeDtypeStruct((n, d), F32),
        grid=(n // tm, nf),
        in_specs=[
            pl.BlockSpec((halo, d), lambda i, j: (jnp.maximum(i * r - 1, 0), 0)),
            pl.BlockSpec((tm, d), lambda i, j: (i, 0)),
            pl.BlockSpec((halo, d), lambda i, j: (jnp.minimum((i + 1) * r, nhalo - 1), 0)),
            pl.BlockSpec((1, MOD_ROWS, d), lambda i, j: ((i * tm) // rows_per_mod, 0, 0)),
            pl.BlockSpec((1, d), lambda i, j: (0, 0)),
            pl.BlockSpec((1, d), lambda i, j: (0, 0)),
            pl.BlockSpec((None, d, tf), lambda i, j: (layer, 0, j)),
            pl.BlockSpec((None, d, tf), lambda i, j: (layer, 0, nf + j)),
            pl.BlockSpec((CONV_W, tf), lambda i, j: (0, j)),
            pl.BlockSpec((CONV_W, tf), lambda i, j: (0, nf + j)),
            pl.BlockSpec((1, tf), lambda i, j: (0, j)),
            pl.BlockSpec((1, tf), lambda i, j: (0, nf + j)),
            pl.BlockSpec((None, tf, d), lambda i, j: (layer, j, 0)),
        ],
        out_specs=pl.BlockSpec((tm, d), lambda i, j: (i, 0)),
        scratch_shapes=[
            pltpu.VMEM((tm + 2 * halo, d), BF16),
            pltpu.VMEM((tm + 2 * halo, tf), F32),
            pltpu.VMEM((tm + 2 * halo, tf), F32),
            pltpu.VMEM((tm, tf), BF16),
        ],
        compiler_params=_params(("arbitrary", "arbitrary"), vmem),
        name="conv_ffn",
    )(x, x, x, mod, g.reshape(1, d), fg, up_stack, up_stack, conv_w, conv_w, conv_b, conv_b, down_stack)


def _rope_tables(seq):
    half = HEAD_DIM // 2
    rows = seq // GRID_W
    row = jnp.repeat(jnp.arange(rows, dtype=F32), GRID_W)
    col = jnp.tile(jnp.arange(GRID_W, dtype=F32), rows)
    inv = ROPE_BASE ** (-jnp.arange(0, half, 2, dtype=F32) / half)
    ang_r = row[:, None] * inv[None, :]
    ang_c = col[:, None] * inv[None, :]
    zero = jnp.zeros_like(ang_r)
    cos = jnp.concatenate([jnp.cos(ang_r)] * 2 + [jnp.cos(ang_c)] * 2, axis=1)
    sa = jnp.concatenate([-jnp.sin(ang_r), zero, -jnp.sin(ang_c), zero], axis=1)
    sb = jnp.concatenate([zero, jnp.sin(ang_r), zero, jnp.sin(ang_c)], axis=1)
    return cos, sa, sb


def _tile(n, target):
    t = min(n, target)
    while n % t:
        t //= 2
    return t


def kernel(x, c, ctx, c_ctx, ada_w, ada_b, norm_mix_g, norm_ffn_g, ffn_w_up, ffn_conv_w, ffn_conv_b, ffn_w_down,
           attn_w_qkv, attn_sink, attn_w_o, gla_w_in, gla_gf_w1, gla_gf_w2, gla_gf_b, gla_gb_w1, gla_gb_w2,
           gla_gb_b, gla_onorm_g, gla_w_o, final_norm_g):
    batch, seq, d = x.shape
    nctx = ctx.shape[1]
    depth = ada_w.shape[0]
    n_lat, n_ctx = batch * seq, batch * nctx
    dk = gla_gf_w2.shape[2] // GLA_HEADS
    dv = gla_onorm_g.shape[1]
    nq = ATT_HEADS * HEAD_DIM
    assert seq % WINDOW == 0 and nctx % WINDOW == 0 and d % LANES == 0

    n_cond = batch + 1
    pad = (-n_cond) % SUBLANES
    cc = jnp.concatenate([c, c_ctx[None, :], jnp.zeros((pad, d), F32)], axis=0)
    mods = _modulations(cc, ada_w, ada_b)
    mods = mods.reshape(depth, n_cond + pad, N_MOD, d)
    mods = jnp.pad(mods, ((0, 0), (0, 0), (0, MOD_ROWS - N_MOD), (0, 0)))

    rope = _rope_tables(seq)
    xl = x.reshape(n_lat, d)
    xc = ctx.reshape(n_ctx, d)

    tm_l, tm_c = _tile(seq, 512), _tile(nctx, 512)
    tm_pc = _tile(n_ctx, 512)
    tq_l = _tile(seq, 512)
    gla_chunk = _tile(nctx, 128)
    tf = 512
    nkv = ATT_KV_HEADS * HEAD_DIM
    hk, hv = GLA_HEADS * dk, GLA_HEADS * dv

    w_up_all, w_down_all = ffn_w_up.astype(BF16), ffn_w_down.astype(BF16)
    w_qkv_all, attn_wo_all = attn_w_qkv.astype(BF16), attn_w_o.astype(BF16)
    gla_qk_all, gla_vr_all = gla_w_in[:, :, :2 * hk].astype(BF16), gla_w_in[:, :, 2 * hk:].astype(BF16)
    gla_wo_all = gla_w_o.astype(BF16)

    for i in range(depth):
        last = i == depth - 1
        mod_l = mods[i, :batch]
        mod_c = mods[i, batch:batch + 1]
        j = i // N_MIXERS
        if i % N_MIXERS == 0:
            w_qkv, w_o = (w_qkv_all, j), (attn_wo_all, j)
            seg = [(nq + 2 * nkv, BF16)]
            qkv_l, = _project(xl, mod_l, norm_mix_g[i], w_qkv, seg, rows_per_mod=seq, tm=tm_l,
                              rope=rope, seq=seq, rope_cols=nq + nkv)
            qkv_c, = _project(xc, mod_c, norm_mix_g[i], w_qkv, seg, rows_per_mod=n_ctx, tm=tm_pc)
            xl = _attention(qkv_l, qkv_c, xl, mod_l, attn_sink[j], w_o, band=True, tq=tq_l, seq=seq, nctx=nctx,
                            rows_per_mod=seq)
            if not last:
                xc = _attention(qkv_c, None, xc, mod_c, attn_sink[j], w_o, band=False, tq=nctx, seq=nctx, nctx=nctx,
                                rows_per_mod=n_ctx)
        else:
            w_qk, w_vr, w_o = (gla_qk_all, j), (gla_vr_all, j), (gla_wo_all, j)
            rank = gla_gf_w1.shape[2]
            w1 = jnp.zeros((d, LANES), F32).at[:, :rank].set(gla_gf_w1[j]).at[:, rank:2 * rank].set(gla_gb_w1[j])
            w2f = jnp.zeros((LANES, hk), F32).at[:rank].set(gla_gf_w2[j])
            w2b = jnp.zeros((LANES, hk), F32).at[rank:2 * rank].set(gla_gb_w2[j])
            gate = (w1.astype(BF16), w2f.astype(BF16), w2b.astype(BF16), gla_gf_b[j], gla_gb_b[j])
            streams = []
            for xs, mod, rpm, tm in ((xc, mod_c, n_ctx, tm_pc), (xl, mod_l, seq, tm_l)):
                qk, gf, gb = _project(xs, mod, norm_mix_g[i], w_qk, [(2 * hk, F32)], rows_per_mod=rpm, tm=tm,
                                      gate=gate)
                v, r = _project(xs, mod, norm_mix_g[i], w_vr, [(hv, BF16), (hv, F32)], rows_per_mod=rpm, tm=tm)
                streams.append((qk, v, r, gf, gb))
            s0 = jnp.zeros((batch, GLA_HEADS, dk, dv), F32)
            scan = functools.partial(_gla_scan, batch=batch, chunk=gla_chunk, dk=dk, dv=dv)
            (qk_c, v_c, r_c, gf_c, gb_c), (qk_l, v_l, r_l, gf_l, gb_l) = streams
            of_c, ob_c, sf, sb = scan(qk_c, v_c, gf_c, gb_c, s0, s0, length=nctx)
            of_l, ob_l, _, _ = scan(qk_l, v_l, gf_l, gb_l, sf, sb, length=seq)
            xl = _gla_output(of_l, ob_l, r_l, xl, mod_l, gla_onorm_g[j], w_o, tm=tm_l, rows_per_mod=seq, dv=dv)
            if not last:
                xc = _gla_output(of_c, ob_c, r_c, xc, mod_c, gla_onorm_g[j], w_o, tm=tm_c, rows_per_mod=n_ctx, dv=dv)
        w_up, w_down = (w_up_all, i), (w_down_all, i)
        xl = _conv_ffn(xl, mod_l, norm_ffn_g[i], w_up, ffn_conv_w[i], ffn_conv_b[i], w_down,
                       tm=_tile(seq, 1024), tf=tf, seq=seq, rows_per_mod=seq, final_g=final_norm_g if last else None)
        if not last:
            xc = _conv_ffn(xc, mod_c, norm_ffn_g[i], w_up, ffn_conv_w[i], ffn_conv_b[i], w_down,
                           tm=tm_c, tf=tf, seq=nctx, rows_per_mod=n_ctx)
    return xl.reshape(batch, seq, d)
```

```python
import functools

import jax
import jax.numpy as jnp
from jax import lax
from jax.experimental import pallas as pl
from jax.experimental.pallas import tpu as pltpu

GRID_W = 64
N_MIXERS = 2
ATT_HEADS = 16
ATT_KV_HEADS = 4
ATT_GROUP = ATT_HEADS // ATT_KV_HEADS
HEAD_DIM = 128
WINDOW = 128
ROPE_BASE = 10000.0
GLA_HEADS = 4
GLA_GATE_RANK = 16
GLA_GATE_NORM = 16.0
CONV_W = 3
EPS = 1e-6
N_MOD = 6
MOD_ROWS = 8

LANES = 128
SUBLANES = 8
VMEM_LIMIT_CAP = 60 * 1024 * 1024

BF16 = jnp.bfloat16
F32 = jnp.float32


def _dot(a, b):
    return jnp.dot(a, b, preferred_element_type=F32)


def _dot_nt(a, b):
    return lax.dot_general(a, b, (((1,), (1,)), ((), ())), preferred_element_type=F32)


def _dot_tn(a, b):
    return lax.dot_general(a, b, (((0,), (0,)), ((), ())), preferred_element_type=F32)


def _params(semantics, vmem_bytes):
    limit = min(VMEM_LIMIT_CAP, int(vmem_bytes * 1.25) + (4 << 20))
    return pltpu.CompilerParams(dimension_semantics=semantics, vmem_limit_bytes=limit)


def _sigmoid(x):
    return 1.0 / (1.0 + jnp.exp(-x))


def _silu(x):
    return x * _sigmoid(x)


def _norm_mod(x, g, shift, scale):
    y = x * lax.rsqrt(jnp.mean(x * x, axis=-1, keepdims=True) + EPS)
    return (y * g) * (1.0 + scale) + shift


def _mod_kernel(c_ref, w_ref, b_ref, o_ref):
    s = _silu(c_ref[...]).astype(BF16)
    o_ref[0] = _dot(s, w_ref[0].astype(BF16)) + b_ref[0]


def _modulations(cc, ada_w, ada_b):
    depth, d, n = ada_w.shape
    r = cc.shape[0]
    tn = 1024
    vmem = 2 * (d * tn * 4 + r * tn * 4 + tn * 4) + r * d * 4 + d * tn * 2
    return pl.pallas_call(
        _mod_kernel,
        out_shape=jax.ShapeDtypeStruct((depth, r, n), F32),
        grid=(depth, n // tn),
        in_specs=[
            pl.BlockSpec((r, d), lambda l, j: (0, 0)),
            pl.BlockSpec((1, d, tn), lambda l, j: (l, 0, j)),
            pl.BlockSpec((1, 1, tn), lambda l, j: (l, 0, j)),
        ],
        out_specs=pl.BlockSpec((1, r, tn), lambda l, j: (l, 0, j)),
        compiler_params=_params(("arbitrary", "arbitrary"), vmem),
        name="adaln_mod",
    )(cc, ada_w, ada_b.reshape(depth, 1, n))


def _log_sigmoid(z):
    return jnp.minimum(z, 0.0) - jnp.log(1.0 + jnp.exp(-jnp.abs(z)))


def _proj_kernel(*refs, segments, rope_cols, gated, col_chunk):
    n_out = len(segments) + (2 if gated else 0)
    x_ref, mod_ref, g_ref, w_ref = refs[:4]
    extra = refs[4:len(refs) - n_out - 2]
    outs = refs[len(refs) - n_out - 2:len(refs) - 2]
    h_even, h_odd = refs[-2:]
    s = pl.program_id(0)

    @pl.when(s == 0)
    def _():
        h_odd[...] = jnp.zeros_like(h_odd)

    def body(h_new, h_old):
        h = h_old[...]
        if gated:
            w1_ref, w2f_ref, w2b_ref, bf_ref, bb_ref = extra
            t = _dot(h, w1_ref[...]).astype(BF16)
            for w2_ref, b_ref, o_ref in ((w2f_ref, bf_ref, outs[-2]), (w2b_ref, bb_ref, outs[-1])):
                o_ref[...] = _dot(t, w2_ref[...]) + b_ref[...]
        if rope_cols:
            cos, sa, sb = extra[0][...], extra[1][...], extra[2][...]
        col = 0
        for (width, _), y_ref in zip(segments, outs):
            for c0 in range(0, width, col_chunk):
                y = _dot(h, w_ref[:, col + c0:col + c0 + col_chunk])
                if col + c0 < rope_cols:
                    parts = []
                    for p in range(col_chunk // HEAD_DIM):
                        yh = y[:, p * HEAD_DIM:(p + 1) * HEAD_DIM]
                        if col + c0 + p * HEAD_DIM < rope_cols:
                            up = pltpu.roll(yh, HEAD_DIM - HEAD_DIM // 4, axis=1)
                            dn = pltpu.roll(yh, HEAD_DIM // 4, axis=1)
                            yh = yh * cos + up * sa + dn * sb
                        parts.append(yh)
                    y = jnp.concatenate(parts, axis=1)
                y_ref[:, c0:c0 + col_chunk] = y.astype(y_ref.dtype)
            col += width
        x = x_ref[...]
        gs = g_ref[...] * (1.0 + mod_ref[0, 1:2, :])
        r = lax.rsqrt(jnp.mean(x * x, axis=-1, keepdims=True) + EPS)
        h_new[...] = ((x * r) * gs + mod_ref[0, 0:1, :]).astype(BF16)

    @pl.when(s % 2 == 0)
    def _():
        body(h_even, h_odd)

    @pl.when(s % 2 == 1)
    def _():
        body(h_odd, h_even)


def _project(x, mod, g, w, segments, *, rows_per_mod, tm, rope=None, seq=None, rope_cols=0, gate=None):
    n, d = x.shape
    w_stack, layer = w
    m = w_stack.shape[2]
    nrow = n // tm
    col_chunk = 1024 if all(wd % 1024 == 0 for wd, _ in segments) else 512

    def cur(s):
        return jnp.minimum(s, nrow - 1)

    def prev(s):
        return jnp.maximum(s - 1, 0)

    in_specs = [
        pl.BlockSpec((tm, d), lambda s: (cur(s), 0)),
        pl.BlockSpec((1, MOD_ROWS, d), lambda s: ((cur(s) * tm) // rows_per_mod, 0, 0)),
        pl.BlockSpec((1, d), lambda s: (0, 0)),
        pl.BlockSpec((None, d, m), lambda s: (layer, 0, 0), pipeline_mode=pl.Buffered(1)),
    ]
    args = [x, mod, g.reshape(1, d), w_stack]
    out_shape = [jax.ShapeDtypeStruct((n, wd), dt) for wd, dt in segments]
    out_specs = [pl.BlockSpec((tm, wd), lambda s: (prev(s), 0)) for wd, _ in segments]
    vmem = (2 * tm * d * 4 + d * m * 2 + 2 * sum(tm * wd * jnp.dtype(dt).itemsize for wd, dt in segments)
            + 2 * tm * d * 2 + 3 * tm * col_chunk * 4 + 2 * tm * d * 4)
    if rope is not None:
        tiles_per_seq = seq // tm
        for tab in rope:
            in_specs.append(pl.BlockSpec((tm, HEAD_DIM), lambda s: (prev(s) % tiles_per_seq, 0)))
            args.append(tab)
        vmem += 6 * tm * HEAD_DIM * 4
    if gate is not None:
        w1, w2f, w2b, bf, bb = gate
        dk = w2f.shape[1]
        in_specs += [
            pl.BlockSpec(w1.shape, lambda s: (0, 0)),
            pl.BlockSpec(w2f.shape, lambda s: (0, 0)),
            pl.BlockSpec(w2b.shape, lambda s: (0, 0)),
            pl.BlockSpec((1, dk), lambda s: (0, 0)),
            pl.BlockSpec((1, dk), lambda s: (0, 0)),
        ]
        args += [w1, w2f, w2b, bf.reshape(1, dk), bb.reshape(1, dk)]
        out_shape += [jax.ShapeDtypeStruct((n, dk), F32)] * 2
        out_specs += [pl.BlockSpec((tm, dk), lambda s: (prev(s), 0))] * 2
        vmem += 6 * tm * dk * 4 + 2 * (d * LANES * 2 + 2 * LANES * dk * 2)
    return pl.pallas_call(
        functools.partial(_proj_kernel, segments=tuple(segments), rope_cols=rope_cols, gated=gate is not None,
                          col_chunk=col_chunk),
        out_shape=out_shape,
        grid=(nrow + 1,),
        in_specs=in_specs,
        out_specs=out_specs,
        scratch_shapes=[pltpu.VMEM((tm, d), BF16), pltpu.VMEM((tm, d), BF16)],
        compiler_params=_params(("arbitrary",), vmem),
        name="proj",
    )(*args)


def _attend(qg, kall, vall, mask, sink_col):
    s = _dot_nt(qg, kall) * (HEAD_DIM ** -0.5)
    if mask is not None:
        s = jnp.where(mask, s, -jnp.inf)
    mx = jnp.maximum(jnp.max(s, axis=-1, keepdims=True), sink_col)
    e = jnp.exp(s - mx)
    denom = jnp.sum(e, axis=-1, keepdims=True) + jnp.exp(sink_col - mx)
    return _dot(e.astype(BF16), vall) / denom


def _sink_column(sink_ref, kv, rows):
    cols = [jnp.broadcast_to(sink_ref[0:1, kv * ATT_GROUP + h:kv * ATT_GROUP + h + 1], (rows, 1))
            for h in range(ATT_GROUP)]
    return jnp.concatenate(cols, axis=0)


def _attn_kernel(*refs, band, tq, seq, sub):
    if band:
        (q_ref, kp_ref, kc_ref, kn_ref, vp_ref, vc_ref, vn_ref, kx_ref, vx_ref,
         x_ref, mod_ref, sink_ref, wo_ref, o_ref) = refs
    else:
        q_ref, kx_ref, vx_ref, x_ref, mod_ref, sink_ref, wo_ref, o_ref = refs
    blk = WINDOW
    nsub = tq // sub if band else 1
    rows = sub if band else tq
    if band:
        kwin = jnp.concatenate([kp_ref[...], kc_ref[...], kn_ref[...]], axis=0)
        vwin = jnp.concatenate([vp_ref[...], vc_ref[...], vn_ref[...]], axis=0)
        t0 = (pl.program_id(0) * tq) % seq
        nctx = kx_ref.shape[0]
        nwin = sub + 2 * blk
        ncol = nwin + nctx
        qi = lax.broadcasted_iota(jnp.int32, (ATT_GROUP * sub, ncol), 0) & (sub - 1)
        kj = lax.broadcasted_iota(jnp.int32, (ATT_GROUP * sub, ncol), 1)
        in_band = jnp.abs(qi - kj + blk) <= WINDOW
    outs = []
    for sb in range(nsub):
        heads = [None] * ATT_HEADS
        for kv in range(ATT_KV_HEADS):
            lo = kv * HEAD_DIM
            qg = jnp.concatenate(
                [q_ref[sb * rows:(sb + 1) * rows, (kv * ATT_GROUP + h) * HEAD_DIM:(kv * ATT_GROUP + h + 1) * HEAD_DIM]
                 for h in range(ATT_GROUP)], axis=0)
            if band:
                kall = jnp.concatenate([kwin[sb * sub:sb * sub + nwin, lo:lo + HEAD_DIM],
                                        kx_ref[:, lo:lo + HEAD_DIM]], axis=0)
                vall = jnp.concatenate([vwin[sb * sub:sb * sub + nwin, lo:lo + HEAD_DIM],
                                        vx_ref[:, lo:lo + HEAD_DIM]], axis=0)
                kpos = t0 + sb * sub - blk + kj
                mask = (kj >= nwin) | (in_band & (kpos >= 0) & (kpos < seq))
            else:
                kall = kx_ref[:, lo:lo + HEAD_DIM]
                vall = vx_ref[:, lo:lo + HEAD_DIM]
                mask = None
            og = _attend(qg, kall, vall, mask, _sink_column(sink_ref, kv, rows))
            for h in range(ATT_GROUP):
                heads[kv * ATT_GROUP + h] = og[h * rows:(h + 1) * rows].astype(BF16)
        outs.append(jnp.concatenate(heads, axis=1))
    o = outs[0] if len(outs) == 1 else jnp.concatenate(outs, axis=0)
    o_ref[...] = x_ref[...] + mod_ref[0, 2:3, :] * _dot(o, wo_ref[...])


def _attention(qkv, qkv_ctx, x, mod, sink, w_o, *, band, tq, seq, nctx, rows_per_mod):
    n, d = x.shape
    wo_stack, layer = w_o
    nq = ATT_HEADS * HEAD_DIM
    nkv = ATT_KV_HEADS * HEAD_DIM
    kcol, vcol = nq // nkv, nq // nkv + 1
    blk = WINDOW
    nblk = n // blk
    r = tq // blk
    q_spec = pl.BlockSpec((tq, nq), lambda i: (i, 0))
    tail_specs = [
        pl.BlockSpec((tq, d), lambda i: (i, 0)),
        pl.BlockSpec((1, MOD_ROWS, d), lambda i: ((i * tq) // rows_per_mod, 0, 0)),
        pl.BlockSpec((1, ATT_HEADS), lambda i: (0, 0)),
        pl.BlockSpec((None, nq, d), lambda i: (layer, 0, 0)),
    ]
    tail_args = [x, mod, sink.reshape(1, ATT_HEADS), wo_stack]
    if band:
        def ctx_idx(col):
            return lambda i: ((i * tq) // seq, col)
        in_specs = [
            q_spec,
            pl.BlockSpec((blk, nkv), lambda i: (jnp.maximum(i * r - 1, 0), kcol)),
            pl.BlockSpec((tq, nkv), lambda i: (i, kcol)),
            pl.BlockSpec((blk, nkv), lambda i: (jnp.minimum((i + 1) * r, nblk - 1), kcol)),
            pl.BlockSpec((blk, nkv), lambda i: (jnp.maximum(i * r - 1, 0), vcol)),
            pl.BlockSpec((tq, nkv), lambda i: (i, vcol)),
            pl.BlockSpec((blk, nkv), lambda i: (jnp.minimum((i + 1) * r, nblk - 1), vcol)),
            pl.BlockSpec((nctx, nkv), ctx_idx(kcol)),
            pl.BlockSpec((nctx, nkv), ctx_idx(vcol)),
        ] + tail_specs
        args = [qkv] * 7 + [qkv_ctx] * 2 + tail_args
        ncol = 3 * blk + nctx
    else:
        in_specs = [
            q_spec,
            pl.BlockSpec((tq, nkv), lambda i: (i, kcol)),
            pl.BlockSpec((tq, nkv), lambda i: (i, vcol)),
        ] + tail_specs
        args = [qkv] * 3 + tail_args
        ncol = tq
    vmem = (2 * (tq * nq * 2 + 4 * (tq + 2 * blk + nctx) * nkv * 2 + 2 * tq * d * 4 + nq * d * 2)
            + 6 * ATT_GROUP * blk * ncol * 4 + 2 * tq * d * 4)
    return pl.pallas_call(
        functools.partial(_attn_kernel, band=band, tq=tq, seq=seq, sub=min(tq, WINDOW)),
        out_shape=jax.ShapeDtypeStruct((n, d), F32),
        grid=(n // tq,),
        in_specs=in_specs,
        out_specs=pl.BlockSpec((tq, d), lambda i: (i, 0)),
        compiler_params=_params(("arbitrary",), vmem),
        name="attn_band" if band else "attn_ctx",
    )(*args)


def _gla_scan_kernel(qf_ref, kf_ref, vf_ref, gf_ref, qb_ref, kb_ref, vb_ref, gb_ref, s0f_ref, s0b_ref,
                     of_ref, ob_ref, sf_ref, sb_ref, *scratch, chunk, dk, dv, q_scale):
    per_set = len(scratch) // 2
    sets = (scratch[:per_set], scratch[per_set:])
    s = pl.program_id(1)

    @pl.when(s == 0)
    def _():
        sf_ref[...] = s0f_ref[...]
        sb_ref[...] = s0b_ref[...]
        for ref in sets[1]:
            ref[...] = jnp.zeros_like(ref) if ref.dtype == BF16 else jnp.ones_like(ref)

    row = lax.broadcasted_iota(jnp.int32, (chunk, chunk), 0)
    col = lax.broadcasted_iota(jnp.int32, (chunk, chunk), 1)
    directions = (
        (qf_ref, kf_ref, vf_ref, gf_ref, of_ref, sf_ref, col <= row, chunk - 1),
        (qb_ref, kb_ref, vb_ref, gb_ref, ob_ref, sb_ref, col >= row, 0),
    )

    def body(new, old):
        for d, (_, _, v_ref, _, o_ref, s_ref, tri, _) in enumerate(directions):
            qa_ref, ka_ref, qs_ref, ks_ref, decay_ref = old[5 * d:5 * d + 5]
            v = v_ref[...]
            for h in range(GLA_HEADS):
                kc = slice(h * dk, (h + 1) * dk)
                vc = slice(h * dv, (h + 1) * dv)
                a = jnp.where(tri, _dot_nt(qa_ref[:, kc], ka_ref[:, kc]), 0.0).astype(BF16)
                st = s_ref[0, h]
                o_ref[:, vc] = _dot(a, v[:, vc]) + _dot(qs_ref[:, kc], st.astype(BF16))
                upd = _dot_tn(ks_ref[:, kc], v[:, vc])
                decay = decay_ref[kc, :]
                s_ref[0, h] = jnp.concatenate(
                    [st[:, t * LANES:(t + 1) * LANES] * decay for t in range(dv // LANES)], axis=1) + upd
        for d, (q_ref, k_ref, _, g_ref, _, _, tri, last) in enumerate(directions):
            qa_ref, ka_ref, qs_ref, ks_ref, decay_ref = new[5 * d:5 * d + 5]
            tri_b = tri.astype(BF16)
            g = _log_sigmoid(g_ref[...]) / GLA_GATE_NORM
            g_hi = g.astype(BF16)
            g_lo = (g - g_hi.astype(F32)).astype(BF16)
            b = _dot(tri_b, g_hi) + _dot(tri_b, g_lo)
            b_tot = b[last:last + 1]
            b_mid = b[chunk // 2:chunk // 2 + 1]
            q = q_ref[...] * q_scale
            k = k_ref[...]
            qa_ref[...] = (q * jnp.exp(b - b_mid)).astype(BF16)
            ka_ref[...] = (k * jnp.exp(b_mid - b)).astype(BF16)
            qs_ref[...] = (q * jnp.exp(b)).astype(BF16)
            ks_ref[...] = (k * jnp.exp(b_tot - b)).astype(BF16)
            decay_ref[...] = jnp.exp(jnp.broadcast_to(b_tot, (LANES, b.shape[1]))).T

    @pl.when(s % 2 == 0)
    def _():
        body(sets[0], sets[1])

    @pl.when(s % 2 == 1)
    def _():
        body(sets[1], sets[0])


def _gla_scan(qk, v, g_f, g_b, s0_f, s0_b, *, batch, length, chunk, dk, dv):
    nc = length // chunk
    hk, hv = GLA_HEADS * dk, GLA_HEADS * dv

    def nxt(c):
        return jnp.minimum(c, nc - 1)

    def cur(c):
        return jnp.maximum(c - 1, 0)

    def fwd(b, c):
        return b * nc + c

    def bwd(b, c):
        return b * nc + nc - 1 - c

    state_spec = pl.BlockSpec((1, GLA_HEADS, dk, dv), lambda b, c: (b, 0, 0, 0))
    in_specs = []
    for rows in (fwd, bwd):
        in_specs += [
            pl.BlockSpec((chunk, hk), lambda b, c, rows=rows: (rows(b, nxt(c)), 0)),
            pl.BlockSpec((chunk, hk), lambda b, c, rows=rows: (rows(b, nxt(c)), 1)),
            pl.BlockSpec((chunk, hv), lambda b, c, rows=rows: (rows(b, cur(c)), 0)),
            pl.BlockSpec((chunk, hk), lambda b, c, rows=rows: (rows(b, nxt(c)), 0)),
        ]
    in_specs += [state_spec, state_spec]
    operand_set = [pltpu.VMEM((chunk, hk), BF16)] * 4 + [pltpu.VMEM((hk, LANES), F32)]
    state_bytes = GLA_HEADS * dk * dv * 4
    vmem = (2 * 2 * (3 * chunk * hk * 4 + chunk * hv * 2 + chunk * hv * 4) + 8 * state_bytes
            + 16 * chunk * hk * 4 + 4 * dk * dv * 4 + 4 * (4 * chunk * hk * 2 + hk * LANES * 4))
    return pl.pallas_call(
        functools.partial(_gla_scan_kernel, chunk=chunk, dk=dk, dv=dv, q_scale=dk ** -0.5),
        out_shape=[jax.ShapeDtypeStruct((batch * length, hv), F32)] * 2
        + [jax.ShapeDtypeStruct(s0_f.shape, F32)] * 2,
        grid=(batch, nc + 1),
        in_specs=in_specs,
        out_specs=[
            pl.BlockSpec((chunk, hv), lambda b, c: (fwd(b, cur(c)), 0)),
            pl.BlockSpec((chunk, hv), lambda b, c: (bwd(b, cur(c)), 0)),
            state_spec,
            state_spec,
        ],
        scratch_shapes=operand_set * 4,
        compiler_params=_params(("arbitrary", "arbitrary"), vmem),
        name="gla_scan",
    )(qk, qk, v, g_f, qk, qk, v, g_b, s0_f, s0_b)


def _gla_out_kernel(of_ref, ob_ref, r_ref, x_ref, mod_ref, g_ref, wo_ref, o_ref, *, dv):
    o = of_ref[...] + ob_ref[...]
    parts = []
    for h in range(o.shape[1] // dv):
        oh = o[:, h * dv:(h + 1) * dv]
        parts.append((oh * lax.rsqrt(jnp.mean(oh * oh, axis=-1, keepdims=True) + EPS)) * g_ref[...])
    gated = (jnp.concatenate(parts, axis=1) * _silu(r_ref[...])).astype(BF16)
    o_ref[...] = x_ref[...] + mod_ref[0, 2:3, :] * _dot(gated, wo_ref[...])


def _gla_output(o_f, o_b, r, x, mod, onorm_g, w_o, *, tm, rows_per_mod, dv):
    n, d = x.shape
    wo_stack, layer = w_o
    dvt = o_f.shape[1]
    vmem = 2 * (5 * tm * dvt * 4 + dvt * d * 2) + 4 * tm * dvt * 4
    return pl.pallas_call(
        functools.partial(_gla_out_kernel, dv=dv),
        out_shape=jax.ShapeDtypeStruct((n, d), F32),
        grid=(n // tm,),
        in_specs=[
            pl.BlockSpec((tm, dvt), lambda i: (i, 0)),
            pl.BlockSpec((tm, dvt), lambda i: (i, 0)),
            pl.BlockSpec((tm, dvt), lambda i: (i, 0)),
            pl.BlockSpec((tm, d), lambda i: (i, 0)),
            pl.BlockSpec((1, MOD_ROWS, d), lambda i: ((i * tm) // rows_per_mod, 0, 0)),
            pl.BlockSpec((1, dv), lambda i: (0, 0)),
            pl.BlockSpec((None, dvt, d), lambda i: (layer, 0, 0)),
        ],
        out_specs=pl.BlockSpec((tm, d), lambda i: (i, 0)),
        compiler_params=_params(("arbitrary",), vmem),
        name="gla_out",
    )(o_f, o_b, r, x, mod, onorm_g.reshape(1, dv), wo_stack)


def _ffn_kernel(xp_ref, x_ref, xn_ref, mod_ref, g_ref, fg_ref, wg_ref, wv_ref, cwg_ref, cwv_ref, cbg_ref, cbv_ref,
                wd_ref, o_ref, h_scr, ug_scr, uv_scr, act_scr, *, tm, seq, final_norm):
    i = pl.program_id(0)
    j = pl.program_id(1)
    halo = SUBLANES
    rows = min(tm, seq, 256)
    assert tm <= seq or (rows == seq and tm % seq == 0)

    @pl.when(j == 0)
    def _():
        shift = mod_ref[0, 3:4, :]
        gs = g_ref[...] * (1.0 + mod_ref[0, 4:5, :])

        def norm_mod(x):
            return (x * lax.rsqrt(jnp.mean(x * x, axis=-1, keepdims=True) + EPS)) * gs + shift

        first = (i * tm) % seq == 0
        last = ((i + 1) * tm) % seq == 0
        h_scr[0:halo, :] = jnp.where(first, 0.0, norm_mod(xp_ref[...])).astype(BF16)
        for r0 in range(0, tm, rows):
            h_scr[halo + r0:halo + r0 + rows, :] = norm_mod(x_ref[r0:r0 + rows, :]).astype(BF16)
        h_scr[halo + tm:2 * halo + tm, :] = jnp.where(last, 0.0, norm_mod(xn_ref[...])).astype(BF16)
        o_ref[...] = jnp.zeros_like(o_ref)

    h = h_scr[...]
    ug_scr[...] = _dot(h, wg_ref[...])
    uv_scr[...] = _dot(h, wv_ref[...])

    def conv(u_scr, cw_ref, cb_ref, r0):
        prev = u_scr[halo - 1 + r0:halo - 1 + r0 + rows, :]
        nxt = u_scr[halo + 1 + r0:halo + 1 + r0 + rows, :]
        if tm > seq:
            rid = lax.broadcasted_iota(jnp.int32, prev.shape, 0)
            prev = jnp.where(rid == 0, 0.0, prev)
            nxt = jnp.where(rid == rows - 1, 0.0, nxt)
        return (cb_ref[...] + cw_ref[0:1, :] * prev + cw_ref[1:2, :] * u_scr[halo + r0:halo + r0 + rows, :]
                + cw_ref[2:3, :] * nxt)

    for r0 in range(0, tm, rows):
        act_scr[r0:r0 + rows, :] = (_silu(conv(ug_scr, cwg_ref, cbg_ref, r0))
                                    * conv(uv_scr, cwv_ref, cbv_ref, r0)).astype(BF16)
    act = act_scr[...]
    for c0 in range(0, o_ref.shape[1], 512):
        o_ref[:, c0:c0 + 512] += _dot(act, wd_ref[:, c0:c0 + 512])

    @pl.when(j == pl.num_programs(1) - 1)
    def _():
        er = min(tm, 128)

        def residual(c, carry):
            sl = pl.ds(pl.multiple_of(c * er, er), er)
            y = x_ref[sl, :] + mod_ref[0, 5:6, :] * o_ref[sl, :]
            if final_norm:
                y = (y * lax.rsqrt(jnp.mean(y * y, axis=-1, keepdims=True) + EPS)) * fg_ref[...]
            o_ref[sl, :] = y
            return carry

        lax.fori_loop(0, tm // er, residual, 0)


def _conv_ffn(x, mod, g, w_up, conv_w, conv_b, w_down, *, tm, tf, seq, rows_per_mod, final_g=None):
    n, d = x.shape
    (up_stack, layer), (down_stack, _) = w_up, w_down
    dff = down_stack.shape[1]
    nf = dff // tf
    halo = SUBLANES
    r = tm // halo
    nhalo = n // halo
    conv_b = conv_b.reshape(1, 2 * dff)
    fg = (g if final_g is None else final_g).reshape(1, d)
    vmem = (2 * (2 * tm * d * 4 + 2 * d * tf * 2 + tf * d * 2) + (tm + 2 * halo) * d * 2
            + 2 * (tm + 2 * halo) * tf * 4 + tm * tf * 2 + 8 * min(tm, 256) * max(tf, d) * 4)
    return pl.pallas_call(
        functools.partial(_ffn_kernel, tm=tm, seq=seq, final_norm=final_g is not None),
        out_shape=jax.ShapeDtypeStruct((n, d), F32),
        grid=(n // tm, nf),
        in_specs=[
            pl.BlockSpec((halo, d), lambda i, j: (jnp.maximum(i * r - 1, 0), 0)),
            pl.BlockSpec((tm, d), lambda i, j: (i, 0)),
            pl.BlockSpec((halo, d), lambda i, j: (jnp.minimum((i + 1) * r, nhalo - 1), 0)),
            pl.BlockSpec((1, MOD_ROWS, d), lambda i, j: ((i * tm) // rows_per_mod, 0, 0)),
            pl.BlockSpec((1, d), lambda i, j: (0, 0)),
            pl.BlockSpec((1, d), lambda i, j: (0, 0)),
            pl.BlockSpec((None, d, tf), lambda i, j: (layer, 0, j)),
            pl.BlockSpec((None, d, tf), lambda i, j: (layer, 0, nf + j)),
            pl.BlockSpec((CONV_W, tf), lambda i, j: (0, j)),
            pl.BlockSpec((CONV_W, tf), lambda i, j: (0, nf + j)),
            pl.BlockSpec((1, tf), lambda i, j: (0, j)),
            pl.BlockSpec((1, tf), lambda i, j: (0, nf + j)),
            pl.BlockSpec((None, tf, d), lambda i, j: (layer, j, 0)),
        ],
        out_specs=pl.BlockSpec((tm, d), lambda i, j: (i, 0)),
        scratch_shapes=[
            pltpu.VMEM((tm + 2 * halo, d), BF16),
            pltpu.VMEM((tm + 2 * halo, tf), F32),
            pltpu.VMEM((tm + 2 * halo, tf), F32),
            pltpu.VMEM((tm, tf), BF16),
        ],
        compiler_params=_params(("arbitrary", "arbitrary"), vmem),
        name="conv_ffn",
    )(x, x, x, mod, g.reshape(1, d), fg, up_stack, up_stack, conv_w, conv_w, conv_b, conv_b, down_stack)


def _rope_tables(seq):
    half = HEAD_DIM // 2
    rows = seq // GRID_W
    row = jnp.repeat(jnp.arange(rows, dtype=F32), GRID_W)
    col = jnp.tile(jnp.arange(GRID_W, dtype=F32), rows)
    inv = ROPE_BASE ** (-jnp.arange(0, half, 2, dtype=F32) / half)
    ang_r = row[:, None] * inv[None, :]
    ang_c = col[:, None] * inv[None, :]
    zero = jnp.zeros_like(ang_r)
    cos = jnp.concatenate([jnp.cos(ang_r)] * 2 + [jnp.cos(ang_c)] * 2, axis=1)
    sa = jnp.concatenate([-jnp.sin(ang_r), zero, -jnp.sin(ang_c), zero], axis=1)
    sb = jnp.concatenate([zero, jnp.sin(ang_r), zero, jnp.sin(ang_c)], axis=1)
    return cos, sa, sb


def _tile(n, target):
    t = min(n, target)
    while n % t:
        t //= 2
    return t


def kernel(x, c, ctx, c_ctx, ada_w, ada_b, norm_mix_g, norm_ffn_g, ffn_w_up, ffn_conv_w, ffn_conv_b, ffn_w_down,
           attn_w_qkv, attn_sink, attn_w_o, gla_w_in, gla_gf_w1, gla_gf_w2, gla_gf_b, gla_gb_w1, gla_gb_w2,
           gla_gb_b, gla_onorm_g, gla_w_o, final_norm_g):
    batch, seq, d = x.shape
    nctx = ctx.shape[1]
    depth = ada_w.shape[0]
    n_lat, n_ctx = batch * seq, batch * nctx
    dk = gla_gf_w2.shape[2] // GLA_HEADS
    dv = gla_onorm_g.shape[1]
    nq = ATT_HEADS * HEAD_DIM
    assert seq % WINDOW == 0 and nctx % WINDOW == 0 and d % LANES == 0

    n_cond = batch + 1
    pad = (-n_cond) % SUBLANES
    cc = jnp.concatenate([c, c_ctx[None, :], jnp.zeros((pad, d), F32)], axis=0)
    mods = _modulations(cc, ada_w, ada_b)
    mods = mods.reshape(depth, n_cond + pad, N_MOD, d)
    mods = jnp.pad(mods, ((0, 0), (0, 0), (0, MOD_ROWS - N_MOD), (0, 0)))

    rope = _rope_tables(seq)
    xl = x.reshape(n_lat, d)
    xc = ctx.reshape(n_ctx, d)

    tm_l, tm_c = _tile(seq, 512), _tile(nctx, 512)
    tm_pc = _tile(n_ctx, 512)
    tm_fc = _tile(n_ctx, 1024) if nctx <= 256 and _tile(n_ctx, 1024) % nctx == 0 else tm_c
    tq_l = _tile(seq, 512)
    gla_chunk = _tile(nctx, 128)
    tf = 512
    nkv = ATT_KV_HEADS * HEAD_DIM
    hk, hv = GLA_HEADS * dk, GLA_HEADS * dv

    w_up_all, w_down_all = ffn_w_up.astype(BF16), ffn_w_down.astype(BF16)
    w_qkv_all, attn_wo_all = attn_w_qkv.astype(BF16), attn_w_o.astype(BF16)
    gla_qk_all, gla_vr_all = gla_w_in[:, :, :2 * hk].astype(BF16), gla_w_in[:, :, 2 * hk:].astype(BF16)
    gla_wo_all = gla_w_o.astype(BF16)

    for i in range(depth):
        last = i == depth - 1
        mod_l = mods[i, :batch]
        mod_c = mods[i, batch:batch + 1]
        j = i // N_MIXERS
        if i % N_MIXERS == 0:
            w_qkv, w_o = (w_qkv_all, j), (attn_wo_all, j)
            seg = [(nq + 2 * nkv, BF16)]
            qkv_l, = _project(xl, mod_l, norm_mix_g[i], w_qkv, seg, rows_per_mod=seq, tm=tm_l,
                              rope=rope, seq=seq, rope_cols=nq + nkv)
            qkv_c, = _project(xc, mod_c, norm_mix_g[i], w_qkv, seg, rows_per_mod=n_ctx, tm=tm_pc)
            xl = _attention(qkv_l, qkv_c, xl, mod_l, attn_sink[j], w_o, band=True, tq=tq_l, seq=seq, nctx=nctx,
                            rows_per_mod=seq)
            if not last:
                xc = _attention(qkv_c, None, xc, mod_c, attn_sink[j], w_o, band=False, tq=nctx, seq=nctx, nctx=nctx,
                                rows_per_mod=n_ctx)
        else:
            w_qk, w_vr, w_o = (gla_qk_all, j), (gla_vr_all, j), (gla_wo_all, j)
            rank = gla_gf_w1.shape[2]
            w1 = jnp.zeros((d, LANES), F32).at[:, :rank].set(gla_gf_w1[j]).at[:, rank:2 * rank].set(gla_gb_w1[j])
            w2f = jnp.zeros((LANES, hk), F32).at[:rank].set(gla_gf_w2[j])
            w2b = jnp.zeros((LANES, hk), F32).at[rank:2 * rank].set(gla_gb_w2[j])
            gate = (w1.astype(BF16), w2f.astype(BF16), w2b.astype(BF16), gla_gf_b[j], gla_gb_b[j])
            streams = []
            for xs, mod, rpm, tm in ((xc, mod_c, n_ctx, tm_pc), (xl, mod_l, seq, tm_l)):
                qk, gf, gb = _project(xs, mod, norm_mix_g[i], w_qk, [(2 * hk, F32)], rows_per_mod=rpm, tm=tm,
                                      gate=gate)
                v, r = _project(xs, mod, norm_mix_g[i], w_vr, [(hv, BF16), (hv, F32)], rows_per_mod=rpm, tm=tm)
                streams.append((qk, v, r, gf, gb))
            s0 = jnp.zeros((batch, GLA_HEADS, dk, dv), F32)
            scan = functools.partial(_gla_scan, batch=batch, chunk=gla_chunk, dk=dk, dv=dv)
            (qk_c, v_c, r_c, gf_c, gb_c), (qk_l, v_l, r_l, gf_l, gb_l) = streams
            of_c, ob_c, sf, sb = scan(qk_c, v_c, gf_c, gb_c, s0, s0, length=nctx)
            of_l, ob_l, _, _ = scan(qk_l, v_l, gf_l, gb_l, sf, sb, length=seq)
            xl = _gla_output(of_l, ob_l, r_l, xl, mod_l, gla_onorm_g[j], w_o, tm=tm_l, rows_per_mod=seq, dv=dv)
            if not last:
                xc = _gla_output(of_c, ob_c, r_c, xc, mod_c, gla_onorm_g[j], w_o, tm=tm_c, rows_per_mod=n_ctx, dv=dv)
        w_up, w_down = (w_up_all, i), (w_down_all, i)
        xl = _conv_ffn(xl, mod_l, norm_ffn_g[i], w_up, ffn_conv_w[i], ffn_conv_b[i], w_down,
                       tm=_tile(seq, 1024), tf=tf, seq=seq, rows_per_mod=seq, final_g=final_norm_g if last else None)
        if not last:
            xc = _conv_ffn(xc, mod_c, norm_ffn_g[i], w_up, ffn_conv_w[i], ffn_conv_b[i], w_down,
                           tm=tm_fc, tf=tf, seq=nctx, rows_per_mod=n_ctx)
    return xl.reshape(batch, seq, d)
```

```python
import functools

import jax
import jax.numpy as jnp
from jax import lax
from jax.experimental import pallas as pl
from jax.experimental.pallas import tpu as pltpu

GRID_W = 64
N_MIXERS = 2
ATT_HEADS = 16
ATT_KV_HEADS = 4
ATT_GROUP = ATT_HEADS // ATT_KV_HEADS
HEAD_DIM = 128
WINDOW = 128
ROPE_BASE = 10000.0
GLA_HEADS = 4
GLA_GATE_RANK = 16
GLA_GATE_NORM = 16.0
CONV_W = 3
EPS = 1e-6
N_MOD = 6
MOD_ROWS = 8

LANES = 128
SUBLANES = 8
VMEM_LIMIT_CAP = 60 * 1024 * 1024

BF16 = jnp.bfloat16
F32 = jnp.float32


def _dot(a, b):
    return jnp.dot(a, b, preferred_element_type=F32)


def _dot_nt(a, b):
    return lax.dot_general(a, b, (((1,), (1,)), ((), ())), preferred_element_type=F32)


def _dot_tn(a, b):
    return lax.dot_general(a, b, (((0,), (0,)), ((), ())), preferred_element_type=F32)


def _params(semantics, vmem_bytes):
    limit = min(VMEM_LIMIT_CAP, int(vmem_bytes * 1.25) + (4 << 20))
    return pltpu.CompilerParams(dimension_semantics=semantics, vmem_limit_bytes=limit)


def _sigmoid(x):
    return 1.0 / (1.0 + jnp.exp(-x))


def _silu(x):
    return x * _sigmoid(x)


def _norm_mod(x, g, shift, scale):
    y = x * lax.rsqrt(jnp.mean(x * x, axis=-1, keepdims=True) + EPS)
    return (y * g) * (1.0 + scale) + shift


def _mod_kernel(c_ref, w_ref, b_ref, o_ref):
    s = _silu(c_ref[...]).astype(BF16)
    o_ref[0] = _dot(s, w_ref[0].astype(BF16)) + b_ref[0]


def _modulations(cc, ada_w, ada_b):
    depth, d, n = ada_w.shape
    r = cc.shape[0]
    tn = 1024
    vmem = 2 * (d * tn * 4 + r * tn * 4 + tn * 4) + r * d * 4 + d * tn * 2
    return pl.pallas_call(
        _mod_kernel,
        out_shape=jax.ShapeDtypeStruct((depth, r, n), F32),
        grid=(depth, n // tn),
        in_specs=[
            pl.BlockSpec((r, d), lambda l, j: (0, 0)),
            pl.BlockSpec((1, d, tn), lambda l, j: (l, 0, j)),
            pl.BlockSpec((1, 1, tn), lambda l, j: (l, 0, j)),
        ],
        out_specs=pl.BlockSpec((1, r, tn), lambda l, j: (l, 0, j)),
        compiler_params=_params(("arbitrary", "arbitrary"), vmem),
        name="adaln_mod",
    )(cc, ada_w, ada_b.reshape(depth, 1, n))


def _log_sigmoid(z):
    return jnp.minimum(z, 0.0) - jnp.log(1.0 + jnp.exp(-jnp.abs(z)))


def _proj_kernel(*refs, segments, rope_cols, gated, col_chunk):
    n_out = len(segments) + (2 if gated else 0)
    x_ref, mod_ref, g_ref, w_ref = refs[:4]
    extra = refs[4:len(refs) - n_out - 2]
    outs = refs[len(refs) - n_out - 2:len(refs) - 2]
    h_even, h_odd = refs[-2:]
    s = pl.program_id(0)

    @pl.when(s == 0)
    def _():
        h_odd[...] = jnp.zeros_like(h_odd)

    def body(h_new, h_old):
        h = h_old[...]
        if gated:
            w1_ref, w2f_ref, w2b_ref, bf_ref, bb_ref = extra
            t = _dot(h, w1_ref[...]).astype(BF16)
            for w2_ref, b_ref, o_ref in ((w2f_ref, bf_ref, outs[-2]), (w2b_ref, bb_ref, outs[-1])):
                o_ref[...] = _dot(t, w2_ref[...]) + b_ref[...]
        if rope_cols:
            cos, sa, sb = extra[0][...], extra[1][...], extra[2][...]
        col = 0
        for (width, _), y_ref in zip(segments, outs):
            for c0 in range(0, width, col_chunk):
                y = _dot(h, w_ref[:, col + c0:col + c0 + col_chunk])
                if col + c0 < rope_cols:
                    parts = []
                    for p in range(col_chunk // HEAD_DIM):
                        yh = y[:, p * HEAD_DIM:(p + 1) * HEAD_DIM]
                        if col + c0 + p * HEAD_DIM < rope_cols:
                            up = pltpu.roll(yh, HEAD_DIM - HEAD_DIM // 4, axis=1)
                            dn = pltpu.roll(yh, HEAD_DIM // 4, axis=1)
                            yh = yh * cos + up * sa + dn * sb
                        parts.append(yh)
                    y = jnp.concatenate(parts, axis=1)
                y_ref[:, c0:c0 + col_chunk] = y.astype(y_ref.dtype)
            col += width
        x = x_ref[...]
        gs = g_ref[...] * (1.0 + mod_ref[0, 1:2, :])
        r = lax.rsqrt(jnp.mean(x * x, axis=-1, keepdims=True) + EPS)
        h_new[...] = ((x * r) * gs + mod_ref[0, 0:1, :]).astype(BF16)

    @pl.when(s % 2 == 0)
    def _():
        body(h_even, h_odd)

    @pl.when(s % 2 == 1)
    def _():
        body(h_odd, h_even)


def _project(x, mod, g, w, segments, *, rows_per_mod, tm, rope=None, seq=None, rope_cols=0, gate=None):
    n, d = x.shape
    w_stack, layer = w
    m = w_stack.shape[2]
    nrow = n // tm
    col_chunk = 1024 if all(wd % 1024 == 0 for wd, _ in segments) else 512

    def cur(s):
        return jnp.minimum(s, nrow - 1)

    def prev(s):
        return jnp.maximum(s - 1, 0)

    in_specs = [
        pl.BlockSpec((tm, d), lambda s: (cur(s), 0)),
        pl.BlockSpec((1, MOD_ROWS, d), lambda s: ((cur(s) * tm) // rows_per_mod, 0, 0)),
        pl.BlockSpec((1, d), lambda s: (0, 0)),
        pl.BlockSpec((None, d, m), lambda s: (layer, 0, 0), pipeline_mode=pl.Buffered(1)),
    ]
    args = [x, mod, g.reshape(1, d), w_stack]
    out_shape = [jax.ShapeDtypeStruct((n, wd), dt) for wd, dt in segments]
    out_specs = [pl.BlockSpec((tm, wd), lambda s: (prev(s), 0)) for wd, _ in segments]
    vmem = (2 * tm * d * 4 + d * m * 2 + 2 * sum(tm * wd * jnp.dtype(dt).itemsize for wd, dt in segments)
            + 2 * tm * d * 2 + 3 * tm * col_chunk * 4 + 2 * tm * d * 4)
    if rope is not None:
        tiles_per_seq = seq // tm
        for tab in rope:
            in_specs.append(pl.BlockSpec((tm, HEAD_DIM), lambda s: (prev(s) % tiles_per_seq, 0)))
            args.append(tab)
        vmem += 6 * tm * HEAD_DIM * 4
    if gate is not None:
        w1, w2f, w2b, bf, bb = gate
        dk = w2f.shape[1]
        in_specs += [
            pl.BlockSpec(w1.shape, lambda s: (0, 0)),
            pl.BlockSpec(w2f.shape, lambda s: (0, 0)),
            pl.BlockSpec(w2b.shape, lambda s: (0, 0)),
            pl.BlockSpec((1, dk), lambda s: (0, 0)),
            pl.BlockSpec((1, dk), lambda s: (0, 0)),
        ]
        args += [w1, w2f, w2b, bf.reshape(1, dk), bb.reshape(1, dk)]
        out_shape += [jax.ShapeDtypeStruct((n, dk), F32)] * 2
        out_specs += [pl.BlockSpec((tm, dk), lambda s: (prev(s), 0))] * 2
        vmem += 6 * tm * dk * 4 + 2 * (d * LANES * 2 + 2 * LANES * dk * 2)
    return pl.pallas_call(
        functools.partial(_proj_kernel, segments=tuple(segments), rope_cols=rope_cols, gated=gate is not None,
                          col_chunk=col_chunk),
        out_shape=out_shape,
        grid=(nrow + 1,),
        in_specs=in_specs,
        out_specs=out_specs,
        scratch_shapes=[pltpu.VMEM((tm, d), BF16), pltpu.VMEM((tm, d), BF16)],
        compiler_params=_params(("arbitrary",), vmem),
        name="proj",
    )(*args)


def _attend(qg, kall, vall, mask, sink_row):
    s = _dot_nt(kall, qg) * (HEAD_DIM ** -0.5)
    if mask is not None:
        s = jnp.where(mask, s, -jnp.inf)
    mx = jnp.maximum(jnp.max(s, axis=0, keepdims=True), sink_row)
    e = jnp.exp(s - mx)
    denom = jnp.sum(e, axis=0, keepdims=True) + jnp.exp(sink_row - mx)
    return _dot_tn(vall, e.astype(BF16)) / denom


def _sink_row(sink_ref, kv, rows):
    cols = [jnp.broadcast_to(sink_ref[0:1, kv * ATT_GROUP + h:kv * ATT_GROUP + h + 1], (1, rows))
            for h in range(ATT_GROUP)]
    return jnp.concatenate(cols, axis=1)


def _attn_kernel(*refs, band, tq, seq, sub):
    if band:
        (q_ref, kp_ref, kc_ref, kn_ref, vp_ref, vc_ref, vn_ref, kx_ref, vx_ref,
         x_ref, mod_ref, sink_ref, wo_ref, o_ref) = refs
    else:
        q_ref, kx_ref, vx_ref, x_ref, mod_ref, sink_ref, wo_ref, o_ref = refs
    blk = WINDOW
    nsub = tq // sub if band else 1
    rows = sub if band else tq
    if band:
        kwin = jnp.concatenate([kp_ref[...], kc_ref[...], kn_ref[...]], axis=0)
        vwin = jnp.concatenate([vp_ref[...], vc_ref[...], vn_ref[...]], axis=0)
        t0 = (pl.program_id(0) * tq) % seq
        nctx = kx_ref.shape[0]
        nwin = sub + 2 * blk
        ncol = nwin + nctx
        kj = lax.broadcasted_iota(jnp.int32, (ncol, ATT_GROUP * sub), 0)
        qi = lax.broadcasted_iota(jnp.int32, (ncol, ATT_GROUP * sub), 1) & (sub - 1)
        in_band = jnp.abs(qi - kj + blk) <= WINDOW
    outs = []
    for sb in range(nsub):
        heads = [None] * ATT_HEADS
        for kv in range(ATT_KV_HEADS):
            lo = kv * HEAD_DIM
            qg = jnp.concatenate(
                [q_ref[sb * rows:(sb + 1) * rows, (kv * ATT_GROUP + h) * HEAD_DIM:(kv * ATT_GROUP + h + 1) * HEAD_DIM]
                 for h in range(ATT_GROUP)], axis=0)
            if band:
                kall = jnp.concatenate([kwin[sb * sub:sb * sub + nwin, lo:lo + HEAD_DIM],
                                        kx_ref[:, lo:lo + HEAD_DIM]], axis=0)
                vall = jnp.concatenate([vwin[sb * sub:sb * sub + nwin, lo:lo + HEAD_DIM],
                                        vx_ref[:, lo:lo + HEAD_DIM]], axis=0)
                kpos = t0 + sb * sub - blk + kj
                mask = (kj >= nwin) | (in_band & (kpos >= 0) & (kpos < seq))
            else:
                kall = kx_ref[:, lo:lo + HEAD_DIM]
                vall = vx_ref[:, lo:lo + HEAD_DIM]
                mask = None
            og = _attend(qg, kall, vall, mask, _sink_row(sink_ref, kv, rows))
            for h in range(ATT_GROUP):
                heads[kv * ATT_GROUP + h] = og[:, h * rows:(h + 1) * rows].T.astype(BF16)
        outs.append(jnp.concatenate(heads, axis=1))
    o = outs[0] if len(outs) == 1 else jnp.concatenate(outs, axis=0)
    o_ref[...] = x_ref[...] + mod_ref[0, 2:3, :] * _dot(o, wo_ref[...])


def _attention(qkv, qkv_ctx, x, mod, sink, w_o, *, band, tq, seq, nctx, rows_per_mod):
    n, d = x.shape
    wo_stack, layer = w_o
    nq = ATT_HEADS * HEAD_DIM
    nkv = ATT_KV_HEADS * HEAD_DIM
    kcol, vcol = nq // nkv, nq // nkv + 1
    blk = WINDOW
    nblk = n // blk
    r = tq // blk
    q_spec = pl.BlockSpec((tq, nq), lambda i: (i, 0))
    tail_specs = [
        pl.BlockSpec((tq, d), lambda i: (i, 0)),
        pl.BlockSpec((1, MOD_ROWS, d), lambda i: ((i * tq) // rows_per_mod, 0, 0)),
        pl.BlockSpec((1, ATT_HEADS), lambda i: (0, 0)),
        pl.BlockSpec((None, nq, d), lambda i: (layer, 0, 0)),
    ]
    tail_args = [x, mod, sink.reshape(1, ATT_HEADS), wo_stack]
    if band:
        def ctx_idx(col):
            return lambda i: ((i * tq) // seq, col)
        in_specs = [
            q_spec,
            pl.BlockSpec((blk, nkv), lambda i: (jnp.maximum(i * r - 1, 0), kcol)),
            pl.BlockSpec((tq, nkv), lambda i: (i, kcol)),
            pl.BlockSpec((blk, nkv), lambda i: (jnp.minimum((i + 1) * r, nblk - 1), kcol)),
            pl.BlockSpec((blk, nkv), lambda i: (jnp.maximum(i * r - 1, 0), vcol)),
            pl.BlockSpec((tq, nkv), lambda i: (i, vcol)),
            pl.BlockSpec((blk, nkv), lambda i: (jnp.minimum((i + 1) * r, nblk - 1), vcol)),
            pl.BlockSpec((nctx, nkv), ctx_idx(kcol)),
            pl.BlockSpec((nctx, nkv), ctx_idx(vcol)),
        ] + tail_specs
        args = [qkv] * 7 + [qkv_ctx] * 2 + tail_args
        ncol = 3 * blk + nctx
    else:
        in_specs = [
            q_spec,
            pl.BlockSpec((tq, nkv), lambda i: (i, kcol)),
            pl.BlockSpec((tq, nkv), lambda i: (i, vcol)),
        ] + tail_specs
        args = [qkv] * 3 + tail_args
        ncol = tq
    vmem = (2 * (tq * nq * 2 + 4 * (tq + 2 * blk + nctx) * nkv * 2 + 2 * tq * d * 4 + nq * d * 2)
            + 6 * ATT_GROUP * blk * ncol * 4 + 2 * tq * d * 4)
    return pl.pallas_call(
        functools.partial(_attn_kernel, band=band, tq=tq, seq=seq, sub=min(tq, WINDOW)),
        out_shape=jax.ShapeDtypeStruct((n, d), F32),
        grid=(n // tq,),
        in_specs=in_specs,
        out_specs=pl.BlockSpec((tq, d), lambda i: (i, 0)),
        compiler_params=_params(("arbitrary",), vmem),
        name="attn_band" if band else "attn_ctx",
    )(*args)


def _gla_scan_kernel(qkf_ref, vf_ref, gf_ref, qkb_ref, vb_ref, gb_ref, s0f_ref, s0b_ref,
                     of_ref, ob_ref, sf_ref, sb_ref, *scratch, chunk, dk, dv, q_scale):
    per_set = len(scratch) // 2
    sets = (scratch[:per_set], scratch[per_set:])
    s = pl.program_id(1)

    @pl.when(s == 0)
    def _():
        sf_ref[...] = s0f_ref[...]
        sb_ref[...] = s0b_ref[...]
        for ref in sets[1]:
            ref[...] = jnp.zeros_like(ref) if ref.dtype == BF16 else jnp.ones_like(ref)

    row = lax.broadcasted_iota(jnp.int32, (chunk, chunk), 0)
    col = lax.broadcasted_iota(jnp.int32, (chunk, chunk), 1)
    directions = (
        (qkf_ref, vf_ref, gf_ref, of_ref, sf_ref, col <= row, chunk - 1),
        (qkb_ref, vb_ref, gb_ref, ob_ref, sb_ref, col >= row, 0),
    )

    def body(new, old):
        for d, (_, v_ref, _, o_ref, s_ref, tri, _) in enumerate(directions):
            qa_ref, ka_ref, qs_ref, ks_ref, decay_ref = old[5 * d:5 * d + 5]
            v = v_ref[...]
            for h in range(GLA_HEADS):
                kc = slice(h * dk, (h + 1) * dk)
                vc = slice(h * dv, (h + 1) * dv)
                a = jnp.where(tri, _dot_nt(qa_ref[:, kc], ka_ref[:, kc]), 0.0).astype(BF16)
                st = s_ref[0, h]
                o_ref[:, vc] = _dot(a, v[:, vc]) + _dot(qs_ref[:, kc], st.astype(BF16))
                upd = _dot_tn(ks_ref[:, kc], v[:, vc])
                decay = decay_ref[kc, :]
                s_ref[0, h] = jnp.concatenate(
                    [st[:, t * LANES:(t + 1) * LANES] * decay for t in range(dv // LANES)], axis=1) + upd
        for d, (qk_ref, _, g_ref, _, _, tri, last) in enumerate(directions):
            qa_ref, ka_ref, qs_ref, ks_ref, decay_ref = new[5 * d:5 * d + 5]
            tri_b = tri.astype(BF16)
            g = _log_sigmoid(g_ref[...]) / GLA_GATE_NORM
            g_hi = g.astype(BF16)
            g_lo = (g - g_hi.astype(F32)).astype(BF16)
            b = _dot(tri_b, g_hi) + _dot(tri_b, g_lo)
            b_tot = b[last:last + 1]
            b_mid = b[chunk // 2:chunk // 2 + 1]
            hk = b.shape[1]
            q = qk_ref[:, :hk] * q_scale
            k = qk_ref[:, hk:]
            qa_ref[...] = (q * jnp.exp(b - b_mid)).astype(BF16)
            ka_ref[...] = (k * jnp.exp(b_mid - b)).astype(BF16)
            qs_ref[...] = (q * jnp.exp(b)).astype(BF16)
            ks_ref[...] = (k * jnp.exp(b_tot - b)).astype(BF16)
            decay_ref[...] = jnp.exp(jnp.broadcast_to(b_tot, (LANES, b.shape[1]))).T

    @pl.when(s % 2 == 0)
    def _():
        body(sets[0], sets[1])

    @pl.when(s % 2 == 1)
    def _():
        body(sets[1], sets[0])


def _gla_scan(qk, v, g_f, g_b, s0_f, s0_b, *, batch, length, chunk, dk, dv):
    nc = length // chunk
    hk, hv = GLA_HEADS * dk, GLA_HEADS * dv

    def nxt(c):
        return jnp.minimum(c, nc - 1)

    def cur(c):
        return jnp.maximum(c - 1, 0)

    def fwd(b, c):
        return b * nc + c

    def bwd(b, c):
        return b * nc + nc - 1 - c

    state_spec = pl.BlockSpec((1, GLA_HEADS, dk, dv), lambda b, c: (b, 0, 0, 0))
    in_specs = []
    for rows in (fwd, bwd):
        in_specs += [
            pl.BlockSpec((chunk, 2 * hk), lambda b, c, rows=rows: (rows(b, nxt(c)), 0)),
            pl.BlockSpec((chunk, hv), lambda b, c, rows=rows: (rows(b, cur(c)), 0)),
            pl.BlockSpec((chunk, hk), lambda b, c, rows=rows: (rows(b, nxt(c)), 0)),
        ]
    in_specs += [state_spec, state_spec]
    operand_set = [pltpu.VMEM((chunk, hk), BF16)] * 4 + [pltpu.VMEM((hk, LANES), F32)]
    state_bytes = GLA_HEADS * dk * dv * 4
    vmem = (2 * 2 * (3 * chunk * hk * 4 + chunk * hv * 2 + chunk * hv * 4) + 8 * state_bytes
            + 16 * chunk * hk * 4 + 4 * dk * dv * 4 + 4 * (4 * chunk * hk * 2 + hk * LANES * 4))
    return pl.pallas_call(
        functools.partial(_gla_scan_kernel, chunk=chunk, dk=dk, dv=dv, q_scale=dk ** -0.5),
        out_shape=[jax.ShapeDtypeStruct((batch * length, hv), F32)] * 2
        + [jax.ShapeDtypeStruct(s0_f.shape, F32)] * 2,
        grid=(batch, nc + 1),
        in_specs=in_specs,
        out_specs=[
            pl.BlockSpec((chunk, hv), lambda b, c: (fwd(b, cur(c)), 0)),
            pl.BlockSpec((chunk, hv), lambda b, c: (bwd(b, cur(c)), 0)),
            state_spec,
            state_spec,
        ],
        scratch_shapes=operand_set * 4,
        compiler_params=_params(("arbitrary", "arbitrary"), vmem),
        name="gla_scan",
    )(qk, v, g_f, qk, v, g_b, s0_f, s0_b)


def _gla_out_kernel(of_ref, ob_ref, r_ref, x_ref, mod_ref, g_ref, wo_ref, o_ref, *, dv):
    o = of_ref[...] + ob_ref[...]
    parts = []
    for h in range(o.shape[1] // dv):
        oh = o[:, h * dv:(h + 1) * dv]
        parts.append((oh * lax.rsqrt(jnp.mean(oh * oh, axis=-1, keepdims=True) + EPS)) * g_ref[...])
    gated = (jnp.concatenate(parts, axis=1) * _silu(r_ref[...])).astype(BF16)
    o_ref[...] = x_ref[...] + mod_ref[0, 2:3, :] * _dot(gated, wo_ref[...])


def _gla_output(o_f, o_b, r, x, mod, onorm_g, w_o, *, tm, rows_per_mod, dv):
    n, d = x.shape
    wo_stack, layer = w_o
    dvt = o_f.shape[1]
    vmem = 2 * (5 * tm * dvt * 4 + dvt * d * 2) + 4 * tm * dvt * 4
    return pl.pallas_call(
        functools.partial(_gla_out_kernel, dv=dv),
        out_shape=jax.ShapeDtypeStruct((n, d), F32),
        grid=(n // tm,),
        in_specs=[
            pl.BlockSpec((tm, dvt), lambda i: (i, 0)),
            pl.BlockSpec((tm, dvt), lambda i: (i, 0)),
            pl.BlockSpec((tm, dvt), lambda i: (i, 0)),
            pl.BlockSpec((tm, d), lambda i: (i, 0)),
            pl.BlockSpec((1, MOD_ROWS, d), lambda i: ((i * tm) // rows_per_mod, 0, 0)),
            pl.BlockSpec((1, dv), lambda i: (0, 0)),
            pl.BlockSpec((None, dvt, d), lambda i: (layer, 0, 0)),
        ],
        out_specs=pl.BlockSpec((tm, d), lambda i: (i, 0)),
        compiler_params=_params(("arbitrary",), vmem),
        name="gla_out",
    )(o_f, o_b, r, x, mod, onorm_g.reshape(1, dv), wo_stack)


def _ffn_kernel(xp_ref, x_ref, xn_ref, mod_ref, g_ref, fg_ref, wg_ref, wv_ref, cwg_ref, cwv_ref, cbg_ref, cbv_ref,
                wd_ref, o_ref, h_scr, ug_scr, uv_scr, act_scr, *, tm, seq, final_norm):
    i = pl.program_id(0)
    j = pl.program_id(1)
    halo = SUBLANES
    rows = min(tm, seq, 256)
    assert tm <= seq or (rows == seq and tm % seq == 0)

    @pl.when(j == 0)
    def _():
        shift = mod_ref[0, 3:4, :]
        gs = g_ref[...] * (1.0 + mod_ref[0, 4:5, :])

        def norm_mod(x):
            return (x * lax.rsqrt(jnp.mean(x * x, axis=-1, keepdims=True) + EPS)) * gs + shift

        first = (i * tm) % seq == 0
        last = ((i + 1) * tm) % seq == 0
        h_scr[0:halo, :] = jnp.where(first, 0.0, norm_mod(xp_ref[...])).astype(BF16)
        for r0 in range(0, tm, rows):
            h_scr[halo + r0:halo + r0 + rows, :] = norm_mod(x_ref[r0:r0 + rows, :]).astype(BF16)
        h_scr[halo + tm:2 * halo + tm, :] = jnp.where(last, 0.0, norm_mod(xn_ref[...])).astype(BF16)
        o_ref[...] = jnp.zeros_like(o_ref)

    h = h_scr[...]
    ug_scr[...] = _dot(h, wg_ref[...])
    uv_scr[...] = _dot(h, wv_ref[...])

    def conv(u_scr, cw_ref, cb_ref, r0):
        prev = u_scr[halo - 1 + r0:halo - 1 + r0 + rows, :]
        nxt = u_scr[halo + 1 + r0:halo + 1 + r0 + rows, :]
        if tm > seq:
            rid = lax.broadcasted_iota(jnp.int32, prev.shape, 0)
            prev = jnp.where(rid == 0, 0.0, prev)
            nxt = jnp.where(rid == rows - 1, 0.0, nxt)
        return (cb_ref[...] + cw_ref[0:1, :] * prev + cw_ref[1:2, :] * u_scr[halo + r0:halo + r0 + rows, :]
                + cw_ref[2:3, :] * nxt)

    for r0 in range(0, tm, rows):
        act_scr[r0:r0 + rows, :] = (_silu(conv(ug_scr, cwg_ref, cbg_ref, r0))
                                    * conv(uv_scr, cwv_ref, cbv_ref, r0)).astype(BF16)
    act = act_scr[...]
    for c0 in range(0, o_ref.shape[1], 512):
        o_ref[:, c0:c0 + 512] += _dot(act, wd_ref[:, c0:c0 + 512])

    @pl.when(j == pl.num_programs(1) - 1)
    def _():
        er = min(tm, 128)

        def residual(c, carry):
            sl = pl.ds(pl.multiple_of(c * er, er), er)
            y = x_ref[sl, :] + mod_ref[0, 5:6, :] * o_ref[sl, :]
            if final_norm:
                y = (y * lax.rsqrt(jnp.mean(y * y, axis=-1, keepdims=True) + EPS)) * fg_ref[...]
            o_ref[sl, :] = y
            return carry

        lax.fori_loop(0, tm // er, residual, 0)


def _conv_ffn(x, mod, g, w_up, conv_w, conv_b, w_down, *, tm, tf, seq, rows_per_mod, final_g=None):
    n, d = x.shape
    (up_stack, layer), (down_stack, _) = w_up, w_down
    dff = down_stack.shape[1]
    nf = dff // tf
    halo = SUBLANES
    r = tm // halo
    nhalo = n // halo
    conv_b = conv_b.reshape(1, 2 * dff)
    fg = (g if final_g is None else final_g).reshape(1, d)
    vmem = (2 * (2 * tm * d * 4 + 2 * d * tf * 2 + tf * d * 2) + (tm + 2 * halo) * d * 2
            + 2 * (tm + 2 * halo) * tf * 4 + tm * tf * 2 + 8 * min(tm, 256) * max(tf, d) * 4)
    return pl.pallas_call(
        functools.partial(_ffn_kernel, tm=tm, seq=seq, final_norm=final_g is not None),
        out_shape=jax.ShapeDtypeStruct((n, d), F32),
        grid=(n // tm, nf),
        in_specs=[
            pl.BlockSpec((halo, d), lambda i, j: (jnp.maximum(i * r - 1, 0), 0)),
            pl.BlockSpec((tm, d), lambda i, j: (i, 0)),
            pl.BlockSpec((halo, d), lambda i, j: (jnp.minimum((i + 1) * r, nhalo - 1), 0)),
            pl.BlockSpec((1, MOD_ROWS, d), lambda i, j: ((i * tm) // rows_per_mod, 0, 0)),
            pl.BlockSpec((1, d), lambda i, j: (0, 0)),
            pl.BlockSpec((1, d), lambda i, j: (0, 0)),
            pl.BlockSpec((None, d, tf), lambda i, j: (layer, 0, j)),
            pl.BlockSpec((None, d, tf), lambda i, j: (layer, 0, nf + j)),
            pl.BlockSpec((CONV_W, tf), lambda i, j: (0, j)),
            pl.BlockSpec((CONV_W, tf), lambda i, j: (0, nf + j)),
            pl.BlockSpec((1, tf), lambda i, j: (0, j)),
            pl.BlockSpec((1, tf), lambda i, j: (0, nf + j)),
            pl.BlockSpec((None, tf, d), lambda i, j: (layer, j, 0)),
        ],
        out_specs=pl.BlockSpec((tm, d), lambda i, j: (i, 0)),
        scratch_shapes=[
            pltpu.VMEM((tm + 2 * halo, d), BF16),
            pltpu.VMEM((tm + 2 * halo, tf), F32),
            pltpu.VMEM((tm + 2 * halo, tf), F32),
            pltpu.VMEM((tm, tf), BF16),
        ],
        compiler_params=_params(("arbitrary", "arbitrary"), vmem),
        name="conv_ffn",
    )(x, x, x, mod, g.reshape(1, d), fg, up_stack, up_stack, conv_w, conv_w, conv_b, conv_b, down_stack)


def _rope_tables(seq):
    half = HEAD_DIM // 2
    rows = seq // GRID_W
    row = jnp.repeat(jnp.arange(rows, dtype=F32), GRID_W)
    col = jnp.tile(jnp.arange(GRID_W, dtype=F32), rows)
    inv = ROPE_BASE ** (-jnp.arange(0, half, 2, dtype=F32) / half)
    ang_r = row[:, None] * inv[None, :]
    ang_c = col[:, None] * inv[None, :]
    zero = jnp.zeros_like(ang_r)
    cos = jnp.concatenate([jnp.cos(ang_r)] * 2 + [jnp.cos(ang_c)] * 2, axis=1)
    sa = jnp.concatenate([-jnp.sin(ang_r), zero, -jnp.sin(ang_c), zero], axis=1)
    sb = jnp.concatenate([zero, jnp.sin(ang_r), zero, jnp.sin(ang_c)], axis=1)
    return cos, sa, sb


def _tile(n, target):
    t = min(n, target)
    while n % t:
        t //= 2
    return t


def kernel(x, c, ctx, c_ctx, ada_w, ada_b, norm_mix_g, norm_ffn_g, ffn_w_up, ffn_conv_w, ffn_conv_b, ffn_w_down,
           attn_w_qkv, attn_sink, attn_w_o, gla_w_in, gla_gf_w1, gla_gf_w2, gla_gf_b, gla_gb_w1, gla_gb_w2,
           gla_gb_b, gla_onorm_g, gla_w_o, final_norm_g):
    batch, seq, d = x.shape
    nctx = ctx.shape[1]
    depth = ada_w.shape[0]
    n_lat, n_ctx = batch * seq, batch * nctx
    dk = gla_gf_w2.shape[2] // GLA_HEADS
    dv = gla_onorm_g.shape[1]
    nq = ATT_HEADS * HEAD_DIM
    assert seq % WINDOW == 0 and nctx % WINDOW == 0 and d % LANES == 0

    n_cond = batch + 1
    pad = (-n_cond) % SUBLANES
    cc = jnp.concatenate([c, c_ctx[None, :], jnp.zeros((pad, d), F32)], axis=0)
    mods = _modulations(cc, ada_w, ada_b)
    mods = mods.reshape(depth, n_cond + pad, N_MOD, d)
    mods = jnp.pad(mods, ((0, 0), (0, 0), (0, MOD_ROWS - N_MOD), (0, 0)))

    rope = _rope_tables(seq)
    xl = x.reshape(n_lat, d)
    xc = ctx.reshape(n_ctx, d)

    tm_l, tm_c = _tile(seq, 512), _tile(nctx, 512)
    tm_pc = _tile(n_ctx, 512)
    tm_fc = _tile(n_ctx, 1024) if nctx <= 256 and _tile(n_ctx, 1024) % nctx == 0 else tm_c
    tq_l = _tile(seq, 512)
    gla_chunk = _tile(nctx, 128)
    tf = 512
    nkv = ATT_KV_HEADS * HEAD_DIM
    hk, hv = GLA_HEADS * dk, GLA_HEADS * dv

    w_up_all, w_down_all = ffn_w_up.astype(BF16), ffn_w_down.astype(BF16)
    w_qkv_all, attn_wo_all = attn_w_qkv.astype(BF16), attn_w_o.astype(BF16)
    gla_qk_all, gla_vr_all = gla_w_in[:, :, :2 * hk].astype(BF16), gla_w_in[:, :, 2 * hk:].astype(BF16)
    gla_wo_all = gla_w_o.astype(BF16)

    for i in range(depth):
        last = i == depth - 1
        mod_l = mods[i, :batch]
        mod_c = mods[i, batch:batch + 1]
        j = i // N_MIXERS
        if i % N_MIXERS == 0:
            w_qkv, w_o = (w_qkv_all, j), (attn_wo_all, j)
            seg = [(nq + 2 * nkv, BF16)]
            qkv_l, = _project(xl, mod_l, norm_mix_g[i], w_qkv, seg, rows_per_mod=seq, tm=tm_l,
                              rope=rope, seq=seq, rope_cols=nq + nkv)
            qkv_c, = _project(xc, mod_c, norm_mix_g[i], w_qkv, seg, rows_per_mod=n_ctx, tm=tm_pc)
            xl = _attention(qkv_l, qkv_c, xl, mod_l, attn_sink[j], w_o, band=True, tq=tq_l, seq=seq, nctx=nctx,
                            rows_per_mod=seq)
            if not last:
                xc = _attention(qkv_c, None, xc, mod_c, attn_sink[j], w_o, band=False, tq=nctx, seq=nctx, nctx=nctx,
                                rows_per_mod=n_ctx)
        else:
            w_qk, w_vr, w_o = (gla_qk_all, j), (gla_vr_all, j), (gla_wo_all, j)
            rank = gla_gf_w1.shape[2]
            w1 = jnp.zeros((d, LANES), F32).at[:, :rank].set(gla_gf_w1[j]).at[:, rank:2 * rank].set(gla_gb_w1[j])
            w2f = jnp.zeros((LANES, hk), F32).at[:rank].set(gla_gf_w2[j])
            w2b = jnp.zeros((LANES, hk), F32).at[rank:2 * rank].set(gla_gb_w2[j])
            gate = (w1.astype(BF16), w2f.astype(BF16), w2b.astype(BF16), gla_gf_b[j], gla_gb_b[j])
            streams = []
            for xs, mod, rpm, tm in ((xc, mod_c, n_ctx, tm_pc), (xl, mod_l, seq, tm_l)):
                qk, gf, gb = _project(xs, mod, norm_mix_g[i], w_qk, [(2 * hk, F32)], rows_per_mod=rpm, tm=tm,
                                      gate=gate)
                v, r = _project(xs, mod, norm_mix_g[i], w_vr, [(hv, BF16), (hv, F32)], rows_per_mod=rpm, tm=tm)
                streams.append((qk, v, r, gf, gb))
            s0 = jnp.zeros((batch, GLA_HEADS, dk, dv), F32)
            scan = functools.partial(_gla_scan, batch=batch, chunk=gla_chunk, dk=dk, dv=dv)
            (qk_c, v_c, r_c, gf_c, gb_c), (qk_l, v_l, r_l, gf_l, gb_l) = streams
            of_c, ob_c, sf, sb = scan(qk_c, v_c, gf_c, gb_c, s0, s0, length=nctx)
            of_l, ob_l, _, _ = scan(qk_l, v_l, gf_l, gb_l, sf, sb, length=seq)
            xl = _gla_output(of_l, ob_l, r_l, xl, mod_l, gla_onorm_g[j], w_o, tm=tm_l, rows_per_mod=seq, dv=dv)
            if not last:
                xc = _gla_output(of_c, ob_c, r_c, xc, mod_c, gla_onorm_g[j], w_o, tm=tm_c, rows_per_mod=n_ctx, dv=dv)
        w_up, w_down = (w_up_all, i), (w_down_all, i)
        xl = _conv_ffn(xl, mod_l, norm_ffn_g[i], w_up, ffn_conv_w[i], ffn_conv_b[i], w_down,
                       tm=_tile(seq, 1024), tf=tf, seq=seq, rows_per_mod=seq, final_g=final_norm_g if last else None)
        if not last:
            xc = _conv_ffn(xc, mod_c, norm_ffn_g[i], w_up, ffn_conv_w[i], ffn_conv_b[i], w_down,
                           tm=tm_fc, tf=tf, seq=nctx, rows_per_mod=n_ctx)
    return xl.reshape(batch, seq, d)
```

```python
import functools

import jax
import jax.numpy as jnp
from jax import lax
from jax.experimental import pallas as pl
from jax.experimental.pallas import tpu as pltpu

GRID_W = 64
N_MIXERS = 2
ATT_HEADS = 16
ATT_KV_HEADS = 4
ATT_GROUP = ATT_HEADS // ATT_KV_HEADS
HEAD_DIM = 128
WINDOW = 128
ROPE_BASE = 10000.0
GLA_HEADS = 4
GLA_GATE_RANK = 16
GLA_GATE_NORM = 16.0
CONV_W = 3
EPS = 1e-6
N_MOD = 6
MOD_ROWS = 8

LANES = 128
SUBLANES = 8
VMEM_LIMIT_CAP = 60 * 1024 * 1024

BF16 = jnp.bfloat16
F32 = jnp.float32


def _dot(a, b):
    return jnp.dot(a, b, preferred_element_type=F32)


def _dot_nt(a, b):
    return lax.dot_general(a, b, (((1,), (1,)), ((), ())), preferred_element_type=F32)


def _dot_tn(a, b):
    return lax.dot_general(a, b, (((0,), (0,)), ((), ())), preferred_element_type=F32)


def _params(semantics, vmem_bytes):
    limit = min(VMEM_LIMIT_CAP, int(vmem_bytes * 1.25) + (4 << 20))
    return pltpu.CompilerParams(dimension_semantics=semantics, vmem_limit_bytes=limit)


def _sigmoid(x):
    return 1.0 / (1.0 + jnp.exp(-x))


def _silu(x):
    return x * _sigmoid(x)


def _norm_mod(x, g, shift, scale):
    y = x * lax.rsqrt(jnp.mean(x * x, axis=-1, keepdims=True) + EPS)
    return (y * g) * (1.0 + scale) + shift


def _mod_kernel(c_ref, w_ref, b_ref, o_ref):
    s = _silu(c_ref[...]).astype(BF16)
    o_ref[0] = _dot(s, w_ref[0].astype(BF16)) + b_ref[0]


def _modulations(cc, ada_w, ada_b):
    depth, d, n = ada_w.shape
    r = cc.shape[0]
    tn = 1024
    vmem = 2 * (d * tn * 4 + r * tn * 4 + tn * 4) + r * d * 4 + d * tn * 2
    return pl.pallas_call(
        _mod_kernel,
        out_shape=jax.ShapeDtypeStruct((depth, r, n), F32),
        grid=(depth, n // tn),
        in_specs=[
            pl.BlockSpec((r, d), lambda l, j: (0, 0)),
            pl.BlockSpec((1, d, tn), lambda l, j: (l, 0, j)),
            pl.BlockSpec((1, 1, tn), lambda l, j: (l, 0, j)),
        ],
        out_specs=pl.BlockSpec((1, r, tn), lambda l, j: (l, 0, j)),
        compiler_params=_params(("arbitrary", "arbitrary"), vmem),
        name="adaln_mod",
    )(cc, ada_w, ada_b.reshape(depth, 1, n))


def _log_sigmoid(z):
    return jnp.minimum(z, 0.0) - jnp.log(1.0 + jnp.exp(-jnp.abs(z)))


def _proj_kernel(*refs, segments, rope_cols, gated, col_chunk):
    n_out = len(segments) + (2 if gated else 0)
    x_ref, mod_ref, g_ref, w_ref = refs[:4]
    extra = refs[4:len(refs) - n_out - 2]
    outs = refs[len(refs) - n_out - 2:len(refs) - 2]
    h_even, h_odd = refs[-2:]
    s = pl.program_id(0)

    @pl.when(s == 0)
    def _():
        h_odd[...] = jnp.zeros_like(h_odd)

    def body(h_new, h_old):
        h = h_old[...]
        if gated:
            w1_ref, w2f_ref, w2b_ref, bf_ref, bb_ref = extra
            t = _dot(h, w1_ref[...]).astype(BF16)
            for w2_ref, b_ref, o_ref in ((w2f_ref, bf_ref, outs[-2]), (w2b_ref, bb_ref, outs[-1])):
                o_ref[...] = _dot(t, w2_ref[...]) + b_ref[...]
        if rope_cols:
            cos, sa, sb = extra[0][...], extra[1][...], extra[2][...]
        col = 0
        for (width, _), y_ref in zip(segments, outs):
            for c0 in range(0, width, col_chunk):
                y = _dot(h, w_ref[:, col + c0:col + c0 + col_chunk])
                if col + c0 < rope_cols:
                    parts = []
                    for p in range(col_chunk // HEAD_DIM):
                        yh = y[:, p * HEAD_DIM:(p + 1) * HEAD_DIM]
                        if col + c0 + p * HEAD_DIM < rope_cols:
                            up = pltpu.roll(yh, HEAD_DIM - HEAD_DIM // 4, axis=1)
                            dn = pltpu.roll(yh, HEAD_DIM // 4, axis=1)
                            yh = yh * cos + up * sa + dn * sb
                        parts.append(yh)
                    y = jnp.concatenate(parts, axis=1)
                y_ref[:, c0:c0 + col_chunk] = y.astype(y_ref.dtype)
            col += width
        x = x_ref[...]
        gs = g_ref[...] * (1.0 + mod_ref[0, 1:2, :])
        r = lax.rsqrt(jnp.mean(x * x, axis=-1, keepdims=True) + EPS)
        h_new[...] = ((x * r) * gs + mod_ref[0, 0:1, :]).astype(BF16)

    @pl.when(s % 2 == 0)
    def _():
        body(h_even, h_odd)

    @pl.when(s % 2 == 1)
    def _():
        body(h_odd, h_even)


def _project(x, mod, g, w, segments, *, rows_per_mod, tm, rope=None, seq=None, rope_cols=0, gate=None):
    n, d = x.shape
    w_stack, layer = w
    m = w_stack.shape[2]
    nrow = n // tm
    col_chunk = 1024 if all(wd % 1024 == 0 for wd, _ in segments) else 512

    def cur(s):
        return jnp.minimum(s, nrow - 1)

    def prev(s):
        return jnp.maximum(s - 1, 0)

    in_specs = [
        pl.BlockSpec((tm, d), lambda s: (cur(s), 0)),
        pl.BlockSpec((1, MOD_ROWS, d), lambda s: ((cur(s) * tm) // rows_per_mod, 0, 0)),
        pl.BlockSpec((1, d), lambda s: (0, 0)),
        pl.BlockSpec((None, d, m), lambda s: (layer, 0, 0), pipeline_mode=pl.Buffered(1)),
    ]
    args = [x, mod, g.reshape(1, d), w_stack]
    out_shape = [jax.ShapeDtypeStruct((n, wd), dt) for wd, dt in segments]
    out_specs = [pl.BlockSpec((tm, wd), lambda s: (prev(s), 0)) for wd, _ in segments]
    vmem = (2 * tm * d * 4 + d * m * 2 + 2 * sum(tm * wd * jnp.dtype(dt).itemsize for wd, dt in segments)
            + 2 * tm * d * 2 + 3 * tm * col_chunk * 4 + 2 * tm * d * 4)
    if rope is not None:
        tiles_per_seq = seq // tm
        for tab in rope:
            in_specs.append(pl.BlockSpec((tm, HEAD_DIM), lambda s: (prev(s) % tiles_per_seq, 0)))
            args.append(tab)
        vmem += 6 * tm * HEAD_DIM * 4
    if gate is not None:
        w1, w2f, w2b, bf, bb = gate
        dk = w2f.shape[1]
        in_specs += [
            pl.BlockSpec(w1.shape, lambda s: (0, 0)),
            pl.BlockSpec(w2f.shape, lambda s: (0, 0)),
            pl.BlockSpec(w2b.shape, lambda s: (0, 0)),
            pl.BlockSpec((1, dk), lambda s: (0, 0)),
            pl.BlockSpec((1, dk), lambda s: (0, 0)),
        ]
        args += [w1, w2f, w2b, bf.reshape(1, dk), bb.reshape(1, dk)]
        out_shape += [jax.ShapeDtypeStruct((n, dk), F32)] * 2
        out_specs += [pl.BlockSpec((tm, dk), lambda s: (prev(s), 0))] * 2
        vmem += 6 * tm * dk * 4 + 2 * (d * LANES * 2 + 2 * LANES * dk * 2)
    return pl.pallas_call(
        functools.partial(_proj_kernel, segments=tuple(segments), rope_cols=rope_cols, gated=gate is not None,
                          col_chunk=col_chunk),
        out_shape=out_shape,
        grid=(nrow + 1,),
        in_specs=in_specs,
        out_specs=out_specs,
        scratch_shapes=[pltpu.VMEM((tm, d), BF16), pltpu.VMEM((tm, d), BF16)],
        compiler_params=_params(("arbitrary",), vmem),
        name="proj",
    )(*args)


def _attend(qg, kall, vall, mask, sink_row):
    s = _dot_nt(kall, qg) * (HEAD_DIM ** -0.5)
    if mask is not None:
        s = jnp.where(mask, s, -jnp.inf)
    mx = jnp.maximum(jnp.max(s, axis=0, keepdims=True), sink_row)
    e = jnp.exp(s - mx)
    denom = jnp.sum(e, axis=0, keepdims=True) + jnp.exp(sink_row - mx)
    return _dot_tn(vall, e.astype(BF16)) / denom


def _sink_row(sink_ref, kv, rows):
    cols = [jnp.broadcast_to(sink_ref[0:1, kv * ATT_GROUP + h:kv * ATT_GROUP + h + 1], (1, rows))
            for h in range(ATT_GROUP)]
    return jnp.concatenate(cols, axis=1)


def _attn_kernel(*refs, band, tq, seq, sub):
    if band:
        (q_ref, kp_ref, kc_ref, kn_ref, vp_ref, vc_ref, vn_ref, kx_ref, vx_ref,
         x_ref, mod_ref, sink_ref, wo_ref, o_ref) = refs
    else:
        q_ref, kx_ref, vx_ref, x_ref, mod_ref, sink_ref, wo_ref, o_ref = refs
    blk = WINDOW
    nsub = tq // sub if band else 1
    rows = sub if band else tq
    if band:
        kwin = jnp.concatenate([kp_ref[...], kc_ref[...], kn_ref[...]], axis=0)
        vwin = jnp.concatenate([vp_ref[...], vc_ref[...], vn_ref[...]], axis=0)
        t0 = (pl.program_id(0) * tq) % seq
        nctx = kx_ref.shape[0]
        nwin = sub + 2 * blk
        ncol = nwin + nctx
        kj = lax.broadcasted_iota(jnp.int32, (ncol, ATT_GROUP * sub), 0)
        qi = lax.broadcasted_iota(jnp.int32, (ncol, ATT_GROUP * sub), 1) & (sub - 1)
        in_band = jnp.abs(qi - kj + blk) <= WINDOW
    outs = []
    for sb in range(nsub):
        heads = [None] * ATT_HEADS
        for kv in range(ATT_KV_HEADS):
            lo = kv * HEAD_DIM
            qg = jnp.concatenate(
                [q_ref[sb * rows:(sb + 1) * rows, (kv * ATT_GROUP + h) * HEAD_DIM:(kv * ATT_GROUP + h + 1) * HEAD_DIM]
                 for h in range(ATT_GROUP)], axis=0)
            if band:
                kall = jnp.concatenate([kwin[sb * sub:sb * sub + nwin, lo:lo + HEAD_DIM],
                                        kx_ref[:, lo:lo + HEAD_DIM]], axis=0)
                vall = jnp.concatenate([vwin[sb * sub:sb * sub + nwin, lo:lo + HEAD_DIM],
                                        vx_ref[:, lo:lo + HEAD_DIM]], axis=0)
                kpos = t0 + sb * sub - blk + kj
                mask = (kj >= nwin) | (in_band & (kpos >= 0) & (kpos < seq))
            else:
                kall = kx_ref[:, lo:lo + HEAD_DIM]
                vall = vx_ref[:, lo:lo + HEAD_DIM]
                mask = None
            og = _attend(qg, kall, vall, mask, _sink_row(sink_ref, kv, rows))
            for h in range(ATT_GROUP):
                heads[kv * ATT_GROUP + h] = og[:, h * rows:(h + 1) * rows].T.astype(BF16)
        outs.append(jnp.concatenate(heads, axis=1))
    o = outs[0] if len(outs) == 1 else jnp.concatenate(outs, axis=0)
    o_ref[...] = x_ref[...] + mod_ref[0, 2:3, :] * _dot(o, wo_ref[...])


def _attention(qkv, qkv_ctx, x, mod, sink, w_o, *, band, tq, seq, nctx, rows_per_mod):
    n, d = x.shape
    wo_stack, layer = w_o
    nq = ATT_HEADS * HEAD_DIM
    nkv = ATT_KV_HEADS * HEAD_DIM
    kcol, vcol = nq // nkv, nq // nkv + 1
    blk = WINDOW
    nblk = n // blk
    r = tq // blk
    q_spec = pl.BlockSpec((tq, nq), lambda i: (i, 0))
    tail_specs = [
        pl.BlockSpec((tq, d), lambda i: (i, 0)),
        pl.BlockSpec((1, MOD_ROWS, d), lambda i: ((i * tq) // rows_per_mod, 0, 0)),
        pl.BlockSpec((1, ATT_HEADS), lambda i: (0, 0)),
        pl.BlockSpec((None, nq, d), lambda i: (layer, 0, 0)),
    ]
    tail_args = [x, mod, sink.reshape(1, ATT_HEADS), wo_stack]
    if band:
        def ctx_idx(col):
            return lambda i: ((i * tq) // seq, col)
        in_specs = [
            q_spec,
            pl.BlockSpec((blk, nkv), lambda i: (jnp.maximum(i * r - 1, 0), kcol)),
            pl.BlockSpec((tq, nkv), lambda i: (i, kcol)),
            pl.BlockSpec((blk, nkv), lambda i: (jnp.minimum((i + 1) * r, nblk - 1), kcol)),
            pl.BlockSpec((blk, nkv), lambda i: (jnp.maximum(i * r - 1, 0), vcol)),
            pl.BlockSpec((tq, nkv), lambda i: (i, vcol)),
            pl.BlockSpec((blk, nkv), lambda i: (jnp.minimum((i + 1) * r, nblk - 1), vcol)),
            pl.BlockSpec((nctx, nkv), ctx_idx(kcol)),
            pl.BlockSpec((nctx, nkv), ctx_idx(vcol)),
        ] + tail_specs
        args = [qkv] * 7 + [qkv_ctx] * 2 + tail_args
        ncol = 3 * blk + nctx
    else:
        in_specs = [
            q_spec,
            pl.BlockSpec((tq, nkv), lambda i: (i, kcol)),
            pl.BlockSpec((tq, nkv), lambda i: (i, vcol)),
        ] + tail_specs
        args = [qkv] * 3 + tail_args
        ncol = tq
    vmem = (2 * (tq * nq * 2 + 4 * (tq + 2 * blk + nctx) * nkv * 2 + 2 * tq * d * 4 + nq * d * 2)
            + 6 * ATT_GROUP * blk * ncol * 4 + 2 * tq * d * 4)
    return pl.pallas_call(
        functools.partial(_attn_kernel, band=band, tq=tq, seq=seq, sub=min(tq, WINDOW)),
        out_shape=jax.ShapeDtypeStruct((n, d), F32),
        grid=(n // tq,),
        in_specs=in_specs,
        out_specs=pl.BlockSpec((tq, d), lambda i: (i, 0)),
        compiler_params=_params(("arbitrary",), vmem),
        name="attn_band" if band else "attn_ctx",
    )(*args)


def _gla_scan_kernel(qkf_ref, vf_ref, gf_ref, qkb_ref, vb_ref, gb_ref, s0f_ref, s0b_ref,
                     of_ref, ob_ref, sf_ref, sb_ref, *scratch, chunk, dk, dv, q_scale):
    per_set = len(scratch) // 2
    sets = (scratch[:per_set], scratch[per_set:])
    s = pl.program_id(1)

    @pl.when(s == 0)
    def _():
        sf_ref[...] = s0f_ref[...]
        sb_ref[...] = s0b_ref[...]
        for ref in sets[1]:
            ref[...] = jnp.zeros_like(ref) if ref.dtype == BF16 else jnp.ones_like(ref)

    row = lax.broadcasted_iota(jnp.int32, (chunk, chunk), 0)
    col = lax.broadcasted_iota(jnp.int32, (chunk, chunk), 1)
    directions = (
        (qkf_ref, vf_ref, gf_ref, of_ref, sf_ref, col <= row, chunk - 1),
        (qkb_ref, vb_ref, gb_ref, ob_ref, sb_ref, col >= row, 0),
    )

    def body(new, old):
        for d, (_, v_ref, _, o_ref, s_ref, tri, _) in enumerate(directions):
            qa_ref, ka_ref, qs_ref, ks_ref, decay_ref = old[5 * d:5 * d + 5]
            v = v_ref[...]
            for h in range(GLA_HEADS):
                kc = slice(h * dk, (h + 1) * dk)
                vc = slice(h * dv, (h + 1) * dv)
                a = jnp.where(tri, _dot_nt(qa_ref[:, kc], ka_ref[:, kc]), 0.0).astype(BF16)
                st = s_ref[0, h]
                o_ref[:, vc] = (_dot(a, v[:, vc]) + _dot(qs_ref[:, kc], st.astype(BF16))).astype(o_ref.dtype)
                upd = _dot_tn(ks_ref[:, kc], v[:, vc])
                decay = decay_ref[kc, :]
                s_ref[0, h] = jnp.concatenate(
                    [st[:, t * LANES:(t + 1) * LANES] * decay for t in range(dv // LANES)], axis=1) + upd
        for d, (qk_ref, _, g_ref, _, _, tri, last) in enumerate(directions):
            qa_ref, ka_ref, qs_ref, ks_ref, decay_ref = new[5 * d:5 * d + 5]
            tri_b = tri.astype(BF16)
            g = _log_sigmoid(g_ref[...]) / GLA_GATE_NORM
            g_hi = g.astype(BF16)
            g_lo = (g - g_hi.astype(F32)).astype(BF16)
            b = _dot(tri_b, g_hi) + _dot(tri_b, g_lo)
            b_tot = b[last:last + 1]
            b_mid = b[chunk // 2:chunk // 2 + 1]
            hk = b.shape[1]
            q = qk_ref[:, :hk] * q_scale
            k = qk_ref[:, hk:]
            qa_ref[...] = (q * jnp.exp(b - b_mid)).astype(BF16)
            ka_ref[...] = (k * jnp.exp(b_mid - b)).astype(BF16)
            qs_ref[...] = (q * jnp.exp(b)).astype(BF16)
            ks_ref[...] = (k * jnp.exp(b_tot - b)).astype(BF16)
            decay_ref[...] = jnp.exp(jnp.broadcast_to(b_tot, (LANES, b.shape[1]))).T

    @pl.when(s % 2 == 0)
    def _():
        body(sets[0], sets[1])

    @pl.when(s % 2 == 1)
    def _():
        body(sets[1], sets[0])


def _gla_scan(qk, v, g_f, g_b, s0_f, s0_b, *, batch, length, chunk, dk, dv):
    nc = length // chunk
    hk, hv = GLA_HEADS * dk, GLA_HEADS * dv

    def nxt(c):
        return jnp.minimum(c, nc - 1)

    def cur(c):
        return jnp.maximum(c - 1, 0)

    def fwd(b, c):
        return b * nc + c

    def bwd(b, c):
        return b * nc + nc - 1 - c

    state_spec = pl.BlockSpec((1, GLA_HEADS, dk, dv), lambda b, c: (b, 0, 0, 0))
    in_specs = []
    for rows in (fwd, bwd):
        in_specs += [
            pl.BlockSpec((chunk, 2 * hk), lambda b, c, rows=rows: (rows(b, nxt(c)), 0)),
            pl.BlockSpec((chunk, hv), lambda b, c, rows=rows: (rows(b, cur(c)), 0)),
            pl.BlockSpec((chunk, hk), lambda b, c, rows=rows: (rows(b, nxt(c)), 0)),
        ]
    in_specs += [state_spec, state_spec]
    operand_set = [pltpu.VMEM((chunk, hk), BF16)] * 4 + [pltpu.VMEM((hk, LANES), F32)]
    state_bytes = GLA_HEADS * dk * dv * 4
    vmem = (2 * 2 * (3 * chunk * hk * 4 + chunk * hv * 2 + chunk * hv * 4) + 8 * state_bytes
            + 16 * chunk * hk * 4 + 4 * dk * dv * 4 + 4 * (4 * chunk * hk * 2 + hk * LANES * 4))
    return pl.pallas_call(
        functools.partial(_gla_scan_kernel, chunk=chunk, dk=dk, dv=dv, q_scale=dk ** -0.5),
        out_shape=[jax.ShapeDtypeStruct((batch * length, hv), BF16)] * 2
        + [jax.ShapeDtypeStruct(s0_f.shape, F32)] * 2,
        grid=(batch, nc + 1),
        in_specs=in_specs,
        out_specs=[
            pl.BlockSpec((chunk, hv), lambda b, c: (fwd(b, cur(c)), 0)),
            pl.BlockSpec((chunk, hv), lambda b, c: (bwd(b, cur(c)), 0)),
            state_spec,
            state_spec,
        ],
        scratch_shapes=operand_set * 4,
        compiler_params=_params(("arbitrary", "arbitrary"), vmem),
        name="gla_scan",
    )(qk, v, g_f, qk, v, g_b, s0_f, s0_b)


def _gla_out_kernel(of_ref, ob_ref, r_ref, x_ref, mod_ref, g_ref, wo_ref, o_ref, *, dv):
    o = of_ref[...].astype(F32) + ob_ref[...].astype(F32)
    parts = []
    for h in range(o.shape[1] // dv):
        oh = o[:, h * dv:(h + 1) * dv]
        parts.append((oh * lax.rsqrt(jnp.mean(oh * oh, axis=-1, keepdims=True) + EPS)) * g_ref[...])
    gated = (jnp.concatenate(parts, axis=1) * _silu(r_ref[...].astype(F32))).astype(BF16)
    o_ref[...] = x_ref[...] + mod_ref[0, 2:3, :] * _dot(gated, wo_ref[...])


def _gla_output(o_f, o_b, r, x, mod, onorm_g, w_o, *, tm, rows_per_mod, dv):
    n, d = x.shape
    wo_stack, layer = w_o
    dvt = o_f.shape[1]
    vmem = 2 * (5 * tm * dvt * 4 + dvt * d * 2) + 4 * tm * dvt * 4
    return pl.pallas_call(
        functools.partial(_gla_out_kernel, dv=dv),
        out_shape=jax.ShapeDtypeStruct((n, d), F32),
        grid=(n // tm,),
        in_specs=[
            pl.BlockSpec((tm, dvt), lambda i: (i, 0)),
            pl.BlockSpec((tm, dvt), lambda i: (i, 0)),
            pl.BlockSpec((tm, dvt), lambda i: (i, 0)),
            pl.BlockSpec((tm, d), lambda i: (i, 0)),
            pl.BlockSpec((1, MOD_ROWS, d), lambda i: ((i * tm) // rows_per_mod, 0, 0)),
            pl.BlockSpec((1, dv), lambda i: (0, 0)),
            pl.BlockSpec((None, dvt, d), lambda i: (layer, 0, 0)),
        ],
        out_specs=pl.BlockSpec((tm, d), lambda i: (i, 0)),
        compiler_params=_params(("arbitrary",), vmem),
        name="gla_out",
    )(o_f, o_b, r, x, mod, onorm_g.reshape(1, dv), wo_stack)


def _ffn_kernel(xp_ref, x_ref, xn_ref, mod_ref, g_ref, fg_ref, wg_ref, wv_ref, cw_ref, cb_ref,
                wd_ref, o_ref, h_scr, ug_scr, uv_scr, act_scr, *, tm, seq, final_norm, nf):
    i = pl.program_id(0)
    j = pl.program_id(1)
    halo = SUBLANES
    rows = min(tm, seq, 256)
    assert tm <= seq or (rows == seq and tm % seq == 0)

    @pl.when(j == 0)
    def _():
        shift = mod_ref[0, 3:4, :]
        gs = g_ref[...] * (1.0 + mod_ref[0, 4:5, :])

        def norm_mod(x):
            return (x * lax.rsqrt(jnp.mean(x * x, axis=-1, keepdims=True) + EPS)) * gs + shift

        first = (i * tm) % seq == 0
        last = ((i + 1) * tm) % seq == 0
        h_scr[0:halo, :] = jnp.where(first, 0.0, norm_mod(xp_ref[...])).astype(BF16)
        for r0 in range(0, tm, rows):
            h_scr[halo + r0:halo + r0 + rows, :] = norm_mod(x_ref[r0:r0 + rows, :]).astype(BF16)
        h_scr[halo + tm:2 * halo + tm, :] = jnp.where(last, 0.0, norm_mod(xn_ref[...])).astype(BF16)
        o_ref[...] = jnp.zeros_like(o_ref)

    h = h_scr[...]
    ug_scr[...] = _dot(h, wg_ref[...])
    uv_scr[...] = _dot(h, wv_ref[...])

    tf = wg_ref.shape[1]
    gate_cols = pl.ds(pl.multiple_of(j * tf, tf), tf)
    val_cols = pl.ds(pl.multiple_of((nf + j) * tf, tf), tf)
    cwg, cbg = cw_ref[:, gate_cols], cb_ref[:, gate_cols]
    cwv, cbv = cw_ref[:, val_cols], cb_ref[:, val_cols]

    def conv(u_scr, cw, cb, r0):
        prev = u_scr[halo - 1 + r0:halo - 1 + r0 + rows, :]
        nxt = u_scr[halo + 1 + r0:halo + 1 + r0 + rows, :]
        if tm > seq:
            rid = lax.broadcasted_iota(jnp.int32, prev.shape, 0)
            prev = jnp.where(rid == 0, 0.0, prev)
            nxt = jnp.where(rid == rows - 1, 0.0, nxt)
        return cb + cw[0:1, :] * prev + cw[1:2, :] * u_scr[halo + r0:halo + r0 + rows, :] + cw[2:3, :] * nxt

    for r0 in range(0, tm, rows):
        act_scr[r0:r0 + rows, :] = (_silu(conv(ug_scr, cwg, cbg, r0)) * conv(uv_scr, cwv, cbv, r0)).astype(BF16)
    act = act_scr[...]
    for c0 in range(0, o_ref.shape[1], 512):
        o_ref[:, c0:c0 + 512] += _dot(act, wd_ref[:, c0:c0 + 512])

    @pl.when(j == pl.num_programs(1) - 1)
    def _():
        er = min(tm, 128)

        def residual(c, carry):
            sl = pl.ds(pl.multiple_of(c * er, er), er)
            y = x_ref[sl, :] + mod_ref[0, 5:6, :] * o_ref[sl, :]
            if final_norm:
                y = (y * lax.rsqrt(jnp.mean(y * y, axis=-1, keepdims=True) + EPS)) * fg_ref[...]
            o_ref[sl, :] = y
            return carry

        lax.fori_loop(0, tm // er, residual, 0)


def _conv_ffn(x, mod, g, w_up, conv_w, conv_b, w_down, *, tm, tf, seq, rows_per_mod, final_g=None):
    n, d = x.shape
    (up_stack, layer), (down_stack, _) = w_up, w_down
    dff = down_stack.shape[1]
    nf = dff // tf
    halo = SUBLANES
    r = tm // halo
    nhalo = n // halo
    conv_b = conv_b.reshape(1, 2 * dff)
    fg = (g if final_g is None else final_g).reshape(1, d)
    vmem = (2 * (2 * tm * d * 4 + 2 * d * tf * 2 + tf * d * 2) + (tm + 2 * halo) * d * 2
            + 2 * (tm + 2 * halo) * tf * 4 + tm * tf * 2 + 8 * min(tm, 256) * max(tf, d) * 4)
    return pl.pallas_call(
        functools.partial(_ffn_kernel, tm=tm, seq=seq, final_norm=final_g is not None, nf=nf),
        out_shape=jax.ShapeDtypeStruct((n, d), F32),
        grid=(n // tm, nf),
        in_specs=[
            pl.BlockSpec((halo, d), lambda i, j: (jnp.maximum(i * r - 1, 0), 0)),
            pl.BlockSpec((tm, d), lambda i, j: (i, 0)),
            pl.BlockSpec((halo, d), lambda i, j: (jnp.minimum((i + 1) * r, nhalo - 1), 0)),
            pl.BlockSpec((1, MOD_ROWS, d), lambda i, j: ((i * tm) // rows_per_mod, 0, 0)),
            pl.BlockSpec((1, d), lambda i, j: (0, 0)),
            pl.BlockSpec((1, d), lambda i, j: (0, 0)),
            pl.BlockSpec((None, d, tf), lambda i, j: (layer, 0, j)),
            pl.BlockSpec((None, d, tf), lambda i, j: (layer, 0, nf + j)),
            pl.BlockSpec((CONV_W, 2 * dff), lambda i, j: (0, 0)),
            pl.BlockSpec((1, 2 * dff), lambda i, j: (0, 0)),
            pl.BlockSpec((None, tf, d), lambda i, j: (layer, j, 0)),
        ],
        out_specs=pl.BlockSpec((tm, d), lambda i, j: (i, 0)),
        scratch_shapes=[
            pltpu.VMEM((tm + 2 * halo, d), BF16),
            pltpu.VMEM((tm + 2 * halo, tf), F32),
            pltpu.VMEM((tm + 2 * halo, tf), F32),
            pltpu.VMEM((tm, tf), BF16),
        ],
        compiler_params=_params(("arbitrary", "arbitrary"), vmem),
        name="conv_ffn",
    )(x, x, x, mod, g.reshape(1, d), fg, up_stack, up_stack, conv_w, conv_b, down_stack)


def _rope_tables(seq):
    half = HEAD_DIM // 2
    rows = seq // GRID_W
    row = jnp.repeat(jnp.arange(rows, dtype=F32), GRID_W)
    col = jnp.tile(jnp.arange(GRID_W, dtype=F32), rows)
    inv = ROPE_BASE ** (-jnp.arange(0, half, 2, dtype=F32) / half)
    ang_r = row[:, None] * inv[None, :]
    ang_c = col[:, None] * inv[None, :]
    zero = jnp.zeros_like(ang_r)
    cos = jnp.concatenate([jnp.cos(ang_r)] * 2 + [jnp.cos(ang_c)] * 2, axis=1)
    sa = jnp.concatenate([-jnp.sin(ang_r), zero, -jnp.sin(ang_c), zero], axis=1)
    sb = jnp.concatenate([zero, jnp.sin(ang_r), zero, jnp.sin(ang_c)], axis=1)
    return cos, sa, sb


def _tile(n, target):
    t = min(n, target)
    while n % t:
        t //= 2
    return t


def kernel(x, c, ctx, c_ctx, ada_w, ada_b, norm_mix_g, norm_ffn_g, ffn_w_up, ffn_conv_w, ffn_conv_b, ffn_w_down,
           attn_w_qkv, attn_sink, attn_w_o, gla_w_in, gla_gf_w1, gla_gf_w2, gla_gf_b, gla_gb_w1, gla_gb_w2,
           gla_gb_b, gla_onorm_g, gla_w_o, final_norm_g):
    batch, seq, d = x.shape
    nctx = ctx.shape[1]
    depth = ada_w.shape[0]
    n_lat, n_ctx = batch * seq, batch * nctx
    dk = gla_gf_w2.shape[2] // GLA_HEADS
    dv = gla_onorm_g.shape[1]
    nq = ATT_HEADS * HEAD_DIM
    assert seq % WINDOW == 0 and nctx % WINDOW == 0 and d % LANES == 0

    n_cond = batch + 1
    pad = (-n_cond) % SUBLANES
    cc = jnp.concatenate([c, c_ctx[None, :], jnp.zeros((pad, d), F32)], axis=0)
    mods = _modulations(cc, ada_w, ada_b)
    mods = mods.reshape(depth, n_cond + pad, N_MOD, d)
    mods = jnp.pad(mods, ((0, 0), (0, 0), (0, MOD_ROWS - N_MOD), (0, 0)))

    rope = _rope_tables(seq)
    xl = x.reshape(n_lat, d)
    xc = ctx.reshape(n_ctx, d)

    tm_l, tm_c = _tile(seq, 512), _tile(nctx, 512)
    tm_pc = _tile(n_ctx, 512)
    tm_fc = _tile(n_ctx, 1024) if nctx <= 256 and _tile(n_ctx, 1024) % nctx == 0 else tm_c
    tq_l = _tile(seq, 512)
    gla_chunk = _tile(nctx, 128)
    tf = 512
    nkv = ATT_KV_HEADS * HEAD_DIM
    hk, hv = GLA_HEADS * dk, GLA_HEADS * dv

    w_up_all, w_down_all = ffn_w_up.astype(BF16), ffn_w_down.astype(BF16)
    w_qkv_all, attn_wo_all = attn_w_qkv.astype(BF16), attn_w_o.astype(BF16)
    gla_qk_all, gla_vr_all = gla_w_in[:, :, :2 * hk].astype(BF16), gla_w_in[:, :, 2 * hk:].astype(BF16)
    gla_wo_all = gla_w_o.astype(BF16)

    for i in range(depth):
        last = i == depth - 1
        mod_l = mods[i, :batch]
        mod_c = mods[i, batch:batch + 1]
        j = i // N_MIXERS
        if i % N_MIXERS == 0:
            w_qkv, w_o = (w_qkv_all, j), (attn_wo_all, j)
            seg = [(nq + 2 * nkv, BF16)]
            qkv_l, = _project(xl, mod_l, norm_mix_g[i], w_qkv, seg, rows_per_mod=seq, tm=tm_l,
                              rope=rope, seq=seq, rope_cols=nq + nkv)
            qkv_c, = _project(xc, mod_c, norm_mix_g[i], w_qkv, seg, rows_per_mod=n_ctx, tm=tm_pc)
            xl = _attention(qkv_l, qkv_c, xl, mod_l, attn_sink[j], w_o, band=True, tq=tq_l, seq=seq, nctx=nctx,
                            rows_per_mod=seq)
            if not last:
                xc = _attention(qkv_c, None, xc, mod_c, attn_sink[j], w_o, band=False, tq=nctx, seq=nctx, nctx=nctx,
                                rows_per_mod=n_ctx)
        else:
            w_qk, w_vr, w_o = (gla_qk_all, j), (gla_vr_all, j), (gla_wo_all, j)
            rank = gla_gf_w1.shape[2]
            w1 = jnp.zeros((d, LANES), F32).at[:, :rank].set(gla_gf_w1[j]).at[:, rank:2 * rank].set(gla_gb_w1[j])
            w2f = jnp.zeros((LANES, hk), F32).at[:rank].set(gla_gf_w2[j])
            w2b = jnp.zeros((LANES, hk), F32).at[rank:2 * rank].set(gla_gb_w2[j])
            gate = (w1.astype(BF16), w2f.astype(BF16), w2b.astype(BF16), gla_gf_b[j], gla_gb_b[j])
            streams = []
            for xs, mod, rpm, tm in ((xc, mod_c, n_ctx, tm_pc), (xl, mod_l, seq, tm_l)):
                qk, gf, gb = _project(xs, mod, norm_mix_g[i], w_qk, [(2 * hk, F32)], rows_per_mod=rpm, tm=tm,
                                      gate=gate)
                v, r = _project(xs, mod, norm_mix_g[i], w_vr, [(hv, BF16), (hv, BF16)], rows_per_mod=rpm, tm=tm)
                streams.append((qk, v, r, gf, gb))
            s0 = jnp.zeros((batch, GLA_HEADS, dk, dv), F32)
            scan = functools.partial(_gla_scan, batch=batch, chunk=gla_chunk, dk=dk, dv=dv)
            (qk_c, v_c, r_c, gf_c, gb_c), (qk_l, v_l, r_l, gf_l, gb_l) = streams
            of_c, ob_c, sf, sb = scan(qk_c, v_c, gf_c, gb_c, s0, s0, length=nctx)
            of_l, ob_l, _, _ = scan(qk_l, v_l, gf_l, gb_l, sf, sb, length=seq)
            xl = _gla_output(of_l, ob_l, r_l, xl, mod_l, gla_onorm_g[j], w_o, tm=tm_l, rows_per_mod=seq, dv=dv)
            if not last:
                xc = _gla_output(of_c, ob_c, r_c, xc, mod_c, gla_onorm_g[j], w_o, tm=tm_c, rows_per_mod=n_ctx, dv=dv)
        w_up, w_down = (w_up_all, i), (w_down_all, i)
        xl = _conv_ffn(xl, mod_l, norm_ffn_g[i], w_up, ffn_conv_w[i], ffn_conv_b[i], w_down,
                       tm=_tile(seq, 1024), tf=tf, seq=seq, rows_per_mod=seq, final_g=final_norm_g if last else None)
        if not last:
            xc = _conv_ffn(xc, mod_c, norm_ffn_g[i], w_up, ffn_conv_w[i], ffn_conv_b[i], w_down,
                           tm=tm_fc, tf=tf, seq=nctx, rows_per_mod=n_ctx)
    return xl.reshape(batch, seq, d)
```

```python
import functools

import jax
import jax.numpy as jnp
from jax import lax
from jax.experimental import pallas as pl
from jax.experimental.pallas import tpu as pltpu

GRID_W = 64
N_MIXERS = 2
ATT_HEADS = 16
ATT_KV_HEADS = 4
ATT_GROUP = ATT_HEADS // ATT_KV_HEADS
HEAD_DIM = 128
WINDOW = 128
ROPE_BASE = 10000.0
GLA_HEADS = 4
GLA_GATE_RANK = 16
GLA_GATE_NORM = 16.0
CONV_W = 3
EPS = 1e-6
N_MOD = 6
MOD_ROWS = 8

LANES = 128
SUBLANES = 8
VMEM_LIMIT_CAP = 60 * 1024 * 1024

BF16 = jnp.bfloat16
F32 = jnp.float32


def _dot(a, b):
    return jnp.dot(a, b, preferred_element_type=F32)


def _dot_nt(a, b):
    return lax.dot_general(a, b, (((1,), (1,)), ((), ())), preferred_element_type=F32)


def _dot_tn(a, b):
    return lax.dot_general(a, b, (((0,), (0,)), ((), ())), preferred_element_type=F32)


def _params(semantics, vmem_bytes):
    limit = min(VMEM_LIMIT_CAP, int(vmem_bytes * 1.25) + (4 << 20))
    return pltpu.CompilerParams(dimension_semantics=semantics, vmem_limit_bytes=limit)


def _sigmoid(x):
    return 1.0 / (1.0 + jnp.exp(-x))


def _silu(x):
    return x * _sigmoid(x)


def _norm_mod(x, g, shift, scale):
    y = x * lax.rsqrt(jnp.mean(x * x, axis=-1, keepdims=True) + EPS)
    return (y * g) * (1.0 + scale) + shift


def _mod_kernel(c_ref, w_ref, b_ref, o_ref):
    s = _silu(c_ref[...]).astype(BF16)
    o_ref[0] = _dot(s, w_ref[0].astype(BF16)) + b_ref[0]


def _modulations(cc, ada_w, ada_b):
    depth, d, n = ada_w.shape
    r = cc.shape[0]
    tn = 1024
    vmem = 2 * (d * tn * 4 + r * tn * 4 + tn * 4) + r * d * 4 + d * tn * 2
    return pl.pallas_call(
        _mod_kernel,
        out_shape=jax.ShapeDtypeStruct((depth, r, n), F32),
        grid=(depth, n // tn),
        in_specs=[
            pl.BlockSpec((r, d), lambda l, j: (0, 0)),
            pl.BlockSpec((1, d, tn), lambda l, j: (l, 0, j)),
            pl.BlockSpec((1, 1, tn), lambda l, j: (l, 0, j)),
        ],
        out_specs=pl.BlockSpec((1, r, tn), lambda l, j: (l, 0, j)),
        compiler_params=_params(("arbitrary", "arbitrary"), vmem),
        name="adaln_mod",
    )(cc, ada_w, ada_b.reshape(depth, 1, n))


def _log_sigmoid(z):
    return jnp.minimum(z, 0.0) - jnp.log(1.0 + jnp.exp(-jnp.abs(z)))


def _proj_kernel(*refs, segments, rope_cols, gated, col_chunk):
    n_out = len(segments) + (2 if gated else 0)
    x_ref, mod_ref, g_ref, w_ref = refs[:4]
    extra = refs[4:len(refs) - n_out - 2]
    outs = refs[len(refs) - n_out - 2:len(refs) - 2]
    h_even, h_odd = refs[-2:]
    s = pl.program_id(0)

    @pl.when(s == 0)
    def _():
        h_odd[...] = jnp.zeros_like(h_odd)

    def body(h_new, h_old):
        h = h_old[...]
        if gated:
            w1_ref, w2f_ref, w2b_ref, bf_ref, bb_ref = extra
            t = _dot(h, w1_ref[...]).astype(BF16)
            for w2_ref, b_ref, o_ref in ((w2f_ref, bf_ref, outs[-2]), (w2b_ref, bb_ref, outs[-1])):
                o_ref[...] = _dot(t, w2_ref[...]) + b_ref[...]
        if rope_cols:
            cos, sa, sb = extra[0][...], extra[1][...], extra[2][...]
        col = 0
        for (width, _), y_ref in zip(segments, outs):
            for c0 in range(0, width, col_chunk):
                y = _dot(h, w_ref[:, col + c0:col + c0 + col_chunk])
                if col + c0 < rope_cols:
                    parts = []
                    for p in range(col_chunk // HEAD_DIM):
                        yh = y[:, p * HEAD_DIM:(p + 1) * HEAD_DIM]
                        if col + c0 + p * HEAD_DIM < rope_cols:
                            up = pltpu.roll(yh, HEAD_DIM - HEAD_DIM // 4, axis=1)
                            dn = pltpu.roll(yh, HEAD_DIM // 4, axis=1)
                            yh = yh * cos + up * sa + dn * sb
                        parts.append(yh)
                    y = jnp.concatenate(parts, axis=1)
                y_ref[:, c0:c0 + col_chunk] = y.astype(y_ref.dtype)
            col += width
        x = x_ref[...]
        gs = g_ref[...] * (1.0 + mod_ref[0, 1:2, :])
        r = lax.rsqrt(jnp.mean(x * x, axis=-1, keepdims=True) + EPS)
        h_new[...] = ((x * r) * gs + mod_ref[0, 0:1, :]).astype(BF16)

    @pl.when(s % 2 == 0)
    def _():
        body(h_even, h_odd)

    @pl.when(s % 2 == 1)
    def _():
        body(h_odd, h_even)


def _project(x, mod, g, w, segments, *, rows_per_mod, tm, rope=None, seq=None, rope_cols=0, gate=None):
    n, d = x.shape
    w_stack, layer = w
    m = w_stack.shape[2]
    nrow = n // tm
    col_chunk = 1024 if all(wd % 1024 == 0 for wd, _ in segments) else 512

    def cur(s):
        return jnp.minimum(s, nrow - 1)

    def prev(s):
        return jnp.maximum(s - 1, 0)

    in_specs = [
        pl.BlockSpec((tm, d), lambda s: (cur(s), 0)),
        pl.BlockSpec((1, MOD_ROWS, d), lambda s: ((cur(s) * tm) // rows_per_mod, 0, 0)),
        pl.BlockSpec((1, d), lambda s: (0, 0)),
        pl.BlockSpec((None, d, m), lambda s: (layer, 0, 0), pipeline_mode=pl.Buffered(1)),
    ]
    args = [x, mod, g.reshape(1, d), w_stack]
    out_shape = [jax.ShapeDtypeStruct((n, wd), dt) for wd, dt in segments]
    out_specs = [pl.BlockSpec((tm, wd), lambda s: (prev(s), 0)) for wd, _ in segments]
    vmem = (2 * tm * d * 4 + d * m * 2 + 2 * sum(tm * wd * jnp.dtype(dt).itemsize for wd, dt in segments)
            + 2 * tm * d * 2 + 3 * tm * col_chunk * 4 + 2 * tm * d * 4)
    if rope is not None:
        tiles_per_seq = seq // tm
        for tab in rope:
            in_specs.append(pl.BlockSpec((tm, HEAD_DIM), lambda s: (prev(s) % tiles_per_seq, 0)))
            args.append(tab)
        vmem += 6 * tm * HEAD_DIM * 4
    if gate is not None:
        w1, w2f, w2b, bf, bb = gate
        dk = w2f.shape[1]
        in_specs += [
            pl.BlockSpec(w1.shape, lambda s: (0, 0)),
            pl.BlockSpec(w2f.shape, lambda s: (0, 0)),
            pl.BlockSpec(w2b.shape, lambda s: (0, 0)),
            pl.BlockSpec((1, dk), lambda s: (0, 0)),
            pl.BlockSpec((1, dk), lambda s: (0, 0)),
        ]
        args += [w1, w2f, w2b, bf.reshape(1, dk), bb.reshape(1, dk)]
        out_shape += [jax.ShapeDtypeStruct((n, dk), F32)] * 2
        out_specs += [pl.BlockSpec((tm, dk), lambda s: (prev(s), 0))] * 2
        vmem += 6 * tm * dk * 4 + 2 * (d * LANES * 2 + 2 * LANES * dk * 2)
    return pl.pallas_call(
        functools.partial(_proj_kernel, segments=tuple(segments), rope_cols=rope_cols, gated=gate is not None,
                          col_chunk=col_chunk),
        out_shape=out_shape,
        grid=(nrow + 1,),
        in_specs=in_specs,
        out_specs=out_specs,
        scratch_shapes=[pltpu.VMEM((tm, d), BF16), pltpu.VMEM((tm, d), BF16)],
        compiler_params=_params(("arbitrary",), vmem),
        name="proj",
    )(*args)


def _attend(qg, kall, vall, mask, sink_row):
    s = _dot_nt(kall, qg) * (HEAD_DIM ** -0.5)
    if mask is not None:
        s = jnp.where(mask, s, -jnp.inf)
    mx = jnp.maximum(jnp.max(s, axis=0, keepdims=True), sink_row)
    e = jnp.exp(s - mx)
    denom = jnp.sum(e, axis=0, keepdims=True) + jnp.exp(sink_row - mx)
    return _dot_tn(vall, e.astype(BF16)) / denom


def _sink_row(sink_ref, kv, rows):
    cols = [jnp.broadcast_to(sink_ref[0:1, kv * ATT_GROUP + h:kv * ATT_GROUP + h + 1], (1, rows))
            for h in range(ATT_GROUP)]
    return jnp.concatenate(cols, axis=1)


def _attn_kernel(*refs, band, tq, seq, sub):
    if band:
        (q_ref, kp_ref, kc_ref, kn_ref, vp_ref, vc_ref, vn_ref, kx_ref, vx_ref,
         x_ref, mod_ref, sink_ref, wo_ref, o_ref) = refs
    else:
        q_ref, kx_ref, vx_ref, x_ref, mod_ref, sink_ref, wo_ref, o_ref = refs
    blk = WINDOW
    nsub = tq // sub if band else 1
    rows = sub if band else tq
    if band:
        kwin = jnp.concatenate([kp_ref[...], kc_ref[...], kn_ref[...]], axis=0)
        vwin = jnp.concatenate([vp_ref[...], vc_ref[...], vn_ref[...]], axis=0)
        t0 = (pl.program_id(0) * tq) % seq
        nctx = kx_ref.shape[0]
        nwin = sub + 2 * blk
        ncol = nwin + nctx
        kj = lax.broadcasted_iota(jnp.int32, (ncol, ATT_GROUP * sub), 0)
        qi = lax.broadcasted_iota(jnp.int32, (ncol, ATT_GROUP * sub), 1) & (sub - 1)
        in_band = jnp.abs(qi - kj + blk) <= WINDOW
    outs = []
    for sb in range(nsub):
        heads = [None] * ATT_HEADS
        for kv in range(ATT_KV_HEADS):
            lo = kv * HEAD_DIM
            qg = jnp.concatenate(
                [q_ref[sb * rows:(sb + 1) * rows, (kv * ATT_GROUP + h) * HEAD_DIM:(kv * ATT_GROUP + h + 1) * HEAD_DIM]
                 for h in range(ATT_GROUP)], axis=0)
            if band:
                kall = jnp.concatenate([kwin[sb * sub:sb * sub + nwin, lo:lo + HEAD_DIM],
                                        kx_ref[:, lo:lo + HEAD_DIM]], axis=0)
                vall = jnp.concatenate([vwin[sb * sub:sb * sub + nwin, lo:lo + HEAD_DIM],
                                        vx_ref[:, lo:lo + HEAD_DIM]], axis=0)
                kpos = t0 + sb * sub - blk + kj
                mask = (kj >= nwin) | (in_band & (kpos >= 0) & (kpos < seq))
            else:
                kall = kx_ref[:, lo:lo + HEAD_DIM]
                vall = vx_ref[:, lo:lo + HEAD_DIM]
                mask = None
            og = _attend(qg, kall, vall, mask, _sink_row(sink_ref, kv, rows))
            for h in range(ATT_GROUP):
                heads[kv * ATT_GROUP + h] = og[:, h * rows:(h + 1) * rows].T.astype(BF16)
        outs.append(jnp.concatenate(heads, axis=1))
    o = outs[0] if len(outs) == 1 else jnp.concatenate(outs, axis=0)
    o_ref[...] = x_ref[...] + mod_ref[0, 2:3, :] * _dot(o, wo_ref[...])


def _attention(qkv, qkv_ctx, x, mod, sink, w_o, *, band, tq, seq, nctx, rows_per_mod):
    n, d = x.shape
    wo_stack, layer = w_o
    nq = ATT_HEADS * HEAD_DIM
    nkv = ATT_KV_HEADS * HEAD_DIM
    kcol, vcol = nq // nkv, nq // nkv + 1
    blk = WINDOW
    nblk = n // blk
    r = tq // blk
    q_spec = pl.BlockSpec((tq, nq), lambda i: (i, 0))
    tail_specs = [
        pl.BlockSpec((tq, d), lambda i: (i, 0)),
        pl.BlockSpec((1, MOD_ROWS, d), lambda i: ((i * tq) // rows_per_mod, 0, 0)),
        pl.BlockSpec((1, ATT_HEADS), lambda i: (0, 0)),
        pl.BlockSpec((None, nq, d), lambda i: (layer, 0, 0)),
    ]
    tail_args = [x, mod, sink.reshape(1, ATT_HEADS), wo_stack]
    if band:
        def ctx_idx(col):
            return lambda i: ((i * tq) // seq, col)
        in_specs = [
            q_spec,
            pl.BlockSpec((blk, nkv), lambda i: (jnp.maximum(i * r - 1, 0), kcol)),
            pl.BlockSpec((tq, nkv), lambda i: (i, kcol)),
            pl.BlockSpec((blk, nkv), lambda i: (jnp.minimum((i + 1) * r, nblk - 1), kcol)),
            pl.BlockSpec((blk, nkv), lambda i: (jnp.maximum(i * r - 1, 0), vcol)),
            pl.BlockSpec((tq, nkv), lambda i: (i, vcol)),
            pl.BlockSpec((blk, nkv), lambda i: (jnp.minimum((i + 1) * r, nblk - 1), vcol)),
            pl.BlockSpec((nctx, nkv), ctx_idx(kcol)),
            pl.BlockSpec((nctx, nkv), ctx_idx(vcol)),
        ] + tail_specs
        args = [qkv] * 7 + [qkv_ctx] * 2 + tail_args
        ncol = 3 * blk + nctx
    else:
        in_specs = [
            q_spec,
            pl.BlockSpec((tq, nkv), lambda i: (i, kcol)),
            pl.BlockSpec((tq, nkv), lambda i: (i, vcol)),
        ] + tail_specs
        args = [qkv] * 3 + tail_args
        ncol = tq
    vmem = (2 * (tq * nq * 2 + 4 * (tq + 2 * blk + nctx) * nkv * 2 + 2 * tq * d * 4 + nq * d * 2)
            + 6 * ATT_GROUP * blk * ncol * 4 + 2 * tq * d * 4)
    return pl.pallas_call(
        functools.partial(_attn_kernel, band=band, tq=tq, seq=seq, sub=min(tq, WINDOW)),
        out_shape=jax.ShapeDtypeStruct((n, d), F32),
        grid=(n // tq,),
        in_specs=in_specs,
        out_specs=pl.BlockSpec((tq, d), lambda i: (i, 0)),
        compiler_params=_params(("arbitrary",), vmem),
        name="attn_band" if band else "attn_ctx",
    )(*args)


def _gla_scan_kernel(qkf_ref, vf_ref, gf_ref, qkb_ref, vb_ref, gb_ref, s0f_ref, s0b_ref,
                     of_ref, ob_ref, sf_ref, sb_ref, *, chunk, dk, dv, q_scale):
    c = pl.program_id(1)

    @pl.when(c == 0)
    def _():
        sf_ref[...] = s0f_ref[...]
        sb_ref[...] = s0b_ref[...]

    row = lax.broadcasted_iota(jnp.int32, (chunk, chunk), 0)
    col = lax.broadcasted_iota(jnp.int32, (chunk, chunk), 1)
    directions = (
        (qkf_ref, vf_ref, gf_ref, of_ref, sf_ref, col <= row, chunk - 1),
        (qkb_ref, vb_ref, gb_ref, ob_ref, sb_ref, col >= row, 0),
    )
    for qk_ref, v_ref, g_ref, o_ref, s_ref, tri, last in directions:
        tri_b = tri.astype(BF16)
        g = _log_sigmoid(g_ref[...]) / GLA_GATE_NORM
        g_hi = g.astype(BF16)
        g_lo = (g - g_hi.astype(F32)).astype(BF16)
        b = _dot(tri_b, g_hi) + _dot(tri_b, g_lo)
        b_tot = b[last:last + 1]
        b_mid = b[chunk // 2:chunk // 2 + 1]

        hk = b.shape[1]
        q = qk_ref[:, :hk] * q_scale
        k = qk_ref[:, hk:]
        v = v_ref[...]
        qa = (q * jnp.exp(b - b_mid)).astype(BF16)
        ka = (k * jnp.exp(b_mid - b)).astype(BF16)
        qs = (q * jnp.exp(b)).astype(BF16)
        ks = (k * jnp.exp(b_tot - b)).astype(BF16)
        decay = jnp.exp(jnp.broadcast_to(b_tot, (LANES, hk))).T

        for h in range(GLA_HEADS):
            kc = slice(h * dk, (h + 1) * dk)
            vc = slice(h * dv, (h + 1) * dv)
            a = jnp.where(tri, _dot_nt(qa[:, kc], ka[:, kc]), 0.0).astype(BF16)
            st = s_ref[0, h]
            o_ref[:, vc] = (_dot(a, v[:, vc]) + _dot(qs[:, kc], st.astype(BF16))).astype(o_ref.dtype)
            upd = _dot_tn(ks[:, kc], v[:, vc])
            s_ref[0, h] = jnp.concatenate(
                [st[:, t * LANES:(t + 1) * LANES] * decay[kc] for t in range(dv // LANES)], axis=1) + upd


def _gla_scan(qk, v, g_f, g_b, s0_f, s0_b, *, batch, length, chunk, dk, dv):
    nc = length // chunk
    hk, hv = GLA_HEADS * dk, GLA_HEADS * dv

    def fwd(b, c):
        return b * nc + c

    def bwd(b, c):
        return b * nc + nc - 1 - c

    state_spec = pl.BlockSpec((1, GLA_HEADS, dk, dv), lambda b, c: (b, 0, 0, 0))
    in_specs = []
    for rows in (fwd, bwd):
        in_specs += [
            pl.BlockSpec((chunk, 2 * hk), lambda b, c, rows=rows: (rows(b, c), 0)),
            pl.BlockSpec((chunk, hv), lambda b, c, rows=rows: (rows(b, c), 0)),
            pl.BlockSpec((chunk, hk), lambda b, c, rows=rows: (rows(b, c), 0)),
        ]
    in_specs += [state_spec, state_spec]
    state_bytes = GLA_HEADS * dk * dv * 4
    vmem = (2 * 2 * (3 * chunk * hk * 4 + 2 * chunk * hv * 2) + 8 * state_bytes
            + 16 * chunk * hk * 4 + 4 * dk * dv * 4)
    return pl.pallas_call(
        functools.partial(_gla_scan_kernel, chunk=chunk, dk=dk, dv=dv, q_scale=dk ** -0.5),
        out_shape=[jax.ShapeDtypeStruct((batch * length, hv), BF16)] * 2
        + [jax.ShapeDtypeStruct(s0_f.shape, F32)] * 2,
        grid=(batch, nc),
        in_specs=in_specs,
        out_specs=[
            pl.BlockSpec((chunk, hv), lambda b, c: (fwd(b, c), 0)),
            pl.BlockSpec((chunk, hv), lambda b, c: (bwd(b, c), 0)),
            state_spec,
            state_spec,
        ],
        compiler_params=_params(("arbitrary", "arbitrary"), vmem),
        name="gla_scan",
    )(qk, v, g_f, qk, v, g_b, s0_f, s0_b)


def _gla_out_kernel(of_ref, ob_ref, r_ref, x_ref, mod_ref, g_ref, wo_ref, o_ref, *, dv):
    o = of_ref[...].astype(F32) + ob_ref[...].astype(F32)
    parts = []
    for h in range(o.shape[1] // dv):
        oh = o[:, h * dv:(h + 1) * dv]
        parts.append((oh * lax.rsqrt(jnp.mean(oh * oh, axis=-1, keepdims=True) + EPS)) * g_ref[...])
    gated = (jnp.concatenate(parts, axis=1) * _silu(r_ref[...].astype(F32))).astype(BF16)
    o_ref[...] = x_ref[...] + mod_ref[0, 2:3, :] * _dot(gated, wo_ref[...])


def _gla_output(o_f, o_b, r, x, mod, onorm_g, w_o, *, tm, rows_per_mod, dv):
    n, d = x.shape
    wo_stack, layer = w_o
    dvt = o_f.shape[1]
    vmem = 2 * (5 * tm * dvt * 4 + dvt * d * 2) + 4 * tm * dvt * 4
    return pl.pallas_call(
        functools.partial(_gla_out_kernel, dv=dv),
        out_shape=jax.ShapeDtypeStruct((n, d), F32),
        grid=(n // tm,),
        in_specs=[
            pl.BlockSpec((tm, dvt), lambda i: (i, 0)),
            pl.BlockSpec((tm, dvt), lambda i: (i, 0)),
            pl.BlockSpec((tm, dvt), lambda i: (i, 0)),
            pl.BlockSpec((tm, d), lambda i: (i, 0)),
            pl.BlockSpec((1, MOD_ROWS, d), lambda i: ((i * tm) // rows_per_mod, 0, 0)),
            pl.BlockSpec((1, dv), lambda i: (0, 0)),
            pl.BlockSpec((None, dvt, d), lambda i: (layer, 0, 0)),
        ],
        out_specs=pl.BlockSpec((tm, d), lambda i: (i, 0)),
        compiler_params=_params(("arbitrary",), vmem),
        name="gla_out",
    )(o_f, o_b, r, x, mod, onorm_g.reshape(1, dv), wo_stack)


def _ffn_kernel(xp_ref, x_ref, xn_ref, mod_ref, g_ref, fg_ref, wg_ref, wv_ref, cw_ref, cb_ref,
                wd_ref, o_ref, h_scr, ug_scr, uv_scr, act_scr, *, tm, seq, final_norm, nf):
    i = pl.program_id(0)
    j = pl.program_id(1)
    halo = SUBLANES
    rows = min(tm, seq, 256)
    assert tm <= seq or (rows == seq and tm % seq == 0)

    @pl.when(j == 0)
    def _():
        shift = mod_ref[0, 3:4, :]
        gs = g_ref[...] * (1.0 + mod_ref[0, 4:5, :])

        def norm_mod(x):
            return (x * lax.rsqrt(jnp.mean(x * x, axis=-1, keepdims=True) + EPS)) * gs + shift

        first = (i * tm) % seq == 0
        last = ((i + 1) * tm) % seq == 0
        h_scr[0:halo, :] = jnp.where(first, 0.0, norm_mod(xp_ref[...])).astype(BF16)
        for r0 in range(0, tm, rows):
            h_scr[halo + r0:halo + r0 + rows, :] = norm_mod(x_ref[r0:r0 + rows, :]).astype(BF16)
        h_scr[halo + tm:2 * halo + tm, :] = jnp.where(last, 0.0, norm_mod(xn_ref[...])).astype(BF16)
        o_ref[...] = jnp.zeros_like(o_ref)

    h = h_scr[...]
    ug_scr[...] = _dot(h, wg_ref[...])
    uv_scr[...] = _dot(h, wv_ref[...])

    tf = wg_ref.shape[1]
    gate_cols = pl.ds(pl.multiple_of(j * tf, tf), tf)
    val_cols = pl.ds(pl.multiple_of((nf + j) * tf, tf), tf)
    cwg, cbg = cw_ref[:, gate_cols], cb_ref[:, gate_cols]
    cwv, cbv = cw_ref[:, val_cols], cb_ref[:, val_cols]

    def conv(u_scr, cw, cb, r0):
        prev = u_scr[halo - 1 + r0:halo - 1 + r0 + rows, :]
        nxt = u_scr[halo + 1 + r0:halo + 1 + r0 + rows, :]
        if tm > seq:
            rid = lax.broadcasted_iota(jnp.int32, prev.shape, 0)
            prev = jnp.where(rid == 0, 0.0, prev)
            nxt = jnp.where(rid == rows - 1, 0.0, nxt)
        return cb + cw[0:1, :] * prev + cw[1:2, :] * u_scr[halo + r0:halo + r0 + rows, :] + cw[2:3, :] * nxt

    for r0 in range(0, tm, rows):
        act_scr[r0:r0 + rows, :] = (_silu(conv(ug_scr, cwg, cbg, r0)) * conv(uv_scr, cwv, cbv, r0)).astype(BF16)
    act = act_scr[...]
    for c0 in range(0, o_ref.shape[1], 512):
        o_ref[:, c0:c0 + 512] += _dot(act, wd_ref[:, c0:c0 + 512])

    @pl.when(j == pl.num_programs(1) - 1)
    def _():
        er = min(tm, 128)

        def residual(c, carry):
            sl = pl.ds(pl.multiple_of(c * er, er), er)
            y = x_ref[sl, :] + mod_ref[0, 5:6, :] * o_ref[sl, :]
            if final_norm:
                y = (y * lax.rsqrt(jnp.mean(y * y, axis=-1, keepdims=True) + EPS)) * fg_ref[...]
            o_ref[sl, :] = y
            return carry

        lax.fori_loop(0, tm // er, residual, 0)


def _conv_ffn(x, mod, g, w_up, conv_w, conv_b, w_down, *, tm, tf, seq, rows_per_mod, final_g=None):
    n, d = x.shape
    (up_stack, layer), (down_stack, _) = w_up, w_down
    dff = down_stack.shape[1]
    nf = dff // tf
    halo = SUBLANES
    r = tm // halo
    nhalo = n // halo
    conv_b = conv_b.reshape(1, 2 * dff)
    fg = (g if final_g is None else final_g).reshape(1, d)
    vmem = (2 * (2 * tm * d * 4 + 2 * d * tf * 2 + tf * d * 2) + (tm + 2 * halo) * d * 2
            + 2 * (tm + 2 * halo) * tf * 4 + tm * tf * 2 + 8 * min(tm, 256) * max(tf, d) * 4)
    return pl.pallas_call(
        functools.partial(_ffn_kernel, tm=tm, seq=seq, final_norm=final_g is not None, nf=nf),
        out_shape=jax.ShapeDtypeStruct((n, d), F32),
        grid=(n // tm, nf),
        in_specs=[
            pl.BlockSpec((halo, d), lambda i, j: (jnp.maximum(i * r - 1, 0), 0)),
            pl.BlockSpec((tm, d), lambda i, j: (i, 0)),
            pl.BlockSpec((halo, d), lambda i, j: (jnp.minimum((i + 1) * r, nhalo - 1), 0)),
            pl.BlockSpec((1, MOD_ROWS, d), lambda i, j: ((i * tm) // rows_per_mod, 0, 0)),
            pl.BlockSpec((1, d), lambda i, j: (0, 0)),
            pl.BlockSpec((1, d), lambda i, j: (0, 0)),
            pl.BlockSpec((None, d, tf), lambda i, j: (layer, 0, j)),
            pl.BlockSpec((None, d, tf), lambda i, j: (layer, 0, nf + j)),
            pl.BlockSpec((CONV_W, 2 * dff), lambda i, j: (0, 0)),
            pl.BlockSpec((1, 2 * dff), lambda i, j: (0, 0)),
            pl.BlockSpec((None, tf, d), lambda i, j: (layer, j, 0)),
        ],
        out_specs=pl.BlockSpec((tm, d), lambda i, j: (i, 0)),
        scratch_shapes=[
            pltpu.VMEM((tm + 2 * halo, d), BF16),
            pltpu.VMEM((tm + 2 * halo, tf), F32),
            pltpu.VMEM((tm + 2 * halo, tf), F32),
            pltpu.VMEM((tm, tf), BF16),
        ],
        compiler_params=_params(("arbitrary", "arbitrary"), vmem),
        name="conv_ffn",
    )(x, x, x, mod, g.reshape(1, d), fg, up_stack, up_stack, conv_w, conv_b, down_stack)


def _rope_tables(seq):
    half = HEAD_DIM // 2
    rows = seq // GRID_W
    row = jnp.repeat(jnp.arange(rows, dtype=F32), GRID_W)
    col = jnp.tile(jnp.arange(GRID_W, dtype=F32), rows)
    inv = ROPE_BASE ** (-jnp.arange(0, half, 2, dtype=F32) / half)
    ang_r = row[:, None] * inv[None, :]
    ang_c = col[:, None] * inv[None, :]
    zero = jnp.zeros_like(ang_r)
    cos = jnp.concatenate([jnp.cos(ang_r)] * 2 + [jnp.cos(ang_c)] * 2, axis=1)
    sa = jnp.concatenate([-jnp.sin(ang_r), zero, -jnp.sin(ang_c), zero], axis=1)
    sb = jnp.concatenate([zero, jnp.sin(ang_r), zero, jnp.sin(ang_c)], axis=1)
    return cos, sa, sb


def _tile(n, target):
    t = min(n, target)
    while n % t:
        t //= 2
    return t


def kernel(x, c, ctx, c_ctx, ada_w, ada_b, norm_mix_g, norm_ffn_g, ffn_w_up, ffn_conv_w, ffn_conv_b, ffn_w_down,
           attn_w_qkv, attn_sink, attn_w_o, gla_w_in, gla_gf_w1, gla_gf_w2, gla_gf_b, gla_gb_w1, gla_gb_w2,
           gla_gb_b, gla_onorm_g, gla_w_o, final_norm_g):
    batch, seq, d = x.shape
    nctx = ctx.shape[1]
    depth = ada_w.shape[0]
    n_lat, n_ctx = batch * seq, batch * nctx
    dk = gla_gf_w2.shape[2] // GLA_HEADS
    dv = gla_onorm_g.shape[1]
    nq = ATT_HEADS * HEAD_DIM
    assert seq % WINDOW == 0 and nctx % WINDOW == 0 and d % LANES == 0

    n_cond = batch + 1
    pad = (-n_cond) % SUBLANES
    cc = jnp.concatenate([c, c_ctx[None, :], jnp.zeros((pad, d), F32)], axis=0)
    mods = _modulations(cc, ada_w, ada_b)
    mods = mods.reshape(depth, n_cond + pad, N_MOD, d)
    mods = jnp.pad(mods, ((0, 0), (0, 0), (0, MOD_ROWS - N_MOD), (0, 0)))

    rope = _rope_tables(seq)
    xl = x.reshape(n_lat, d)
    xc = ctx.reshape(n_ctx, d)

    tm_l, tm_c = _tile(seq, 512), _tile(nctx, 512)
    tm_pc = _tile(n_ctx, 512)
    tm_fc = _tile(n_ctx, 1024) if nctx <= 256 and _tile(n_ctx, 1024) % nctx == 0 else tm_c
    tq_l = _tile(seq, 512)
    gla_chunk = _tile(nctx, 128)
    tf = 512
    nkv = ATT_KV_HEADS * HEAD_DIM
    hk, hv = GLA_HEADS * dk, GLA_HEADS * dv

    w_up_all, w_down_all = ffn_w_up.astype(BF16), ffn_w_down.astype(BF16)
    w_qkv_all, attn_wo_all = attn_w_qkv.astype(BF16), attn_w_o.astype(BF16)
    gla_qk_all, gla_vr_all = gla_w_in[:, :, :2 * hk].astype(BF16), gla_w_in[:, :, 2 * hk:].astype(BF16)
    gla_wo_all = gla_w_o.astype(BF16)

    for i in range(depth):
        last = i == depth - 1
        mod_l = mods[i, :batch]
        mod_c = mods[i, batch:batch + 1]
        j = i // N_MIXERS
        if i % N_MIXERS == 0:
            w_qkv, w_o = (w_qkv_all, j), (attn_wo_all, j)
            seg = [(nq + 2 * nkv, BF16)]
            qkv_l, = _project(xl, mod_l, norm_mix_g[i], w_qkv, seg, rows_per_mod=seq, tm=tm_l,
                              rope=rope, seq=seq, rope_cols=nq + nkv)
            qkv_c, = _project(xc, mod_c, norm_mix_g[i], w_qkv, seg, rows_per_mod=n_ctx, tm=tm_pc)
            xl = _attention(qkv_l, qkv_c, xl, mod_l, attn_sink[j], w_o, band=True, tq=tq_l, seq=seq, nctx=nctx,
                            rows_per_mod=seq)
            if not last:
                xc = _attention(qkv_c, None, xc, mod_c, attn_sink[j], w_o, band=False, tq=nctx, seq=nctx, nctx=nctx,
                                rows_per_mod=n_ctx)
        else:
            w_qk, w_vr, w_o = (gla_qk_all, j), (gla_vr_all, j), (gla_wo_all, j)
            rank = gla_gf_w1.shape[2]
            w1 = jnp.zeros((d, LANES), F32).at[:, :rank].set(gla_gf_w1[j]).at[:, rank:2 * rank].set(gla_gb_w1[j])
            w2f = jnp.zeros((LANES, hk), F32).at[:rank].set(gla_gf_w2[j])
            w2b = jnp.zeros((LANES, hk), F32).at[rank:2 * rank].set(gla_gb_w2[j])
            gate = (w1.astype(BF16), w2f.astype(BF16), w2b.astype(BF16), gla_gf_b[j], gla_gb_b[j])
            streams = []
            for xs, mod, rpm, tm in ((xc, mod_c, n_ctx, tm_pc), (xl, mod_l, seq, tm_l)):
                qk, gf, gb = _project(xs, mod, norm_mix_g[i], w_qk, [(2 * hk, F32)], rows_per_mod=rpm, tm=tm,
                                      gate=gate)
                v, r = _project(xs, mod, norm_mix_g[i], w_vr, [(hv, BF16), (hv, BF16)], rows_per_mod=rpm, tm=tm)
                streams.append((qk, v, r, gf, gb))
            s0 = jnp.zeros((batch, GLA_HEADS, dk, dv), F32)
            scan = functools.partial(_gla_scan, batch=batch, chunk=gla_chunk, dk=dk, dv=dv)
            (qk_c, v_c, r_c, gf_c, gb_c), (qk_l, v_l, r_l, gf_l, gb_l) = streams
            of_c, ob_c, sf, sb = scan(qk_c, v_c, gf_c, gb_c, s0, s0, length=nctx)
            of_l, ob_l, _, _ = scan(qk_l, v_l, gf_l, gb_l, sf, sb, length=seq)
            xl = _gla_output(of_l, ob_l, r_l, xl, mod_l, gla_onorm_g[j], w_o, tm=tm_l, rows_per_mod=seq, dv=dv)
            if not last:
                xc = _gla_output(of_c, ob_c, r_c, xc, mod_c, gla_onorm_g[j], w_o, tm=tm_c, rows_per_mod=n_ctx, dv=dv)
        w_up, w_down = (w_up_all, i), (w_down_all, i)
        xl = _conv_ffn(xl, mod_l, norm_ffn_g[i], w_up, ffn_conv_w[i], ffn_conv_b[i], w_down,
                       tm=_tile(seq, 1024), tf=tf, seq=seq, rows_per_mod=seq, final_g=final_norm_g if last else None)
        if not last:
            xc = _conv_ffn(xc, mod_c, norm_ffn_g[i], w_up, ffn_conv_w[i], ffn_conv_b[i], w_down,
                           tm=tm_fc, tf=tf, seq=nctx, rows_per_mod=n_ctx)
    return xl.reshape(batch, seq, d)
```

```python
import functools

import jax
import jax.numpy as jnp
from jax import lax
from jax.experimental import pallas as pl
from jax.experimental.pallas import tpu as pltpu

GRID_W = 64
N_MIXERS = 2
ATT_HEADS = 16
ATT_KV_HEADS = 4
ATT_GROUP = ATT_HEADS // ATT_KV_HEADS
HEAD_DIM = 128
WINDOW = 128
ROPE_BASE = 10000.0
GLA_HEADS = 4
GLA_GATE_RANK = 16
GLA_GATE_NORM = 16.0
GLA_BLOCK = 16
CONV_W = 3
EPS = 1e-6
N_MOD = 6
MOD_ROWS = 8

LANES = 128
SUBLANES = 8
VMEM_LIMIT_CAP = 60 * 1024 * 1024

BF16 = jnp.bfloat16
F32 = jnp.float32


def _dot(a, b):
    return jnp.dot(a, b, preferred_element_type=F32)


def _dot_nt(a, b):
    return lax.dot_general(a, b, (((1,), (1,)), ((), ())), preferred_element_type=F32)


def _dot_tn(a, b):
    return lax.dot_general(a, b, (((0,), (0,)), ((), ())), preferred_element_type=F32)


def _params(semantics, vmem_bytes):
    limit = min(VMEM_LIMIT_CAP, int(vmem_bytes * 1.25) + (4 << 20))
    return pltpu.CompilerParams(dimension_semantics=semantics, vmem_limit_bytes=limit)


def _sigmoid(x):
    return 1.0 / (1.0 + jnp.exp(-x))


def _silu(x):
    return x * _sigmoid(x)


def _norm_mod(x, g, shift, scale):
    y = x * lax.rsqrt(jnp.mean(x * x, axis=-1, keepdims=True) + EPS)
    return (y * g) * (1.0 + scale) + shift


def _mod_kernel(c_ref, w_ref, b_ref, o_ref):
    s = _silu(c_ref[...]).astype(BF16)
    o_ref[0] = _dot(s, w_ref[0].astype(BF16)) + b_ref[0]


def _modulations(cc, ada_w, ada_b):
    depth, d, n = ada_w.shape
    r = cc.shape[0]
    tn = 1024
    vmem = 2 * (d * tn * 4 + r * tn * 4 + tn * 4) + r * d * 4 + d * tn * 2
    return pl.pallas_call(
        _mod_kernel,
        out_shape=jax.ShapeDtypeStruct((depth, r, n), F32),
        grid=(depth, n // tn),
        in_specs=[
            pl.BlockSpec((r, d), lambda l, j: (0, 0)),
            pl.BlockSpec((1, d, tn), lambda l, j: (l, 0, j)),
            pl.BlockSpec((1, 1, tn), lambda l, j: (l, 0, j)),
        ],
        out_specs=pl.BlockSpec((1, r, tn), lambda l, j: (l, 0, j)),
        compiler_params=_params(("arbitrary", "arbitrary"), vmem),
        name="adaln_mod",
    )(cc, ada_w, ada_b.reshape(depth, 1, n))


def _log_sigmoid(z):
    return jnp.minimum(z, 0.0) - jnp.log(1.0 + jnp.exp(-jnp.abs(z)))


def _proj_kernel(*refs, segments, rope_cols, gated, col_chunk):
    n_out = len(segments) + (2 if gated else 0)
    x_ref, mod_ref, g_ref, w_ref = refs[:4]
    extra = refs[4:len(refs) - n_out - 2]
    outs = refs[len(refs) - n_out - 2:len(refs) - 2]
    h_even, h_odd = refs[-2:]
    s = pl.program_id(0)

    @pl.when(s == 0)
    def _():
        h_odd[...] = jnp.zeros_like(h_odd)

    def body(h_new, h_old):
        h = h_old[...]
        if gated:
            w1_ref, w2f_ref, w2b_ref, bf_ref, bb_ref = extra
            t = _dot(h, w1_ref[...]).astype(BF16)
            for w2_ref, b_ref, o_ref in ((w2f_ref, bf_ref, outs[-2]), (w2b_ref, bb_ref, outs[-1])):
                o_ref[...] = _dot(t, w2_ref[...]) + b_ref[...]
        if rope_cols:
            cos, sa, sb = extra[0][...], extra[1][...], extra[2][...]
        col = 0
        for (width, _), y_ref in zip(segments, outs):
            for c0 in range(0, width, col_chunk):
                y = _dot(h, w_ref[:, col + c0:col + c0 + col_chunk])
                if col + c0 < rope_cols:
                    parts = []
                    for p in range(col_chunk // HEAD_DIM):
                        yh = y[:, p * HEAD_DIM:(p + 1) * HEAD_DIM]
                        if col + c0 + p * HEAD_DIM < rope_cols:
                            up = pltpu.roll(yh, HEAD_DIM - HEAD_DIM // 4, axis=1)
                            dn = pltpu.roll(yh, HEAD_DIM // 4, axis=1)
                            yh = yh * cos + up * sa + dn * sb
                        parts.append(yh)
                    y = jnp.concatenate(parts, axis=1)
                y_ref[:, c0:c0 + col_chunk] = y.astype(y_ref.dtype)
            col += width
        x = x_ref[...]
        gs = g_ref[...] * (1.0 + mod_ref[0, 1:2, :])
        r = lax.rsqrt(jnp.mean(x * x, axis=-1, keepdims=True) + EPS)
        h_new[...] = ((x * r) * gs + mod_ref[0, 0:1, :]).astype(BF16)

    @pl.when(s % 2 == 0)
    def _():
        body(h_even, h_odd)

    @pl.when(s % 2 == 1)
    def _():
        body(h_odd, h_even)


def _project(x, mod, g, w, segments, *, rows_per_mod, tm, rope=None, seq=None, rope_cols=0, gate=None):
    n, d = x.shape
    w_stack, layer = w
    m = w_stack.shape[2]
    nrow = n // tm
    col_chunk = 1024 if all(wd % 1024 == 0 for wd, _ in segments) else 512

    def cur(s):
        return jnp.minimum(s, nrow - 1)

    def prev(s):
        return jnp.maximum(s - 1, 0)

    in_specs = [
        pl.BlockSpec((tm, d), lambda s: (cur(s), 0)),
        pl.BlockSpec((1, MOD_ROWS, d), lambda s: ((cur(s) * tm) // rows_per_mod, 0, 0)),
        pl.BlockSpec((1, d), lambda s: (0, 0)),
        pl.BlockSpec((None, d, m), lambda s: (layer, 0, 0), pipeline_mode=pl.Buffered(1)),
    ]
    args = [x, mod, g.reshape(1, d), w_stack]
    out_shape = [jax.ShapeDtypeStruct((n, wd), dt) for wd, dt in segments]
    out_specs = [pl.BlockSpec((tm, wd), lambda s: (prev(s), 0)) for wd, _ in segments]
    vmem = (2 * tm * d * 4 + d * m * 2 + 2 * sum(tm * wd * jnp.dtype(dt).itemsize for wd, dt in segments)
            + 2 * tm * d * 2 + 3 * tm * col_chunk * 4 + 2 * tm * d * 4)
    if rope is not None:
        tiles_per_seq = seq // tm
        for tab in rope:
            in_specs.append(pl.BlockSpec((tm, HEAD_DIM), lambda s: (prev(s) % tiles_per_seq, 0)))
            args.append(tab)
        vmem += 6 * tm * HEAD_DIM * 4
    if gate is not None:
        w1, w2f, w2b, bf, bb = gate
        dk = w2f.shape[1]
        in_specs += [
            pl.BlockSpec(w1.shape, lambda s: (0, 0)),
            pl.BlockSpec(w2f.shape, lambda s: (0, 0)),
            pl.BlockSpec(w2b.shape, lambda s: (0, 0)),
            pl.BlockSpec((1, dk), lambda s: (0, 0)),
            pl.BlockSpec((1, dk), lambda s: (0, 0)),
        ]
        args += [w1, w2f, w2b, bf.reshape(1, dk), bb.reshape(1, dk)]
        out_shape += [jax.ShapeDtypeStruct((n, dk), F32)] * 2
        out_specs += [pl.BlockSpec((tm, dk), lambda s: (prev(s), 0))] * 2
        vmem += 6 * tm * dk * 4 + 2 * (d * LANES * 2 + 2 * LANES * dk * 2)
    return pl.pallas_call(
        functools.partial(_proj_kernel, segments=tuple(segments), rope_cols=rope_cols, gated=gate is not None,
                          col_chunk=col_chunk),
        out_shape=out_shape,
        grid=(nrow + 1,),
        in_specs=in_specs,
        out_specs=out_specs,
        scratch_shapes=[pltpu.VMEM((tm, d), BF16), pltpu.VMEM((tm, d), BF16)],
        compiler_params=_params(("arbitrary",), vmem),
        name="proj",
    )(*args)


def _attend(qg, kall, vall, mask, sink_row):
    s = _dot_nt(kall, qg) * (HEAD_DIM ** -0.5)
    if mask is not None:
        s = jnp.where(mask, s, -jnp.inf)
    mx = jnp.maximum(jnp.max(s, axis=0, keepdims=True), sink_row)
    e = jnp.exp(s - mx)
    denom = jnp.sum(e, axis=0, keepdims=True) + jnp.exp(sink_row - mx)
    return _dot_tn(vall, e.astype(BF16)) / denom


def _sink_row(sink_ref, kv, rows):
    cols = [jnp.broadcast_to(sink_ref[0:1, kv * ATT_GROUP + h:kv * ATT_GROUP + h + 1], (1, rows))
            for h in range(ATT_GROUP)]
    return jnp.concatenate(cols, axis=1)


def _attn_kernel(*refs, band, tq, seq, sub):
    if band:
        (q_ref, kp_ref, kc_ref, kn_ref, vp_ref, vc_ref, vn_ref, kx_ref, vx_ref,
         x_ref, mod_ref, sink_ref, wo_ref, o_ref) = refs
    else:
        q_ref, kx_ref, vx_ref, x_ref, mod_ref, sink_ref, wo_ref, o_ref = refs
    blk = WINDOW
    nsub = tq // sub if band else 1
    rows = sub if band else tq
    if band:
        kwin = jnp.concatenate([kp_ref[...], kc_ref[...], kn_ref[...]], axis=0)
        vwin = jnp.concatenate([vp_ref[...], vc_ref[...], vn_ref[...]], axis=0)
        t0 = (pl.program_id(0) * tq) % seq
        nctx = kx_ref.shape[0]
        nwin = sub + 2 * blk
        ncol = nwin + nctx
        kj = lax.broadcasted_iota(jnp.int32, (ncol, ATT_GROUP * sub), 0)
        qi = lax.broadcasted_iota(jnp.int32, (ncol, ATT_GROUP * sub), 1) & (sub - 1)
        in_band = jnp.abs(qi - kj + blk) <= WINDOW
    outs = []
    for sb in range(nsub):
        heads = [None] * ATT_HEADS
        for kv in range(ATT_KV_HEADS):
            lo = kv * HEAD_DIM
            qg = jnp.concatenate(
                [q_ref[sb * rows:(sb + 1) * rows, (kv * ATT_GROUP + h) * HEAD_DIM:(kv * ATT_GROUP + h + 1) * HEAD_DIM]
                 for h in range(ATT_GROUP)], axis=0)
            if band:
                kall = jnp.concatenate([kwin[sb * sub:sb * sub + nwin, lo:lo + HEAD_DIM],
                                        kx_ref[:, lo:lo + HEAD_DIM]], axis=0)
                vall = jnp.concatenate([vwin[sb * sub:sb * sub + nwin, lo:lo + HEAD_DIM],
                                        vx_ref[:, lo:lo + HEAD_DIM]], axis=0)
                kpos = t0 + sb * sub - blk + kj
                mask = (kj >= nwin) | (in_band & (kpos >= 0) & (kpos < seq))
            else:
                kall = kx_ref[:, lo:lo + HEAD_DIM]
                vall = vx_ref[:, lo:lo + HEAD_DIM]
                mask = None
            og = _attend(qg, kall, vall, mask, _sink_row(sink_ref, kv, rows))
            for h in range(ATT_GROUP):
                heads[kv * ATT_GROUP + h] = og[:, h * rows:(h + 1) * rows].T.astype(BF16)
        outs.append(jnp.concatenate(heads, axis=1))
    o = outs[0] if len(outs) == 1 else jnp.concatenate(outs, axis=0)
    o_ref[...] = x_ref[...] + mod_ref[0, 2:3, :] * _dot(o, wo_ref[...])


def _attention(qkv, qkv_ctx, x, mod, sink, w_o, *, band, tq, seq, nctx, rows_per_mod):
    n, d = x.shape
    wo_stack, layer = w_o
    nq = ATT_HEADS * HEAD_DIM
    nkv = ATT_KV_HEADS * HEAD_DIM
    kcol, vcol = nq // nkv, nq // nkv + 1
    blk = WINDOW
    nblk = n // blk
    r = tq // blk
    q_spec = pl.BlockSpec((tq, nq), lambda i: (i, 0))
    tail_specs = [
        pl.BlockSpec((tq, d), lambda i: (i, 0)),
        pl.BlockSpec((1, MOD_ROWS, d), lambda i: ((i * tq) // rows_per_mod, 0, 0)),
        pl.BlockSpec((1, ATT_HEADS), lambda i: (0, 0)),
        pl.BlockSpec((None, nq, d), lambda i: (layer, 0, 0)),
    ]
    tail_args = [x, mod, sink.reshape(1, ATT_HEADS), wo_stack]
    if band:
        def ctx_idx(col):
            return lambda i: ((i * tq) // seq, col)
        in_specs = [
            q_spec,
            pl.BlockSpec((blk, nkv), lambda i: (jnp.maximum(i * r - 1, 0), kcol)),
            pl.BlockSpec((tq, nkv), lambda i: (i, kcol)),
            pl.BlockSpec((blk, nkv), lambda i: (jnp.minimum((i + 1) * r, nblk - 1), kcol)),
            pl.BlockSpec((blk, nkv), lambda i: (jnp.maximum(i * r - 1, 0), vcol)),
            pl.BlockSpec((tq, nkv), lambda i: (i, vcol)),
            pl.BlockSpec((blk, nkv), lambda i: (jnp.minimum((i + 1) * r, nblk - 1), vcol)),
            pl.BlockSpec((nctx, nkv), ctx_idx(kcol)),
            pl.BlockSpec((nctx, nkv), ctx_idx(vcol)),
        ] + tail_specs
        args = [qkv] * 7 + [qkv_ctx] * 2 + tail_args
        ncol = 3 * blk + nctx
    else:
        in_specs = [
            q_spec,
            pl.BlockSpec((tq, nkv), lambda i: (i, kcol)),
            pl.BlockSpec((tq, nkv), lambda i: (i, vcol)),
        ] + tail_specs
        args = [qkv] * 3 + tail_args
        ncol = tq
    vmem = (2 * (tq * nq * 2 + 4 * (tq + 2 * blk + nctx) * nkv * 2 + 2 * tq * d * 4 + nq * d * 2)
            + 6 * ATT_GROUP * blk * ncol * 4 + 2 * tq * d * 4)
    return pl.pallas_call(
        functools.partial(_attn_kernel, band=band, tq=tq, seq=seq, sub=min(tq, WINDOW)),
        out_shape=jax.ShapeDtypeStruct((n, d), F32),
        grid=(n // tq,),
        in_specs=in_specs,
        out_specs=pl.BlockSpec((tq, d), lambda i: (i, 0)),
        compiler_params=_params(("arbitrary",), vmem),
        name="attn_band" if band else "attn_ctx",
    )(*args)


def _gla_scan_kernel(qkf_ref, vf_ref, gf_ref, qkb_ref, vb_ref, gb_ref, s0f_ref, s0b_ref,
                     of_ref, ob_ref, sf_ref, sb_ref, *, chunk, dk, dv, q_scale):
    c = pl.program_id(1)

    @pl.when(c == 0)
    def _():
        sf_ref[...] = s0f_ref[...]
        sb_ref[...] = s0b_ref[...]

    nb = chunk // GLA_BLOCK
    levels = nb.bit_length() - 1
    row = lax.broadcasted_iota(jnp.int32, (chunk, chunk), 0)
    col = lax.broadcasted_iota(jnp.int32, (chunk, chunk), 1)
    blk_xor = (row // GLA_BLOCK) ^ (col // GLA_BLOCK)
    directions = (
        (qkf_ref, vf_ref, gf_ref, of_ref, sf_ref, False),
        (qkb_ref, vb_ref, gb_ref, ob_ref, sb_ref, True),
    )
    for qk_ref, v_ref, g_ref, o_ref, s_ref, reverse in directions:
        tri = (col >= row) if reverse else (col <= row)
        earlier = (col > row) if reverse else (col < row)
        tri_b = tri.astype(BF16)
        g = _log_sigmoid(g_ref[...]) / GLA_GATE_NORM
        g_hi = g.astype(BF16)
        g_lo = (g - g_hi.astype(F32)).astype(BF16)
        b = _dot(tri_b, g_hi) + _dot(tri_b, g_lo)

        hk = b.shape[1]
        end_row = 0 if reverse else GLA_BLOCK - 1
        ends = [b[i * GLA_BLOCK + end_row:i * GLA_BLOCK + end_row + 1] for i in range(nb)]
        zero = jnp.zeros((1, hk), F32)
        if reverse:
            starts = [ends[i + 1] if i + 1 < nb else zero for i in range(nb)]
        else:
            starts = [ends[i - 1] if i > 0 else zero for i in range(nb)]
        b_tot = ends[0] if reverse else ends[nb - 1]

        def group_first(i, size):
            return i | (size - 1) if reverse else i & ~(size - 1)

        def group_last(i, size):
            return i & ~(size - 1) if reverse else i | (size - 1)

        q = qk_ref[:, :hk] * q_scale
        k = qk_ref[:, hk:]
        v = v_ref[...]
        q_lv = [[] for _ in range(levels)]
        k_lv = [[] for _ in range(levels)]
        k_diag, q_state, k_state = [], [], []
        for i in range(nb):
            rs = slice(i * GLA_BLOCK, (i + 1) * GLA_BLOCK)
            q0 = q[rs] * jnp.exp(b[rs] - starts[i])
            k0 = k[rs] * jnp.exp(ends[i] - b[rs])
            k_diag.append(k[rs] * jnp.exp(starts[i] - b[rs]))
            for lv in range(levels):
                size = 1 << lv
                later = bool(i & size) != reverse
                q_lv[lv].append(q0 * jnp.exp(starts[i] - starts[group_first(i, size)]) if lv and later else q0)
                k_lv[lv].append(k0 * jnp.exp(ends[group_last(i, size)] - ends[i]) if lv and not later else k0)
            q_state.append(q0 * jnp.exp(starts[i]))
            k_state.append(k0 * jnp.exp(b_tot - ends[i]))

        def cat(parts):
            return jnp.concatenate(parts, axis=0).astype(BF16)

        q_lv, k_lv = [cat(p) for p in q_lv], [cat(p) for p in k_lv]
        k_diag, qs, ks = cat(k_diag), cat(q_state), cat(k_state)
        decay = jnp.exp(jnp.broadcast_to(b_tot, (LANES, hk))).T

        for h in range(GLA_HEADS):
            kc = slice(h * dk, (h + 1) * dk)
            vc = slice(h * dv, (h + 1) * dv)
            a = jnp.where((blk_xor == 0) & tri, _dot_nt(q_lv[0][:, kc], k_diag[:, kc]), 0.0)
            for lv in range(levels):
                pair = ((blk_xor >> lv) == 1) & earlier
                a = jnp.where(pair, _dot_nt(q_lv[lv][:, kc], k_lv[lv][:, kc]), a)
            a = a.astype(BF16)
            st = s_ref[0, h]
            o_ref[:, vc] = (_dot(a, v[:, vc]) + _dot(qs[:, kc], st.astype(BF16))).astype(o_ref.dtype)
            upd = _dot_tn(ks[:, kc], v[:, vc])
            s_ref[0, h] = jnp.concatenate(
                [st[:, t * LANES:(t + 1) * LANES] * decay[kc] for t in range(dv // LANES)], axis=1) + upd


def _gla_scan(qk, v, g_f, g_b, s0_f, s0_b, *, batch, length, chunk, dk, dv):
    nc = length // chunk
    hk, hv = GLA_HEADS * dk, GLA_HEADS * dv

    def fwd(b, c):
        return b * nc + c

    def bwd(b, c):
        return b * nc + nc - 1 - c

    state_spec = pl.BlockSpec((1, GLA_HEADS, dk, dv), lambda b, c: (b, 0, 0, 0))
    in_specs = []
    for rows in (fwd, bwd):
        in_specs += [
            pl.BlockSpec((chunk, 2 * hk), lambda b, c, rows=rows: (rows(b, c), 0)),
            pl.BlockSpec((chunk, hv), lambda b, c, rows=rows: (rows(b, c), 0)),
            pl.BlockSpec((chunk, hk), lambda b, c, rows=rows: (rows(b, c), 0)),
        ]
    in_specs += [state_spec, state_spec]
    state_bytes = GLA_HEADS * dk * dv * 4
    vmem = (2 * 2 * (3 * chunk * hk * 4 + 2 * chunk * hv * 2) + 8 * state_bytes
            + 16 * chunk * hk * 4 + 4 * dk * dv * 4)
    return pl.pallas_call(
        functools.partial(_gla_scan_kernel, chunk=chunk, dk=dk, dv=dv, q_scale=dk ** -0.5),
        out_shape=[jax.ShapeDtypeStruct((batch * length, hv), BF16)] * 2
        + [jax.ShapeDtypeStruct(s0_f.shape, F32)] * 2,
        grid=(batch, nc),
        in_specs=in_specs,
        out_specs=[
            pl.BlockSpec((chunk, hv), lambda b, c: (fwd(b, c), 0)),
            pl.BlockSpec((chunk, hv), lambda b, c: (bwd(b, c), 0)),
            state_spec,
            state_spec,
        ],
        compiler_params=_params(("arbitrary", "arbitrary"), vmem),
        name="gla_scan",
    )(qk, v, g_f, qk, v, g_b, s0_f, s0_b)


def _gla_out_kernel(of_ref, ob_ref, r_ref, x_ref, mod_ref, g_ref, wo_ref, o_ref, *, dv):
    o = of_ref[...].astype(F32) + ob_ref[...].astype(F32)
    parts = []
    for h in range(o.shape[1] // dv):
        oh = o[:, h * dv:(h + 1) * dv]
        parts.append((oh * lax.rsqrt(jnp.mean(oh * oh, axis=-1, keepdims=True) + EPS)) * g_ref[...])
    gated = (jnp.concatenate(parts, axis=1) * _silu(r_ref[...].astype(F32))).astype(BF16)
    o_ref[...] = x_ref[...] + mod_ref[0, 2:3, :] * _dot(gated, wo_ref[...])


def _gla_output(o_f, o_b, r, x, mod, onorm_g, w_o, *, tm, rows_per_mod, dv):
    n, d = x.shape
    wo_stack, layer = w_o
    dvt = o_f.shape[1]
    vmem = 2 * (5 * tm * dvt * 4 + dvt * d * 2) + 4 * tm * dvt * 4
    return pl.pallas_call(
        functools.partial(_gla_out_kernel, dv=dv),
        out_shape=jax.ShapeDtypeStruct((n, d), F32),
        grid=(n // tm,),
        in_specs=[
            pl.BlockSpec((tm, dvt), lambda i: (i, 0)),
            pl.BlockSpec((tm, dvt), lambda i: (i, 0)),
            pl.BlockSpec((tm, dvt), lambda i: (i, 0)),
            pl.BlockSpec((tm, d), lambda i: (i, 0)),
            pl.BlockSpec((1, MOD_ROWS, d), lambda i: ((i * tm) // rows_per_mod, 0, 0)),
            pl.BlockSpec((1, dv), lambda i: (0, 0)),
            pl.BlockSpec((None, dvt, d), lambda i: (layer, 0, 0)),
        ],
        out_specs=pl.BlockSpec((tm, d), lambda i: (i, 0)),
        compiler_params=_params(("arbitrary",), vmem),
        name="gla_out",
    )(o_f, o_b, r, x, mod, onorm_g.reshape(1, dv), wo_stack)


def _ffn_kernel(xp_ref, x_ref, xn_ref, mod_ref, g_ref, fg_ref, wg_ref, wv_ref, cw_ref, cb_ref,
                wd_ref, o_ref, h_scr, ug_scr, uv_scr, act_scr, *, tm, seq, final_norm, nf):
    i = pl.program_id(0)
    j = pl.program_id(1)
    halo = SUBLANES
    rows = min(tm, seq, 256)
    assert tm <= seq or (rows == seq and tm % seq == 0)

    @pl.when(j == 0)
    def _():
        shift = mod_ref[0, 3:4, :]
        gs = g_ref[...] * (1.0 + mod_ref[0, 4:5, :])

        def norm_mod(x):
            return (x * lax.rsqrt(jnp.mean(x * x, axis=-1, keepdims=True) + EPS)) * gs + shift

        first = (i * tm) % seq == 0
        last = ((i + 1) * tm) % seq == 0
        h_scr[0:halo, :] = jnp.where(first, 0.0, norm_mod(xp_ref[...])).astype(BF16)
        for r0 in range(0, tm, rows):
            h_scr[halo + r0:halo + r0 + rows, :] = norm_mod(x_ref[r0:r0 + rows, :]).astype(BF16)
        h_scr[halo + tm:2 * halo + tm, :] = jnp.where(last, 0.0, norm_mod(xn_ref[...])).astype(BF16)
        o_ref[...] = jnp.zeros_like(o_ref)

    h = h_scr[...]
    ug_scr[...] = _dot(h, wg_ref[...])
    uv_scr[...] = _dot(h, wv_ref[...])

    tf = wg_ref.shape[1]
    gate_cols = pl.ds(pl.multiple_of(j * tf, tf), tf)
    val_cols = pl.ds(pl.multiple_of((nf + j) * tf, tf), tf)
    cwg, cbg = cw_ref[:, gate_cols], cb_ref[:, gate_cols]
    cwv, cbv = cw_ref[:, val_cols], cb_ref[:, val_cols]

    def conv(u_scr, cw, cb, r0):
        prev = u_scr[halo - 1 + r0:halo - 1 + r0 + rows, :]
        nxt = u_scr[halo + 1 + r0:halo + 1 + r0 + rows, :]
        if tm > seq:
            rid = lax.broadcasted_iota(jnp.int32, prev.shape, 0)
            prev = jnp.where(rid == 0, 0.0, prev)
            nxt = jnp.where(rid == rows - 1, 0.0, nxt)
        return cb + cw[0:1, :] * prev + cw[1:2, :] * u_scr[halo + r0:halo + r0 + rows, :] + cw[2:3, :] * nxt

    for r0 in range(0, tm, rows):
        act_scr[r0:r0 + rows, :] = (_silu(conv(ug_scr, cwg, cbg, r0)) * conv(uv_scr, cwv, cbv, r0)).astype(BF16)
    act = act_scr[...]
    for c0 in range(0, o_ref.shape[1], 512):
        o_ref[:, c0:c0 + 512] += _dot(act, wd_ref[:, c0:c0 + 512])

    @pl.when(j == pl.num_programs(1) - 1)
    def _():
        er = min(tm, 128)

        def residual(c, carry):
            sl = pl.ds(pl.multiple_of(c * er, er), er)
            y = x_ref[sl, :] + mod_ref[0, 5:6, :] * o_ref[sl, :]
            if final_norm:
                y = (y * lax.rsqrt(jnp.mean(y * y, axis=-1, keepdims=True) + EPS)) * fg_ref[...]
            o_ref[sl, :] = y
            return carry

        lax.fori_loop(0, tm // er, residual, 0)


def _conv_ffn(x, mod, g, w_up, conv_w, conv_b, w_down, *, tm, tf, seq, rows_per_mod, final_g=None):
    n, d = x.shape
    (up_stack, layer), (down_stack, _) = w_up, w_down
    dff = down_stack.shape[1]
    nf = dff // tf
    halo = SUBLANES
    r = tm // halo
    nhalo = n // halo
    conv_b = conv_b.reshape(1, 2 * dff)
    fg = (g if final_g is None else final_g).reshape(1, d)
    vmem = (2 * (2 * tm * d * 4 + 2 * d * tf * 2 + tf * d * 2) + (tm + 2 * halo) * d * 2
            + 2 * (tm + 2 * halo) * tf * 4 + tm * tf * 2 + 8 * min(tm, 256) * max(tf, d) * 4)
    return pl.pallas_call(
        functools.partial(_ffn_kernel, tm=tm, seq=seq, final_norm=final_g is not None, nf=nf),
        out_shape=jax.ShapeDtypeStruct((n, d), F32),
        grid=(n // tm, nf),
        in_specs=[
            pl.BlockSpec((halo, d), lambda i, j: (jnp.maximum(i * r - 1, 0), 0)),
            pl.BlockSpec((tm, d), lambda i, j: (i, 0)),
            pl.BlockSpec((halo, d), lambda i, j: (jnp.minimum((i + 1) * r, nhalo - 1), 0)),
            pl.BlockSpec((1, MOD_ROWS, d), lambda i, j: ((i * tm) // rows_per_mod, 0, 0)),
            pl.BlockSpec((1, d), lambda i, j: (0, 0)),
            pl.BlockSpec((1, d), lambda i, j: (0, 0)),
            pl.BlockSpec((None, d, tf), lambda i, j: (layer, 0, j)),
            pl.BlockSpec((None, d, tf), lambda i, j: (layer, 0, nf + j)),
            pl.BlockSpec((CONV_W, 2 * dff), lambda i, j: (0, 0)),
            pl.BlockSpec((1, 2 * dff), lambda i, j: (0, 0)),
            pl.BlockSpec((None, tf, d), lambda i, j: (layer, j, 0)),
        ],
        out_specs=pl.BlockSpec((tm, d), lambda i, j: (i, 0)),
        scratch_shapes=[
            pltpu.VMEM((tm + 2 * halo, d), BF16),
            pltpu.VMEM((tm + 2 * halo, tf), F32),
            pltpu.VMEM((tm + 2 * halo, tf), F32),
            pltpu.VMEM((tm, tf), BF16),
        ],
        compiler_params=_params(("arbitrary", "arbitrary"), vmem),
        name="conv_ffn",
    )(x, x, x, mod, g.reshape(1, d), fg, up_stack, up_stack, conv_w, conv_b, down_stack)


def _rope_tables(seq):
    half = HEAD_DIM // 2
    rows = seq // GRID_W
    row = jnp.repeat(jnp.arange(rows, dtype=F32), GRID_W)
    col = jnp.tile(jnp.arange(GRID_W, dtype=F32), rows)
    inv = ROPE_BASE ** (-jnp.arange(0, half, 2, dtype=F32) / half)
    ang_r = row[:, None] * inv[None, :]
    ang_c = col[:, None] * inv[None, :]
    zero = jnp.zeros_like(ang_r)
    cos = jnp.concatenate([jnp.cos(ang_r)] * 2 + [jnp.cos(ang_c)] * 2, axis=1)
    sa = jnp.concatenate([-jnp.sin(ang_r), zero, -jnp.sin(ang_c), zero], axis=1)
    sb = jnp.concatenate([zero, jnp.sin(ang_r), zero, jnp.sin(ang_c)], axis=1)
    return cos, sa, sb


def _tile(n, target):
    t = min(n, target)
    while n % t:
        t //= 2
    return t


def kernel(x, c, ctx, c_ctx, ada_w, ada_b, norm_mix_g, norm_ffn_g, ffn_w_up, ffn_conv_w, ffn_conv_b, ffn_w_down,
           attn_w_qkv, attn_sink, attn_w_o, gla_w_in, gla_gf_w1, gla_gf_w2, gla_gf_b, gla_gb_w1, gla_gb_w2,
           gla_gb_b, gla_onorm_g, gla_w_o, final_norm_g):
    batch, seq, d = x.shape
    nctx = ctx.shape[1]
    depth = ada_w.shape[0]
    n_lat, n_ctx = batch * seq, batch * nctx
    dk = gla_gf_w2.shape[2] // GLA_HEADS
    dv = gla_onorm_g.shape[1]
    nq = ATT_HEADS * HEAD_DIM
    assert seq % WINDOW == 0 and nctx % WINDOW == 0 and d % LANES == 0

    n_cond = batch + 1
    pad = (-n_cond) % SUBLANES
    cc = jnp.concatenate([c, c_ctx[None, :], jnp.zeros((pad, d), F32)], axis=0)
    mods = _modulations(cc, ada_w, ada_b)
    mods = mods.reshape(depth, n_cond + pad, N_MOD, d)
    mods = jnp.pad(mods, ((0, 0), (0, 0), (0, MOD_ROWS - N_MOD), (0, 0)))

    rope = _rope_tables(seq)
    xl = x.reshape(n_lat, d)
    xc = ctx.reshape(n_ctx, d)

    tm_l, tm_c = _tile(seq, 512), _tile(nctx, 512)
    tm_pc = _tile(n_ctx, 512)
    tm_fc = _tile(n_ctx, 1024) if nctx <= 256 and _tile(n_ctx, 1024) % nctx == 0 else tm_c
    tq_l = _tile(seq, 512)
    gla_chunk = _tile(nctx, 128)
    tf = 512
    nkv = ATT_KV_HEADS * HEAD_DIM
    hk, hv = GLA_HEADS * dk, GLA_HEADS * dv

    w_up_all, w_down_all = ffn_w_up.astype(BF16), ffn_w_down.astype(BF16)
    w_qkv_all, attn_wo_all = attn_w_qkv.astype(BF16), attn_w_o.astype(BF16)
    gla_qk_all, gla_vr_all = gla_w_in[:, :, :2 * hk].astype(BF16), gla_w_in[:, :, 2 * hk:].astype(BF16)
    gla_wo_all = gla_w_o.astype(BF16)

    for i in range(depth):
        last = i == depth - 1
        mod_l = mods[i, :batch]
        mod_c = mods[i, batch:batch + 1]
        j = i // N_MIXERS
        if i % N_MIXERS == 0:
            w_qkv, w_o = (w_qkv_all, j), (attn_wo_all, j)
            seg = [(nq + 2 * nkv, BF16)]
            qkv_l, = _project(xl, mod_l, norm_mix_g[i], w_qkv, seg, rows_per_mod=seq, tm=tm_l,
                              rope=rope, seq=seq, rope_cols=nq + nkv)
            qkv_c, = _project(xc, mod_c, norm_mix_g[i], w_qkv, seg, rows_per_mod=n_ctx, tm=tm_pc)
            xl = _attention(qkv_l, qkv_c, xl, mod_l, attn_sink[j], w_o, band=True, tq=tq_l, seq=seq, nctx=nctx,
                            rows_per_mod=seq)
            if not last:
                xc = _attention(qkv_c, None, xc, mod_c, attn_sink[j], w_o, band=False, tq=nctx, seq=nctx, nctx=nctx,
                                rows_per_mod=n_ctx)
        else:
            w_qk, w_vr, w_o = (gla_qk_all, j), (gla_vr_all, j), (gla_wo_all, j)
            rank = gla_gf_w1.shape[2]
            w1 = jnp.zeros((d, LANES), F32).at[:, :rank].set(gla_gf_w1[j]).at[:, rank:2 * rank].set(gla_gb_w1[j])
            w2f = jnp.zeros((LANES, hk), F32).at[:rank].set(gla_gf_w2[j])
            w2b = jnp.zeros((LANES, hk), F32).at[rank:2 * rank].set(gla_gb_w2[j])
            gate = (w1.astype(BF16), w2f.astype(BF16), w2b.astype(BF16), gla_gf_b[j], gla_gb_b[j])
            streams = []
            for xs, mod, rpm, tm in ((xc, mod_c, n_ctx, tm_pc), (xl, mod_l, seq, tm_l)):
                qk, gf, gb = _project(xs, mod, norm_mix_g[i], w_qk, [(2 * hk, F32)], rows_per_mod=rpm, tm=tm,
                                      gate=gate)
                v, r = _project(xs, mod, norm_mix_g[i], w_vr, [(hv, BF16), (hv, BF16)], rows_per_mod=rpm, tm=tm)
                streams.append((qk, v, r, gf, gb))
            s0 = jnp.zeros((batch, GLA_HEADS, dk, dv), F32)
            scan = functools.partial(_gla_scan, batch=batch, chunk=gla_chunk, dk=dk, dv=dv)
            (qk_c, v_c, r_c, gf_c, gb_c), (qk_l, v_l, r_l, gf_l, gb_l) = streams
            of_c, ob_c, sf, sb = scan(qk_c, v_c, gf_c, gb_c, s0, s0, length=nctx)
            of_l, ob_l, _, _ = scan(qk_l, v_l, gf_l, gb_l, sf, sb, length=seq)
            xl = _gla_output(of_l, ob_l, r_l, xl, mod_l, gla_onorm_g[j], w_o, tm=tm_l, rows_per_mod=seq, dv=dv)
            if not last:
                xc = _gla_output(of_c, ob_c, r_c, xc, mod_c, gla_onorm_g[j], w_o, tm=tm_c, rows_per_mod=n_ctx, dv=dv)
        w_up, w_down = (w_up_all, i), (w_down_all, i)
        xl = _conv_ffn(xl, mod_l, norm_ffn_g[i], w_up, ffn_conv_w[i], ffn_conv_b[i], w_down,
                       tm=_tile(seq, 1024), tf=tf, seq=seq, rows_per_mod=seq, final_g=final_norm_g if last else None)
        if not last:
            xc = _conv_ffn(xc, mod_c, norm_ffn_g[i], w_up, ffn_conv_w[i], ffn_conv_b[i], w_down,
                           tm=tm_fc, tf=tf, seq=nctx, rows_per_mod=n_ctx)
    return xl.reshape(batch, seq, d)
```

```python
import functools

import jax
import jax.numpy as jnp
from jax import lax
from jax.experimental import pallas as pl
from jax.experimental.pallas import tpu as pltpu

GRID_W = 64
N_MIXERS = 2
ATT_HEADS = 16
ATT_KV_HEADS = 4
ATT_GROUP = ATT_HEADS // ATT_KV_HEADS
HEAD_DIM = 128
WINDOW = 128
ROPE_BASE = 10000.0
GLA_HEADS = 4
GLA_GATE_NORM = 16.0
GLA_BLOCK = 16
CONV_W = 3
EPS = 1e-6
N_MOD = 6
MOD_ROWS = 8

LANES = 128
SUBLANES = 8
VMEM_LIMIT_CAP = 60 * 1024 * 1024

BF16 = jnp.bfloat16
F32 = jnp.float32


def _dot(a, b):
    return jnp.dot(a, b, preferred_element_type=F32)


def _dot_nt(a, b):
    return lax.dot_general(a, b, (((1,), (1,)), ((), ())), preferred_element_type=F32)


def _dot_tn(a, b):
    return lax.dot_general(a, b, (((0,), (0,)), ((), ())), preferred_element_type=F32)


def _params(semantics, vmem_bytes):
    limit = min(VMEM_LIMIT_CAP, int(vmem_bytes * 1.25) + (4 << 20))
    return pltpu.CompilerParams(dimension_semantics=semantics, vmem_limit_bytes=limit)


def _sigmoid(x):
    return 1.0 / (1.0 + jnp.exp(-x))


def _silu(x):
    return x * _sigmoid(x)


def _mod_kernel(c_ref, w_ref, b_ref, o_ref):
    s = _silu(c_ref[...]).astype(BF16)
    o_ref[0] = _dot(s, w_ref[0].astype(BF16)) + b_ref[0]


def _modulations(cc, ada_w, ada_b):
    depth, d, n = ada_w.shape
    r = cc.shape[0]
    tn = 1024
    vmem = 2 * (d * tn * 4 + r * tn * 4 + tn * 4) + r * d * 4 + d * tn * 2
    return pl.pallas_call(
        _mod_kernel,
        out_shape=jax.ShapeDtypeStruct((depth, r, n), F32),
        grid=(depth, n // tn),
        in_specs=[
            pl.BlockSpec((r, d), lambda l, j: (0, 0)),
            pl.BlockSpec((1, d, tn), lambda l, j: (l, 0, j)),
            pl.BlockSpec((1, 1, tn), lambda l, j: (l, 0, j)),
        ],
        out_specs=pl.BlockSpec((1, r, tn), lambda l, j: (l, 0, j)),
        compiler_params=_params(("arbitrary", "arbitrary"), vmem),
        name="adaln_mod",
    )(cc, ada_w, ada_b.reshape(depth, 1, n))


def _log_sigmoid(z):
    return jnp.minimum(z, 0.0) - jnp.log(1.0 + jnp.exp(-jnp.abs(z)))


def _proj_kernel(*refs, segments, rope_cols, gated, col_chunk):
    n_out = len(segments) + (2 if gated else 0)
    x_ref, mod_ref, g_ref, w_ref = refs[:4]
    extra = refs[4:len(refs) - n_out - 2]
    outs = refs[len(refs) - n_out - 2:len(refs) - 2]
    h_even, h_odd = refs[-2:]
    s = pl.program_id(0)

    @pl.when(s == 0)
    def _():
        h_odd[...] = jnp.zeros_like(h_odd)

    def body(h_new, h_old):
        h = h_old[...]
        if gated:
            w1_ref, w2f_ref, w2b_ref, bf_ref, bb_ref = extra
            t = _dot(h, w1_ref[...]).astype(BF16)
            for w2_ref, b_ref, o_ref in ((w2f_ref, bf_ref, outs[-2]), (w2b_ref, bb_ref, outs[-1])):
                o_ref[...] = _dot(t, w2_ref[...]) + b_ref[...]
        if rope_cols:
            cos, sa, sb = extra[0][...], extra[1][...], extra[2][...]
        col = 0
        for (width, _), y_ref in zip(segments, outs):
            for c0 in range(0, width, col_chunk):
                y = _dot(h, w_ref[:, col + c0:col + c0 + col_chunk])
                if col + c0 < rope_cols:
                    parts = []
                    for p in range(col_chunk // HEAD_DIM):
                        yh = y[:, p * HEAD_DIM:(p + 1) * HEAD_DIM]
                        if col + c0 + p * HEAD_DIM < rope_cols:
                            up = pltpu.roll(yh, HEAD_DIM - HEAD_DIM // 4, axis=1)
                            dn = pltpu.roll(yh, HEAD_DIM // 4, axis=1)
                            yh = yh * cos + up * sa + dn * sb
                        parts.append(yh)
                    y = jnp.concatenate(parts, axis=1)
                y_ref[:, c0:c0 + col_chunk] = y.astype(y_ref.dtype)
            col += width
        x = x_ref[...]
        gs = g_ref[...] * (1.0 + mod_ref[0, 1:2, :])
        r = lax.rsqrt(jnp.mean(x * x, axis=-1, keepdims=True) + EPS)
        h_new[...] = ((x * r) * gs + mod_ref[0, 0:1, :]).astype(BF16)

    @pl.when(s % 2 == 0)
    def _():
        body(h_even, h_odd)

    @pl.when(s % 2 == 1)
    def _():
        body(h_odd, h_even)


def _project(x, mod, g, w, segments, *, rows_per_mod, tm, rope=None, seq=None, rope_cols=0, gate=None):
    n, d = x.shape
    w_stack, layer = w
    m = w_stack.shape[2]
    nrow = n // tm
    col_chunk = 1024 if all(wd % 1024 == 0 for wd, _ in segments) else 512

    def cur(s):
        return jnp.minimum(s, nrow - 1)

    def prev(s):
        return jnp.maximum(s - 1, 0)

    in_specs = [
        pl.BlockSpec((tm, d), lambda s: (cur(s), 0)),
        pl.BlockSpec((1, MOD_ROWS, d), lambda s: ((cur(s) * tm) // rows_per_mod, 0, 0)),
        pl.BlockSpec((1, d), lambda s: (0, 0)),
        pl.BlockSpec((None, d, m), lambda s: (layer, 0, 0), pipeline_mode=pl.Buffered(1)),
    ]
    args = [x, mod, g.reshape(1, d), w_stack]
    out_shape = [jax.ShapeDtypeStruct((n, wd), dt) for wd, dt in segments]
    out_specs = [pl.BlockSpec((tm, wd), lambda s: (prev(s), 0)) for wd, _ in segments]
    vmem = (2 * tm * d * 4 + d * m * 2 + 2 * sum(tm * wd * jnp.dtype(dt).itemsize for wd, dt in segments)
            + 2 * tm * d * 2 + 3 * tm * col_chunk * 4 + 2 * tm * d * 4)
    if rope is not None:
        tiles_per_seq = seq // tm
        for tab in rope:
            in_specs.append(pl.BlockSpec((tm, HEAD_DIM), lambda s: (prev(s) % tiles_per_seq, 0)))
            args.append(tab)
        vmem += 6 * tm * HEAD_DIM * 4
    if gate is not None:
        w1, w2f, w2b, bf, bb = gate
        dk = w2f.shape[1]
        in_specs += [
            pl.BlockSpec(w1.shape, lambda s: (0, 0)),
            pl.BlockSpec(w2f.shape, lambda s: (0, 0)),
            pl.BlockSpec(w2b.shape, lambda s: (0, 0)),
            pl.BlockSpec((1, dk), lambda s: (0, 0)),
            pl.BlockSpec((1, dk), lambda s: (0, 0)),
        ]
        args += [w1, w2f, w2b, bf.reshape(1, dk), bb.reshape(1, dk)]
        out_shape += [jax.ShapeDtypeStruct((n, dk), F32)] * 2
        out_specs += [pl.BlockSpec((tm, dk), lambda s: (prev(s), 0))] * 2
        vmem += 6 * tm * dk * 4 + 2 * (d * LANES * 2 + 2 * LANES * dk * 2)
    return pl.pallas_call(
        functools.partial(_proj_kernel, segments=tuple(segments), rope_cols=rope_cols, gated=gate is not None,
                          col_chunk=col_chunk),
        out_shape=out_shape,
        grid=(nrow + 1,),
        in_specs=in_specs,
        out_specs=out_specs,
        scratch_shapes=[pltpu.VMEM((tm, d), BF16), pltpu.VMEM((tm, d), BF16)],
        compiler_params=_params(("arbitrary",), vmem),
        name="proj",
    )(*args)


def _attend(qg, kall, vall, mask, sink_row):
    s = _dot_nt(kall, qg) * (HEAD_DIM ** -0.5)
    if mask is not None:
        s = jnp.where(mask, s, -jnp.inf)
    mx = jnp.maximum(jnp.max(s, axis=0, keepdims=True), sink_row)
    e = jnp.exp(s - mx)
    denom = jnp.sum(e, axis=0, keepdims=True) + jnp.exp(sink_row - mx)
    return _dot_tn(vall, e.astype(BF16)) / denom


def _sink_row(sink_ref, kv, rows):
    cols = [jnp.broadcast_to(sink_ref[0:1, kv * ATT_GROUP + h:kv * ATT_GROUP + h + 1], (1, rows))
            for h in range(ATT_GROUP)]
    return jnp.concatenate(cols, axis=1)


def _attn_kernel(*refs, band, tq, seq, sub):
    if band:
        (q_ref, kp_ref, kc_ref, kn_ref, vp_ref, vc_ref, vn_ref, kx_ref, vx_ref,
         x_ref, mod_ref, sink_ref, wo_ref, o_ref) = refs
    else:
        q_ref, kx_ref, vx_ref, x_ref, mod_ref, sink_ref, wo_ref, o_ref = refs
    blk = WINDOW
    nsub = tq // sub if band else 1
    rows = sub if band else tq
    if band:
        kwin = jnp.concatenate([kp_ref[...], kc_ref[...], kn_ref[...]], axis=0)
        vwin = jnp.concatenate([vp_ref[...], vc_ref[...], vn_ref[...]], axis=0)
        t0 = (pl.program_id(0) * tq) % seq
        nctx = kx_ref.shape[0]
        nwin = sub + 2 * blk
        ncol = nwin + nctx
        kj = lax.broadcasted_iota(jnp.int32, (ncol, ATT_GROUP * sub), 0)
        qi = lax.broadcasted_iota(jnp.int32, (ncol, ATT_GROUP * sub), 1) & (sub - 1)
        in_band = jnp.abs(qi - kj + blk) <= WINDOW
    outs = []
    for sb in range(nsub):
        heads = [None] * ATT_HEADS
        for kv in range(ATT_KV_HEADS):
            lo = kv * HEAD_DIM
            qg = jnp.concatenate(
                [q_ref[sb * rows:(sb + 1) * rows, (kv * ATT_GROUP + h) * HEAD_DIM:(kv * ATT_GROUP + h + 1) * HEAD_DIM]
                 for h in range(ATT_GROUP)], axis=0)
            if band:
                kall = jnp.concatenate([kwin[sb * sub:sb * sub + nwin, lo:lo + HEAD_DIM],
                                        kx_ref[:, lo:lo + HEAD_DIM]], axis=0)
                vall = jnp.concatenate([vwin[sb * sub:sb * sub + nwin, lo:lo + HEAD_DIM],
                                        vx_ref[:, lo:lo + HEAD_DIM]], axis=0)
                kpos = t0 + sb * sub - blk + kj
                mask = (kj >= nwin) | (in_band & (kpos >= 0) & (kpos < seq))
            else:
                kall = kx_ref[:, lo:lo + HEAD_DIM]
                vall = vx_ref[:, lo:lo + HEAD_DIM]
                mask = None
            og = _attend(qg, kall, vall, mask, _sink_row(sink_ref, kv, rows))
            for h in range(ATT_GROUP):
                heads[kv * ATT_GROUP + h] = og[:, h * rows:(h + 1) * rows].T.astype(BF16)
        outs.append(jnp.concatenate(heads, axis=1))
    o = outs[0] if len(outs) == 1 else jnp.concatenate(outs, axis=0)
    o_ref[...] = x_ref[...] + mod_ref[0, 2:3, :] * _dot(o, wo_ref[...])


def _attention(qkv, qkv_ctx, x, mod, sink, w_o, *, band, tq, seq, nctx, rows_per_mod):
    n, d = x.shape
    wo_stack, layer = w_o
    nq = ATT_HEADS * HEAD_DIM
    nkv = ATT_KV_HEADS * HEAD_DIM
    kcol, vcol = nq // nkv, nq // nkv + 1
    blk = WINDOW
    nblk = n // blk
    r = tq // blk
    q_spec = pl.BlockSpec((tq, nq), lambda i: (i, 0))
    tail_specs = [
        pl.BlockSpec((tq, d), lambda i: (i, 0)),
        pl.BlockSpec((1, MOD_ROWS, d), lambda i: ((i * tq) // rows_per_mod, 0, 0)),
        pl.BlockSpec((1, ATT_HEADS), lambda i: (0, 0)),
        pl.BlockSpec((None, nq, d), lambda i: (layer, 0, 0)),
    ]
    tail_args = [x, mod, sink.reshape(1, ATT_HEADS), wo_stack]
    if band:
        def ctx_idx(col):
            return lambda i: ((i * tq) // seq, col)
        in_specs = [
            q_spec,
            pl.BlockSpec((blk, nkv), lambda i: (jnp.maximum(i * r - 1, 0), kcol)),
            pl.BlockSpec((tq, nkv), lambda i: (i, kcol)),
            pl.BlockSpec((blk, nkv), lambda i: (jnp.minimum((i + 1) * r, nblk - 1), kcol)),
            pl.BlockSpec((blk, nkv), lambda i: (jnp.maximum(i * r - 1, 0), vcol)),
            pl.BlockSpec((tq, nkv), lambda i: (i, vcol)),
            pl.BlockSpec((blk, nkv), lambda i: (jnp.minimum((i + 1) * r, nblk - 1), vcol)),
            pl.BlockSpec((nctx, nkv), ctx_idx(kcol)),
            pl.BlockSpec((nctx, nkv), ctx_idx(vcol)),
        ] + tail_specs
        args = [qkv] * 7 + [qkv_ctx] * 2 + tail_args
        ncol = 3 * blk + nctx
    else:
        in_specs = [
            q_spec,
            pl.BlockSpec((tq, nkv), lambda i: (i, kcol)),
            pl.BlockSpec((tq, nkv), lambda i: (i, vcol)),
        ] + tail_specs
        args = [qkv] * 3 + tail_args
        ncol = tq
    vmem = (2 * (tq * nq * 2 + 4 * (tq + 2 * blk + nctx) * nkv * 2 + 2 * tq * d * 4 + nq * d * 2)
            + 6 * ATT_GROUP * blk * ncol * 4 + 2 * tq * d * 4)
    return pl.pallas_call(
        functools.partial(_attn_kernel, band=band, tq=tq, seq=seq, sub=min(tq, WINDOW)),
        out_shape=jax.ShapeDtypeStruct((n, d), F32),
        grid=(n // tq,),
        in_specs=in_specs,
        out_specs=pl.BlockSpec((tq, d), lambda i: (i, 0)),
        compiler_params=_params(("arbitrary",), vmem),
        name="attn_band" if band else "attn_ctx",
    )(*args)


def _gla_scan_kernel(qkf_ref, vf_ref, gf_ref, qkb_ref, vb_ref, gb_ref, s0f_ref, s0b_ref,
                     of_ref, ob_ref, sf_ref, sb_ref, *, chunk, dk, dv, q_scale):
    c = pl.program_id(1)

    @pl.when(c == 0)
    def _():
        sf_ref[...] = s0f_ref[...]
        sb_ref[...] = s0b_ref[...]

    nb = chunk // GLA_BLOCK
    levels = nb.bit_length() - 1
    row = lax.broadcasted_iota(jnp.int32, (chunk, chunk), 0)
    col = lax.broadcasted_iota(jnp.int32, (chunk, chunk), 1)
    blk_xor = (row // GLA_BLOCK) ^ (col // GLA_BLOCK)
    directions = (
        (qkf_ref, vf_ref, gf_ref, of_ref, sf_ref, False),
        (qkb_ref, vb_ref, gb_ref, ob_ref, sb_ref, True),
    )
    for qk_ref, v_ref, g_ref, o_ref, s_ref, reverse in directions:
        tri = (col >= row) if reverse else (col <= row)
        earlier = (col > row) if reverse else (col < row)
        tri_b = tri.astype(BF16)
        g = _log_sigmoid(g_ref[...]) / GLA_GATE_NORM
        g_hi = g.astype(BF16)
        g_lo = (g - g_hi.astype(F32)).astype(BF16)
        b = _dot(tri_b, g_hi) + _dot(tri_b, g_lo)

        hk = b.shape[1]
        end_row = 0 if reverse else GLA_BLOCK - 1
        ends = [b[i * GLA_BLOCK + end_row:i * GLA_BLOCK + end_row + 1] for i in range(nb)]
        zero = jnp.zeros((1, hk), F32)
        if reverse:
            starts = [ends[i + 1] if i + 1 < nb else zero for i in range(nb)]
        else:
            starts = [ends[i - 1] if i > 0 else zero for i in range(nb)]
        b_tot = ends[0] if reverse else ends[nb - 1]

        def group_first(i, size):
            return i | (size - 1) if reverse else i & ~(size - 1)

        def group_last(i, size):
            return i & ~(size - 1) if reverse else i | (size - 1)

        q = qk_ref[:, :hk] * q_scale
        k = qk_ref[:, hk:]
        v = v_ref[...]
        q_lv = [[] for _ in range(levels)]
        k_lv = [[] for _ in range(levels)]
        k_diag, q_state, k_state = [], [], []
        for i in range(nb):
            rs = slice(i * GLA_BLOCK, (i + 1) * GLA_BLOCK)
            q0 = q[rs] * jnp.exp(b[rs] - starts[i])
            k0 = k[rs] * jnp.exp(ends[i] - b[rs])
            k_diag.append(k0 * jnp.exp(starts[i] - ends[i]))
            for lv in range(levels):
                size = 1 << lv
                later = bool(i & size) != reverse
                q_lv[lv].append(q0 * jnp.exp(starts[i] - starts[group_first(i, size)]) if lv and later else q0)
                k_lv[lv].append(k0 * jnp.exp(ends[group_last(i, size)] - ends[i]) if lv and not later else k0)
            q_state.append(q0 * jnp.exp(starts[i]))
            k_state.append(k0 * jnp.exp(b_tot - ends[i]))

        def cat(parts):
            return jnp.concatenate(parts, axis=0).astype(BF16)

        q_lv, k_lv = [cat(p) for p in q_lv], [cat(p) for p in k_lv]
        k_diag, qs, ks = cat(k_diag), cat(q_state), cat(k_state)
        decay = jnp.exp(jnp.broadcast_to(b_tot, (LANES, hk))).T
        same_block = (blk_xor == 0) & tri
        pairs = [((blk_xor >> lv) == 1) & earlier for lv in range(levels)]

        for h in range(GLA_HEADS):
            kc = slice(h * dk, (h + 1) * dk)
            vc = slice(h * dv, (h + 1) * dv)
            a = jnp.where(same_block, _dot_nt(q_lv[0][:, kc], k_diag[:, kc]), 0.0)
            for lv in range(levels):
                a = jnp.where(pairs[lv], _dot_nt(q_lv[lv][:, kc], k_lv[lv][:, kc]), a)
            a = a.astype(BF16)
            st = s_ref[0, h]
            o_ref[:, vc] = (_dot(a, v[:, vc]) + _dot(qs[:, kc], st.astype(BF16))).astype(o_ref.dtype)
            upd = _dot_tn(ks[:, kc], v[:, vc])
            s_ref[0, h] = jnp.concatenate(
                [st[:, t * LANES:(t + 1) * LANES] * decay[kc] for t in range(dv // LANES)], axis=1) + upd


def _gla_scan(qk, v, g_f, g_b, s0_f, s0_b, *, batch, length, chunk, dk, dv):
    nc = length // chunk
    hk, hv = GLA_HEADS * dk, GLA_HEADS * dv

    def fwd(b, c):
        return b * nc + c

    def bwd(b, c):
        return b * nc + nc - 1 - c

    state_spec = pl.BlockSpec((1, GLA_HEADS, dk, dv), lambda b, c: (b, 0, 0, 0))
    in_specs = []
    for rows in (fwd, bwd):
        in_specs += [
            pl.BlockSpec((chunk, 2 * hk), lambda b, c, rows=rows: (rows(b, c), 0)),
            pl.BlockSpec((chunk, hv), lambda b, c, rows=rows: (rows(b, c), 0)),
            pl.BlockSpec((chunk, hk), lambda b, c, rows=rows: (rows(b, c), 0)),
        ]
    in_specs += [state_spec, state_spec]
    state_bytes = GLA_HEADS * dk * dv * 4
    vmem = (2 * 2 * (3 * chunk * hk * 4 + 2 * chunk * hv * 2) + 8 * state_bytes
            + 16 * chunk * hk * 4 + 4 * dk * dv * 4)
    return pl.pallas_call(
        functools.partial(_gla_scan_kernel, chunk=chunk, dk=dk, dv=dv, q_scale=dk ** -0.5),
        out_shape=[jax.ShapeDtypeStruct((batch * length, hv), BF16)] * 2
        + [jax.ShapeDtypeStruct(s0_f.shape, F32)] * 2,
        grid=(batch, nc),
        in_specs=in_specs,
        out_specs=[
            pl.BlockSpec((chunk, hv), lambda b, c: (fwd(b, c), 0)),
            pl.BlockSpec((chunk, hv), lambda b, c: (bwd(b, c), 0)),
            state_spec,
            state_spec,
        ],
        compiler_params=_params(("arbitrary", "arbitrary"), vmem),
        name="gla_scan",
    )(qk, v, g_f, qk, v, g_b, s0_f, s0_b)


def _gla_out_kernel(of_ref, ob_ref, r_ref, x_ref, mod_ref, g_ref, wo_ref, o_ref, *, dv):
    o = of_ref[...].astype(F32) + ob_ref[...].astype(F32)
    parts = []
    for h in range(o.shape[1] // dv):
        oh = o[:, h * dv:(h + 1) * dv]
        parts.append((oh * lax.rsqrt(jnp.mean(oh * oh, axis=-1, keepdims=True) + EPS)) * g_ref[...])
    gated = (jnp.concatenate(parts, axis=1) * _silu(r_ref[...].astype(F32))).astype(BF16)
    o_ref[...] = x_ref[...] + mod_ref[0, 2:3, :] * _dot(gated, wo_ref[...])


def _gla_output(o_f, o_b, r, x, mod, onorm_g, w_o, *, tm, rows_per_mod, dv):
    n, d = x.shape
    wo_stack, layer = w_o
    dvt = o_f.shape[1]
    vmem = 2 * (5 * tm * dvt * 4 + dvt * d * 2) + 4 * tm * dvt * 4
    return pl.pallas_call(
        functools.partial(_gla_out_kernel, dv=dv),
        out_shape=jax.ShapeDtypeStruct((n, d), F32),
        grid=(n // tm,),
        in_specs=[
            pl.BlockSpec((tm, dvt), lambda i: (i, 0)),
            pl.BlockSpec((tm, dvt), lambda i: (i, 0)),
            pl.BlockSpec((tm, dvt), lambda i: (i, 0)),
            pl.BlockSpec((tm, d), lambda i: (i, 0)),
            pl.BlockSpec((1, MOD_ROWS, d), lambda i: ((i * tm) // rows_per_mod, 0, 0)),
            pl.BlockSpec((1, dv), lambda i: (0, 0)),
            pl.BlockSpec((None, dvt, d), lambda i: (layer, 0, 0)),
        ],
        out_specs=pl.BlockSpec((tm, d), lambda i: (i, 0)),
        compiler_params=_params(("arbitrary",), vmem),
        name="gla_out",
    )(o_f, o_b, r, x, mod, onorm_g.reshape(1, dv), wo_stack)


def _ffn_kernel(xp_ref, x_ref, xn_ref, mod_ref, g_ref, fg_ref, wg_ref, wv_ref, cw_ref, cb_ref,
                wd_ref, o_ref, h_scr, ug_scr, uv_scr, act_scr, *, tm, seq, final_norm, nf):
    i = pl.program_id(0)
    j = pl.program_id(1)
    halo = SUBLANES
    rows = min(tm, seq, 256)
    assert tm <= seq or (rows == seq and tm % seq == 0)

    @pl.when(j == 0)
    def _():
        shift = mod_ref[0, 3:4, :]
        gs = g_ref[...] * (1.0 + mod_ref[0, 4:5, :])

        def norm_mod(x):
            return (x * lax.rsqrt(jnp.mean(x * x, axis=-1, keepdims=True) + EPS)) * gs + shift

        first = (i * tm) % seq == 0
        last = ((i + 1) * tm) % seq == 0
        h_scr[0:halo, :] = jnp.where(first, 0.0, norm_mod(xp_ref[...])).astype(BF16)
        for r0 in range(0, tm, rows):
            h_scr[halo + r0:halo + r0 + rows, :] = norm_mod(x_ref[r0:r0 + rows, :]).astype(BF16)
        h_scr[halo + tm:2 * halo + tm, :] = jnp.where(last, 0.0, norm_mod(xn_ref[...])).astype(BF16)
        o_ref[...] = jnp.zeros_like(o_ref)

    h = h_scr[...]
    ug_scr[...] = _dot(h, wg_ref[...])
    uv_scr[...] = _dot(h, wv_ref[...])

    tf = wg_ref.shape[1]
    gate_cols = pl.ds(pl.multiple_of(j * tf, tf), tf)
    val_cols = pl.ds(pl.multiple_of((nf + j) * tf, tf), tf)
    cwg, cbg = cw_ref[:, gate_cols], cb_ref[:, gate_cols]
    cwv, cbv = cw_ref[:, val_cols], cb_ref[:, val_cols]

    def conv(u_scr, cw, cb, r0):
        prev = u_scr[halo - 1 + r0:halo - 1 + r0 + rows, :]
        nxt = u_scr[halo + 1 + r0:halo + 1 + r0 + rows, :]
        if tm > seq:
            rid = lax.broadcasted_iota(jnp.int32, prev.shape, 0)
            prev = jnp.where(rid == 0, 0.0, prev)
            nxt = jnp.where(rid == rows - 1, 0.0, nxt)
        return cb + cw[0:1, :] * prev + cw[1:2, :] * u_scr[halo + r0:halo + r0 + rows, :] + cw[2:3, :] * nxt

    for r0 in range(0, tm, rows):
        act_scr[r0:r0 + rows, :] = (_silu(conv(ug_scr, cwg, cbg, r0)) * conv(uv_scr, cwv, cbv, r0)).astype(BF16)
    act = act_scr[...]
    for c0 in range(0, o_ref.shape[1], 512):
        o_ref[:, c0:c0 + 512] += _dot(act, wd_ref[:, c0:c0 + 512])

    @pl.when(j == pl.num_programs(1) - 1)
    def _():
        er = min(tm, 128)

        def residual(c, carry):
            sl = pl.ds(pl.multiple_of(c * er, er), er)
            y = x_ref[sl, :] + mod_ref[0, 5:6, :] * o_ref[sl, :]
            if final_norm:
                y = (y * lax.rsqrt(jnp.mean(y * y, axis=-1, keepdims=True) + EPS)) * fg_ref[...]
            o_ref[sl, :] = y
            return carry

        lax.fori_loop(0, tm // er, residual, 0)


def _conv_ffn(x, mod, g, w_up, conv_w, conv_b, w_down, *, tm, tf, seq, rows_per_mod, final_g=None):
    n, d = x.shape
    (up_stack, layer), (down_stack, _) = w_up, w_down
    dff = down_stack.shape[1]
    nf = dff // tf
    halo = SUBLANES
    r = tm // halo
    nhalo = n // halo
    conv_b = conv_b.reshape(1, 2 * dff)
    fg = (g if final_g is None else final_g).reshape(1, d)
    vmem = (2 * (2 * tm * d * 4 + 2 * d * tf * 2 + tf * d * 2) + (tm + 2 * halo) * d * 2
            + 2 * (tm + 2 * halo) * tf * 4 + tm * tf * 2 + 8 * min(tm, 256) * max(tf, d) * 4)
    return pl.pallas_call(
        functools.partial(_ffn_kernel, tm=tm, seq=seq, final_norm=final_g is not None, nf=nf),
        out_shape=jax.ShapeDtypeStruct((n, d), F32),
        grid=(n // tm, nf),
        in_specs=[
            pl.BlockSpec((halo, d), lambda i, j: (jnp.maximum(i * r - 1, 0), 0)),
            pl.BlockSpec((tm, d), lambda i, j: (i, 0)),
            pl.BlockSpec((halo, d), lambda i, j: (jnp.minimum((i + 1) * r, nhalo - 1), 0)),
            pl.BlockSpec((1, MOD_ROWS, d), lambda i, j: ((i * tm) // rows_per_mod, 0, 0)),
            pl.BlockSpec((1, d), lambda i, j: (0, 0)),
            pl.BlockSpec((1, d), lambda i, j: (0, 0)),
            pl.BlockSpec((None, d, tf), lambda i, j: (layer, 0, j)),
            pl.BlockSpec((None, d, tf), lambda i, j: (layer, 0, nf + j)),
            pl.BlockSpec((CONV_W, 2 * dff), lambda i, j: (0, 0)),
            pl.BlockSpec((1, 2 * dff), lambda i, j: (0, 0)),
            pl.BlockSpec((None, tf, d), lambda i, j: (layer, j, 0)),
        ],
        out_specs=pl.BlockSpec((tm, d), lambda i, j: (i, 0)),
        scratch_shapes=[
            pltpu.VMEM((tm + 2 * halo, d), BF16),
            pltpu.VMEM((tm + 2 * halo, tf), F32),
            pltpu.VMEM((tm + 2 * halo, tf), F32),
            pltpu.VMEM((tm, tf), BF16),
        ],
        compiler_params=_params(("arbitrary", "arbitrary"), vmem),
        name="conv_ffn",
    )(x, x, x, mod, g.reshape(1, d), fg, up_stack, up_stack, conv_w, conv_b, down_stack)


def _rope_tables(seq):
    half = HEAD_DIM // 2
    rows = seq // GRID_W
    row = jnp.repeat(jnp.arange(rows, dtype=F32), GRID_W)
    col = jnp.tile(jnp.arange(GRID_W, dtype=F32), rows)
    inv = ROPE_BASE ** (-jnp.arange(0, half, 2, dtype=F32) / half)
    ang_r = row[:, None] * inv[None, :]
    ang_c = col[:, None] * inv[None, :]
    zero = jnp.zeros_like(ang_r)
    cos = jnp.concatenate([jnp.cos(ang_r)] * 2 + [jnp.cos(ang_c)] * 2, axis=1)
    sa = jnp.concatenate([-jnp.sin(ang_r), zero, -jnp.sin(ang_c), zero], axis=1)
    sb = jnp.concatenate([zero, jnp.sin(ang_r), zero, jnp.sin(ang_c)], axis=1)
    return cos, sa, sb


def _tile(n, target):
    t = min(n, target)
    while n % t:
        t //= 2
    return t


def kernel(x, c, ctx, c_ctx, ada_w, ada_b, norm_mix_g, norm_ffn_g, ffn_w_up, ffn_conv_w, ffn_conv_b, ffn_w_down,
           attn_w_qkv, attn_sink, attn_w_o, gla_w_in, gla_gf_w1, gla_gf_w2, gla_gf_b, gla_gb_w1, gla_gb_w2,
           gla_gb_b, gla_onorm_g, gla_w_o, final_norm_g):
    batch, seq, d = x.shape
    nctx = ctx.shape[1]
    depth = ada_w.shape[0]
    n_lat, n_ctx = batch * seq, batch * nctx
    dk = gla_gf_w2.shape[2] // GLA_HEADS
    dv = gla_onorm_g.shape[1]
    nq = ATT_HEADS * HEAD_DIM
    assert seq % WINDOW == 0 and nctx % WINDOW == 0 and d % LANES == 0

    n_cond = batch + 1
    pad = (-n_cond) % SUBLANES
    cc = jnp.concatenate([c, c_ctx[None, :], jnp.zeros((pad, d), F32)], axis=0)
    mods = _modulations(cc, ada_w, ada_b)
    mods = mods.reshape(depth, n_cond + pad, N_MOD, d)
    mods = jnp.pad(mods, ((0, 0), (0, 0), (0, MOD_ROWS - N_MOD), (0, 0)))

    rope = _rope_tables(seq)
    xl = x.reshape(n_lat, d)
    xc = ctx.reshape(n_ctx, d)

    tm_l, tm_c = _tile(seq, 512), _tile(nctx, 512)
    tm_pc = _tile(n_ctx, 512)
    tm_fc = _tile(n_ctx, 1024) if nctx <= 256 and _tile(n_ctx, 1024) % nctx == 0 else tm_c
    tq_l = _tile(seq, 512)
    gla_chunk = _tile(nctx, 128)
    tf = 512
    nkv = ATT_KV_HEADS * HEAD_DIM
    hk, hv = GLA_HEADS * dk, GLA_HEADS * dv

    w_up_all, w_down_all = ffn_w_up.astype(BF16), ffn_w_down.astype(BF16)
    w_qkv_all, attn_wo_all = attn_w_qkv.astype(BF16), attn_w_o.astype(BF16)
    gla_qk_all, gla_vr_all = gla_w_in[:, :, :2 * hk].astype(BF16), gla_w_in[:, :, 2 * hk:].astype(BF16)
    gla_wo_all = gla_w_o.astype(BF16)

    for i in range(depth):
        last = i == depth - 1
        mod_l = mods[i, :batch]
        mod_c = mods[i, batch:batch + 1]
        j = i // N_MIXERS
        if i % N_MIXERS == 0:
            w_qkv, w_o = (w_qkv_all, j), (attn_wo_all, j)
            seg = [(nq + 2 * nkv, BF16)]
            qkv_l, = _project(xl, mod_l, norm_mix_g[i], w_qkv, seg, rows_per_mod=seq, tm=tm_l,
                              rope=rope, seq=seq, rope_cols=nq + nkv)
            qkv_c, = _project(xc, mod_c, norm_mix_g[i], w_qkv, seg, rows_per_mod=n_ctx, tm=tm_pc)
            xl = _attention(qkv_l, qkv_c, xl, mod_l, attn_sink[j], w_o, band=True, tq=tq_l, seq=seq, nctx=nctx,
                            rows_per_mod=seq)
            if not last:
                xc = _attention(qkv_c, None, xc, mod_c, attn_sink[j], w_o, band=False, tq=nctx, seq=nctx, nctx=nctx,
                                rows_per_mod=n_ctx)
        else:
            w_qk, w_vr, w_o = (gla_qk_all, j), (gla_vr_all, j), (gla_wo_all, j)
            rank = gla_gf_w1.shape[2]
            w1 = jnp.zeros((d, LANES), F32).at[:, :rank].set(gla_gf_w1[j]).at[:, rank:2 * rank].set(gla_gb_w1[j])
            w2f = jnp.zeros((LANES, hk), F32).at[:rank].set(gla_gf_w2[j])
            w2b = jnp.zeros((LANES, hk), F32).at[rank:2 * rank].set(gla_gb_w2[j])
            gate = (w1.astype(BF16), w2f.astype(BF16), w2b.astype(BF16), gla_gf_b[j], gla_gb_b[j])
            streams = []
            for xs, mod, rpm, tm in ((xc, mod_c, n_ctx, tm_pc), (xl, mod_l, seq, tm_l)):
                qk, gf, gb = _project(xs, mod, norm_mix_g[i], w_qk, [(2 * hk, F32)], rows_per_mod=rpm, tm=tm,
                                      gate=gate)
                v, r = _project(xs, mod, norm_mix_g[i], w_vr, [(hv, BF16), (hv, BF16)], rows_per_mod=rpm, tm=tm)
                streams.append((qk, v, r, gf, gb))
            s0 = jnp.zeros((batch, GLA_HEADS, dk, dv), F32)
            scan = functools.partial(_gla_scan, batch=batch, chunk=gla_chunk, dk=dk, dv=dv)
            (qk_c, v_c, r_c, gf_c, gb_c), (qk_l, v_l, r_l, gf_l, gb_l) = streams
            of_c, ob_c, sf, sb = scan(qk_c, v_c, gf_c, gb_c, s0, s0, length=nctx)
            of_l, ob_l, _, _ = scan(qk_l, v_l, gf_l, gb_l, sf, sb, length=seq)
            xl = _gla_output(of_l, ob_l, r_l, xl, mod_l, gla_onorm_g[j], w_o, tm=tm_l, rows_per_mod=seq, dv=dv)
            if not last:
                xc = _gla_output(of_c, ob_c, r_c, xc, mod_c, gla_onorm_g[j], w_o, tm=tm_c, rows_per_mod=n_ctx, dv=dv)
        w_up, w_down = (w_up_all, i), (w_down_all, i)
        xl = _conv_ffn(xl, mod_l, norm_ffn_g[i], w_up, ffn_conv_w[i], ffn_conv_b[i], w_down,
                       tm=_tile(seq, 1024), tf=tf, seq=seq, rows_per_mod=seq, final_g=final_norm_g if last else None)
        if not last:
            xc = _conv_ffn(xc, mod_c, norm_ffn_g[i], w_up, ffn_conv_w[i], ffn_conv_b[i], w_down,
                           tm=tm_fc, tf=tf, seq=nctx, rows_per_mod=n_ctx)
    return xl.reshape(batch, seq, d)
```

```python
import functools

import jax
import jax.numpy as jnp
from jax import lax
from jax.experimental import pallas as pl
from jax.experimental.pallas import tpu as pltpu

GRID_W = 64
N_MIXERS = 2
ATT_HEADS = 16
ATT_KV_HEADS = 4
ATT_GROUP = ATT_HEADS // ATT_KV_HEADS
HEAD_DIM = 128
WINDOW = 128
ROPE_BASE = 10000.0
GLA_HEADS = 4
GLA_GATE_NORM = 16.0
GLA_BLOCK = 16
CONV_W = 3
EPS = 1e-6
N_MOD = 6
MOD_ROWS = 8

LANES = 128
SUBLANES = 8
VMEM_LIMIT_CAP = 60 * 1024 * 1024

BF16 = jnp.bfloat16
F32 = jnp.float32


def _dot(a, b):
    return jnp.dot(a, b, preferred_element_type=F32)


def _dot_nt(a, b):
    return lax.dot_general(a, b, (((1,), (1,)), ((), ())), preferred_element_type=F32)


def _dot_tn(a, b):
    return lax.dot_general(a, b, (((0,), (0,)), ((), ())), preferred_element_type=F32)


def _params(semantics, vmem_bytes):
    limit = min(VMEM_LIMIT_CAP, int(vmem_bytes * 1.25) + (4 << 20))
    return pltpu.CompilerParams(dimension_semantics=semantics, vmem_limit_bytes=limit)


def _sigmoid(x):
    return 1.0 / (1.0 + jnp.exp(-x))


def _silu(x):
    return x * _sigmoid(x)


def _mod_kernel(c_ref, w_ref, b_ref, o_ref):
    s = _silu(c_ref[...]).astype(BF16)
    o_ref[0] = _dot(s, w_ref[0].astype(BF16)) + b_ref[0]


def _modulations(cc, ada_w, ada_b):
    depth, d, n = ada_w.shape
    r = cc.shape[0]
    tn = 1024
    vmem = 2 * (d * tn * 4 + r * tn * 4 + tn * 4) + r * d * 4 + d * tn * 2
    return pl.pallas_call(
        _mod_kernel,
        out_shape=jax.ShapeDtypeStruct((depth, r, n), F32),
        grid=(depth, n // tn),
        in_specs=[
            pl.BlockSpec((r, d), lambda l, j: (0, 0)),
            pl.BlockSpec((1, d, tn), lambda l, j: (l, 0, j)),
            pl.BlockSpec((1, 1, tn), lambda l, j: (l, 0, j)),
        ],
        out_specs=pl.BlockSpec((1, r, tn), lambda l, j: (l, 0, j)),
        compiler_params=_params(("arbitrary", "arbitrary"), vmem),
        name="adaln_mod",
    )(cc, ada_w, ada_b.reshape(depth, 1, n))


def _log_sigmoid(z):
    return jnp.minimum(z, 0.0) - jnp.log(1.0 + jnp.exp(-jnp.abs(z)))


def _proj_kernel(*refs, segments, rope_cols, gated, col_chunk):
    n_out = len(segments) + (2 if gated else 0)
    x_ref, mod_ref, g_ref, w_ref = refs[:4]
    extra = refs[4:len(refs) - n_out - 2]
    outs = refs[len(refs) - n_out - 2:len(refs) - 2]
    h_even, h_odd = refs[-2:]
    s = pl.program_id(0)

    @pl.when(s == 0)
    def _():
        h_odd[...] = jnp.zeros_like(h_odd)

    def body(h_new, h_old):
        h = h_old[...]
        if gated:
            w1_ref, w2f_ref, w2b_ref, bf_ref, bb_ref = extra
            t = _dot(h, w1_ref[...]).astype(BF16)
            for w2_ref, b_ref, o_ref in ((w2f_ref, bf_ref, outs[-2]), (w2b_ref, bb_ref, outs[-1])):
                o_ref[...] = _dot(t, w2_ref[...]) + b_ref[...]
        if rope_cols:
            cos, sa, sb = extra[0][...], extra[1][...], extra[2][...]
        col = 0
        for (width, _), y_ref in zip(segments, outs):
            for c0 in range(0, width, col_chunk):
                y = _dot(h, w_ref[:, col + c0:col + c0 + col_chunk])
                if col + c0 < rope_cols:
                    parts = []
                    for p in range(col_chunk // HEAD_DIM):
                        yh = y[:, p * HEAD_DIM:(p + 1) * HEAD_DIM]
                        if col + c0 + p * HEAD_DIM < rope_cols:
                            up = pltpu.roll(yh, HEAD_DIM - HEAD_DIM // 4, axis=1)
                            dn = pltpu.roll(yh, HEAD_DIM // 4, axis=1)
                            yh = yh * cos + up * sa + dn * sb
                        parts.append(yh)
                    y = jnp.concatenate(parts, axis=1)
                y_ref[:, c0:c0 + col_chunk] = y.astype(y_ref.dtype)
            col += width
        x = x_ref[...]
        gs = g_ref[...] * (1.0 + mod_ref[0, 1:2, :])
        r = lax.rsqrt(jnp.mean(x * x, axis=-1, keepdims=True) + EPS)
        h_new[...] = ((x * r) * gs + mod_ref[0, 0:1, :]).astype(BF16)

    @pl.when(s % 2 == 0)
    def _():
        body(h_even, h_odd)

    @pl.when(s % 2 == 1)
    def _():
        body(h_odd, h_even)


def _project(x, mod, g, w, segments, *, rows_per_mod, tm, rope=None, seq=None, rope_cols=0, gate=None):
    n, d = x.shape
    w_stack, layer = w
    m = w_stack.shape[2]
    nrow = n // tm
    col_chunk = 1024 if all(wd % 1024 == 0 for wd, _ in segments) else 512

    def cur(s):
        return jnp.minimum(s, nrow - 1)

    def prev(s):
        return jnp.maximum(s - 1, 0)

    in_specs = [
        pl.BlockSpec((tm, d), lambda s: (cur(s), 0)),
        pl.BlockSpec((1, MOD_ROWS, d), lambda s: ((cur(s) * tm) // rows_per_mod, 0, 0)),
        pl.BlockSpec((1, d), lambda s: (0, 0)),
        pl.BlockSpec((None, d, m), lambda s: (layer, 0, 0), pipeline_mode=pl.Buffered(1)),
    ]
    args = [x, mod, g.reshape(1, d), w_stack]
    out_shape = [jax.ShapeDtypeStruct((n, wd), dt) for wd, dt in segments]
    out_specs = [pl.BlockSpec((tm, wd), lambda s: (prev(s), 0)) for wd, _ in segments]
    vmem = (2 * tm * d * 4 + d * m * 2 + 2 * sum(tm * wd * jnp.dtype(dt).itemsize for wd, dt in segments)
            + 2 * tm * d * 2 + 3 * tm * col_chunk * 4 + 2 * tm * d * 4)
    if rope is not None:
        tiles_per_seq = seq // tm
        for tab in rope:
            in_specs.append(pl.BlockSpec((tm, HEAD_DIM), lambda s: (prev(s) % tiles_per_seq, 0)))
            args.append(tab)
        vmem += 6 * tm * HEAD_DIM * 4
    if gate is not None:
        w1, w2f, w2b, bf, bb = gate
        dk = w2f.shape[1]
        in_specs += [
            pl.BlockSpec(w1.shape, lambda s: (0, 0)),
            pl.BlockSpec(w2f.shape, lambda s: (0, 0)),
            pl.BlockSpec(w2b.shape, lambda s: (0, 0)),
            pl.BlockSpec((1, dk), lambda s: (0, 0)),
            pl.BlockSpec((1, dk), lambda s: (0, 0)),
        ]
        args += [w1, w2f, w2b, bf.reshape(1, dk), bb.reshape(1, dk)]
        out_shape += [jax.ShapeDtypeStruct((n, dk), F32)] * 2
        out_specs += [pl.BlockSpec((tm, dk), lambda s: (prev(s), 0))] * 2
        vmem += 6 * tm * dk * 4 + 2 * (d * LANES * 2 + 2 * LANES * dk * 2)
    return pl.pallas_call(
        functools.partial(_proj_kernel, segments=tuple(segments), rope_cols=rope_cols, gated=gate is not None,
                          col_chunk=col_chunk),
        out_shape=out_shape,
        grid=(nrow + 1,),
        in_specs=in_specs,
        out_specs=out_specs,
        scratch_shapes=[pltpu.VMEM((tm, d), BF16), pltpu.VMEM((tm, d), BF16)],
        compiler_params=_params(("arbitrary",), vmem),
        name="proj",
    )(*args)


def _attend(qg, kall, vall, mask, sink_row):
    s = _dot_nt(kall, qg) * (HEAD_DIM ** -0.5)
    if mask is not None:
        s = jnp.where(mask, s, -jnp.inf)
    mx = jnp.maximum(jnp.max(s, axis=0, keepdims=True), sink_row)
    e = jnp.exp(s - mx)
    denom = jnp.sum(e, axis=0, keepdims=True) + jnp.exp(sink_row - mx)
    return _dot_tn(vall, e.astype(BF16)) / denom


def _sink_row(sink_ref, kv, rows):
    cols = [jnp.broadcast_to(sink_ref[0:1, kv * ATT_GROUP + h:kv * ATT_GROUP + h + 1], (1, rows))
            for h in range(ATT_GROUP)]
    return jnp.concatenate(cols, axis=1)


def _attn_kernel(*refs, band, tq, seq, sub):
    if band:
        (q_ref, kp_ref, kc_ref, kn_ref, vp_ref, vc_ref, vn_ref, kx_ref, vx_ref,
         x_ref, mod_ref, sink_ref, wo_ref, o_ref) = refs
    else:
        q_ref, kx_ref, vx_ref, x_ref, mod_ref, sink_ref, wo_ref, o_ref = refs
    blk = WINDOW
    nsub = tq // sub if band else 1
    rows = sub if band else tq
    if band:
        kwin = jnp.concatenate([kp_ref[...], kc_ref[...], kn_ref[...]], axis=0)
        vwin = jnp.concatenate([vp_ref[...], vc_ref[...], vn_ref[...]], axis=0)
        t0 = (pl.program_id(0) * tq) % seq
        nctx = kx_ref.shape[0]
        nwin = sub + 2 * blk
        ncol = nwin + nctx
        kj = lax.broadcasted_iota(jnp.int32, (ncol, ATT_GROUP * sub), 0)
        qi = lax.broadcasted_iota(jnp.int32, (ncol, ATT_GROUP * sub), 1) & (sub - 1)
        in_band = jnp.abs(qi - kj + blk) <= WINDOW
    outs = []
    for sb in range(nsub):
        heads = [None] * ATT_HEADS
        for kv in range(ATT_KV_HEADS):
            lo = kv * HEAD_DIM
            qg = jnp.concatenate(
                [q_ref[sb * rows:(sb + 1) * rows, (kv * ATT_GROUP + h) * HEAD_DIM:(kv * ATT_GROUP + h + 1) * HEAD_DIM]
                 for h in range(ATT_GROUP)], axis=0)
            if band:
                kall = jnp.concatenate([kwin[sb * sub:sb * sub + nwin, lo:lo + HEAD_DIM],
                                        kx_ref[:, lo:lo + HEAD_DIM]], axis=0)
                vall = jnp.concatenate([vwin[sb * sub:sb * sub + nwin, lo:lo + HEAD_DIM],
                                        vx_ref[:, lo:lo + HEAD_DIM]], axis=0)
                kpos = t0 + sb * sub - blk + kj
                mask = (kj >= nwin) | (in_band & (kpos >= 0) & (kpos < seq))
            else:
                kall = kx_ref[:, lo:lo + HEAD_DIM]
                vall = vx_ref[:, lo:lo + HEAD_DIM]
                mask = None
            og = _attend(qg, kall, vall, mask, _sink_row(sink_ref, kv, rows))
            for h in range(ATT_GROUP):
                heads[kv * ATT_GROUP + h] = og[:, h * rows:(h + 1) * rows].T.astype(BF16)
        outs.append(jnp.concatenate(heads, axis=1))
    o = outs[0] if len(outs) == 1 else jnp.concatenate(outs, axis=0)
    o_ref[...] = x_ref[...] + mod_ref[0, 2:3, :] * _dot(o, wo_ref[...])


def _attention(qkv, qkv_ctx, x, mod, sink, w_o, *, band, tq, seq, nctx, rows_per_mod):
    n, d = x.shape
    wo_stack, layer = w_o
    nq = ATT_HEADS * HEAD_DIM
    nkv = ATT_KV_HEADS * HEAD_DIM
    kcol, vcol = nq // nkv, nq // nkv + 1
    blk = WINDOW
    nblk = n // blk
    r = tq // blk
    q_spec = pl.BlockSpec((tq, nq), lambda i: (i, 0))
    tail_specs = [
        pl.BlockSpec((tq, d), lambda i: (i, 0)),
        pl.BlockSpec((1, MOD_ROWS, d), lambda i: ((i * tq) // rows_per_mod, 0, 0)),
        pl.BlockSpec((1, ATT_HEADS), lambda i: (0, 0)),
        pl.BlockSpec((None, nq, d), lambda i: (layer, 0, 0)),
    ]
    tail_args = [x, mod, sink.reshape(1, ATT_HEADS), wo_stack]
    if band:
        def ctx_idx(col):
            return lambda i: ((i * tq) // seq, col)
        in_specs = [
            q_spec,
            pl.BlockSpec((blk, nkv), lambda i: (jnp.maximum(i * r - 1, 0), kcol)),
            pl.BlockSpec((tq, nkv), lambda i: (i, kcol)),
            pl.BlockSpec((blk, nkv), lambda i: (jnp.minimum((i + 1) * r, nblk - 1), kcol)),
            pl.BlockSpec((blk, nkv), lambda i: (jnp.maximum(i * r - 1, 0), vcol)),
            pl.BlockSpec((tq, nkv), lambda i: (i, vcol)),
            pl.BlockSpec((blk, nkv), lambda i: (jnp.minimum((i + 1) * r, nblk - 1), vcol)),
            pl.BlockSpec((nctx, nkv), ctx_idx(kcol)),
            pl.BlockSpec((nctx, nkv), ctx_idx(vcol)),
        ] + tail_specs
        args = [qkv] * 7 + [qkv_ctx] * 2 + tail_args
        ncol = 3 * blk + nctx
    else:
        in_specs = [
            q_spec,
            pl.BlockSpec((tq, nkv), lambda i: (i, kcol)),
            pl.BlockSpec((tq, nkv), lambda i: (i, vcol)),
        ] + tail_specs
        args = [qkv] * 3 + tail_args
        ncol = tq
    vmem = (2 * (tq * nq * 2 + 4 * (tq + 2 * blk + nctx) * nkv * 2 + 2 * tq * d * 4 + nq * d * 2)
            + 6 * ATT_GROUP * blk * ncol * 4 + 2 * tq * d * 4)
    return pl.pallas_call(
        functools.partial(_attn_kernel, band=band, tq=tq, seq=seq, sub=min(tq, WINDOW)),
        out_shape=jax.ShapeDtypeStruct((n, d), F32),
        grid=(n // tq,),
        in_specs=in_specs,
        out_specs=pl.BlockSpec((tq, d), lambda i: (i, 0)),
        compiler_params=_params(("arbitrary",), vmem),
        name="attn_band" if band else "attn_ctx",
    )(*args)


def _gla_scan_kernel(qkf_ref, vf_ref, gf_ref, qkb_ref, vb_ref, gb_ref, s0f_ref, s0b_ref,
                     of_ref, ob_ref, sf_ref, sb_ref, *, chunk, dk, dv, q_scale):
    c = pl.program_id(1)

    @pl.when(c == 0)
    def _():
        sf_ref[...] = s0f_ref[...]
        sb_ref[...] = s0b_ref[...]

    nb = chunk // GLA_BLOCK
    levels = nb.bit_length() - 1
    row = lax.broadcasted_iota(jnp.int32, (chunk, chunk), 0)
    col = lax.broadcasted_iota(jnp.int32, (chunk, chunk), 1)
    blk_xor = (row // GLA_BLOCK) ^ (col // GLA_BLOCK)
    directions = (
        (qkf_ref, vf_ref, gf_ref, of_ref, sf_ref, False),
        (qkb_ref, vb_ref, gb_ref, ob_ref, sb_ref, True),
    )
    for qk_ref, v_ref, g_ref, o_ref, s_ref, reverse in directions:
        tri = (col >= row) if reverse else (col <= row)
        earlier = (col > row) if reverse else (col < row)
        tri_b = tri.astype(BF16)
        g = _log_sigmoid(g_ref[...]) / GLA_GATE_NORM
        g_hi = g.astype(BF16)
        g_lo = (g - g_hi.astype(F32)).astype(BF16)
        b = _dot(tri_b, g_hi) + _dot(tri_b, g_lo)

        hk = b.shape[1]
        end_row = 0 if reverse else GLA_BLOCK - 1
        ends = [b[i * GLA_BLOCK + end_row:i * GLA_BLOCK + end_row + 1] for i in range(nb)]
        zero = jnp.zeros((1, hk), F32)
        if reverse:
            starts = [ends[i + 1] if i + 1 < nb else zero for i in range(nb)]
        else:
            starts = [ends[i - 1] if i > 0 else zero for i in range(nb)]
        b_tot = ends[0] if reverse else ends[nb - 1]

        def group_first(i, size):
            return i | (size - 1) if reverse else i & ~(size - 1)

        def group_last(i, size):
            return i & ~(size - 1) if reverse else i | (size - 1)

        q = qk_ref[:, :hk] * q_scale
        k = qk_ref[:, hk:]
        v = v_ref[...]
        q_lv = [[] for _ in range(levels)]
        k_lv = [[] for _ in range(levels)]
        k_diag, q_state, k_state = [], [], []
        for i in range(nb):
            rs = slice(i * GLA_BLOCK, (i + 1) * GLA_BLOCK)
            q0 = q[rs] * jnp.exp(b[rs] - starts[i])
            k0 = k[rs] * jnp.exp(ends[i] - b[rs])
            k_diag.append(k0 * jnp.exp(starts[i] - ends[i]))
            for lv in range(levels):
                size = 1 << lv
                later = bool(i & size) != reverse
                q_lv[lv].append(q0 * jnp.exp(starts[i] - starts[group_first(i, size)]) if lv and later else q0)
                k_lv[lv].append(k0 * jnp.exp(ends[group_last(i, size)] - ends[i]) if lv and not later else k0)
            q_state.append(q0 * jnp.exp(starts[i]))
            k_state.append(k0 * jnp.exp(b_tot - ends[i]))

        def cat(parts):
            return jnp.concatenate(parts, axis=0).astype(BF16)

        q_lv, k_lv = [cat(p) for p in q_lv], [cat(p) for p in k_lv]
        k_diag, qs, ks = cat(k_diag), cat(q_state), cat(k_state)
        decay = jnp.exp(jnp.broadcast_to(b_tot, (LANES, hk))).T
        same_block = (blk_xor == 0) & tri
        pairs = [((blk_xor >> lv) == 1) & earlier for lv in range(levels)]

        for h in range(GLA_HEADS):
            kc = slice(h * dk, (h + 1) * dk)
            vc = slice(h * dv, (h + 1) * dv)
            a = jnp.where(same_block, _dot_nt(q_lv[0][:, kc], k_diag[:, kc]), 0.0)
            for lv in range(levels):
                a = jnp.where(pairs[lv], _dot_nt(q_lv[lv][:, kc], k_lv[lv][:, kc]), a)
            a = a.astype(BF16)
            st = s_ref[0, h]
            o_ref[:, vc] = (_dot(a, v[:, vc]) + _dot(qs[:, kc], st.astype(BF16))).astype(o_ref.dtype)
            upd = _dot_tn(ks[:, kc], v[:, vc])
            s_ref[0, h] = jnp.concatenate(
                [st[:, t * LANES:(t + 1) * LANES] * decay[kc] for t in range(dv // LANES)], axis=1) + upd


def _gla_scan(qk, v, g_f, g_b, s0_f, s0_b, *, batch, length, chunk, dk, dv):
    nc = length // chunk
    hk, hv = GLA_HEADS * dk, GLA_HEADS * dv

    def fwd(b, c):
        return b * nc + c

    def bwd(b, c):
        return b * nc + nc - 1 - c

    state_spec = pl.BlockSpec((1, GLA_HEADS, dk, dv), lambda b, c: (b, 0, 0, 0))
    in_specs = []
    for rows in (fwd, bwd):
        in_specs += [
            pl.BlockSpec((chunk, 2 * hk), lambda b, c, rows=rows: (rows(b, c), 0)),
            pl.BlockSpec((chunk, hv), lambda b, c, rows=rows: (rows(b, c), 0)),
            pl.BlockSpec((chunk, hk), lambda b, c, rows=rows: (rows(b, c), 0)),
        ]
    in_specs += [state_spec, state_spec]
    state_bytes = GLA_HEADS * dk * dv * 4
    vmem = (2 * 2 * (3 * chunk * hk * 4 + 2 * chunk * hv * 2) + 8 * state_bytes
            + 16 * chunk * hk * 4 + 4 * dk * dv * 4)
    return pl.pallas_call(
        functools.partial(_gla_scan_kernel, chunk=chunk, dk=dk, dv=dv, q_scale=dk ** -0.5),
        out_shape=[jax.ShapeDtypeStruct((batch * length, hv), BF16)] * 2
        + [jax.ShapeDtypeStruct(s0_f.shape, F32)] * 2,
        grid=(batch, nc),
        in_specs=in_specs,
        out_specs=[
            pl.BlockSpec((chunk, hv), lambda b, c: (fwd(b, c), 0)),
            pl.BlockSpec((chunk, hv), lambda b, c: (bwd(b, c), 0)),
            state_spec,
            state_spec,
        ],
        compiler_params=_params(("arbitrary", "arbitrary"), vmem),
        name="gla_scan",
    )(qk, v, g_f, qk, v, g_b, s0_f, s0_b)


def _gla_out_kernel(of_ref, ob_ref, r_ref, x_ref, mod_ref, g_ref, wo_ref, o_ref, *, dv):
    o = of_ref[...].astype(F32) + ob_ref[...].astype(F32)
    parts = []
    for h in range(o.shape[1] // dv):
        oh = o[:, h * dv:(h + 1) * dv]
        parts.append((oh * lax.rsqrt(jnp.mean(oh * oh, axis=-1, keepdims=True) + EPS)) * g_ref[...])
    gated = (jnp.concatenate(parts, axis=1) * _silu(r_ref[...].astype(F32))).astype(BF16)
    o_ref[...] = x_ref[...] + mod_ref[0, 2:3, :] * _dot(gated, wo_ref[...])


def _gla_output(o_f, o_b, r, x, mod, onorm_g, w_o, *, tm, rows_per_mod, dv):
    n, d = x.shape
    wo_stack, layer = w_o
    dvt = o_f.shape[1]
    vmem = 2 * (5 * tm * dvt * 4 + dvt * d * 2) + 4 * tm * dvt * 4
    return pl.pallas_call(
        functools.partial(_gla_out_kernel, dv=dv),
        out_shape=jax.ShapeDtypeStruct((n, d), F32),
        grid=(n // tm,),
        in_specs=[
            pl.BlockSpec((tm, dvt), lambda i: (i, 0)),
            pl.BlockSpec((tm, dvt), lambda i: (i, 0)),
            pl.BlockSpec((tm, dvt), lambda i: (i, 0)),
            pl.BlockSpec((tm, d), lambda i: (i, 0)),
            pl.BlockSpec((1, MOD_ROWS, d), lambda i: ((i * tm) // rows_per_mod, 0, 0)),
            pl.BlockSpec((1, dv), lambda i: (0, 0)),
            pl.BlockSpec((None, dvt, d), lambda i: (layer, 0, 0)),
        ],
        out_specs=pl.BlockSpec((tm, d), lambda i: (i, 0)),
        compiler_params=_params(("arbitrary",), vmem),
        name="gla_out",
    )(o_f, o_b, r, x, mod, onorm_g.reshape(1, dv), wo_stack)


def _ffn_kernel(xp_ref, x_ref, xn_ref, mod_ref, g_ref, fg_ref, wg_ref, wv_ref, cw_ref, cb_ref,
                wd_ref, o_ref, h_scr, ug_scr, uv_scr, act_scr, *, tm, seq, final_norm, nf):
    i = pl.program_id(0)
    j = pl.program_id(1)
    halo = SUBLANES
    rows = min(tm, seq, 256)
    assert tm <= seq or (rows == seq and tm % seq == 0)

    @pl.when(j == 0)
    def _():
        shift = mod_ref[0, 3:4, :]
        gs = g_ref[...] * (1.0 + mod_ref[0, 4:5, :])

        def norm_mod(x):
            return (x * lax.rsqrt(jnp.mean(x * x, axis=-1, keepdims=True) + EPS)) * gs + shift

        first = (i * tm) % seq == 0
        last = ((i + 1) * tm) % seq == 0
        h_scr[0:halo, :] = jnp.where(first, 0.0, norm_mod(xp_ref[...])).astype(BF16)
        for r0 in range(0, tm, rows):
            h_scr[halo + r0:halo + r0 + rows, :] = norm_mod(x_ref[r0:r0 + rows, :]).astype(BF16)
        h_scr[halo + tm:2 * halo + tm, :] = jnp.where(last, 0.0, norm_mod(xn_ref[...])).astype(BF16)
        o_ref[...] = jnp.zeros_like(o_ref)

    h = h_scr[...]
    ug_scr[...] = _dot(h, wg_ref[...])
    uv_scr[...] = _dot(h, wv_ref[...])

    tf = wg_ref.shape[1]
    gate_cols = pl.ds(pl.multiple_of(j * tf, tf), tf)
    val_cols = pl.ds(pl.multiple_of((nf + j) * tf, tf), tf)
    cwg, cbg = cw_ref[:, gate_cols], cb_ref[:, gate_cols]
    cwv, cbv = cw_ref[:, val_cols], cb_ref[:, val_cols]

    def conv(u_scr, cw, cb, r0):
        prev = u_scr[halo - 1 + r0:halo - 1 + r0 + rows, :]
        nxt = u_scr[halo + 1 + r0:halo + 1 + r0 + rows, :]
        if tm > seq:
            rid = lax.broadcasted_iota(jnp.int32, prev.shape, 0)
            prev = jnp.where(rid == 0, 0.0, prev)
            nxt = jnp.where(rid == rows - 1, 0.0, nxt)
        return cb + cw[0:1, :] * prev + cw[1:2, :] * u_scr[halo + r0:halo + r0 + rows, :] + cw[2:3, :] * nxt

    for r0 in range(0, tm, rows):
        act_scr[r0:r0 + rows, :] = (_silu(conv(ug_scr, cwg, cbg, r0)) * conv(uv_scr, cwv, cbv, r0)).astype(BF16)
    act = act_scr[...]
    for c0 in range(0, o_ref.shape[1], 512):
        o_ref[:, c0:c0 + 512] += _dot(act, wd_ref[:, c0:c0 + 512])

    @pl.when(j == pl.num_programs(1) - 1)
    def _():
        er = min(tm, 128)

        def residual(c, carry):
            sl = pl.ds(pl.multiple_of(c * er, er), er)
            y = x_ref[sl, :] + mod_ref[0, 5:6, :] * o_ref[sl, :]
            if final_norm:
                y = (y * lax.rsqrt(jnp.mean(y * y, axis=-1, keepdims=True) + EPS)) * fg_ref[...]
            o_ref[sl, :] = y
            return carry

        lax.fori_loop(0, tm // er, residual, 0)


def _conv_ffn(x, mod, g, w_up, conv_w, conv_b, w_down, *, tm, tf, seq, rows_per_mod, final_g=None):
    n, d = x.shape
    (up_stack, layer), (down_stack, _) = w_up, w_down
    dff = down_stack.shape[1]
    nf = dff // tf
    halo = SUBLANES
    r = tm // halo
    nhalo = n // halo
    conv_b = conv_b.reshape(1, 2 * dff)
    fg = (g if final_g is None else final_g).reshape(1, d)
    vmem = (2 * (2 * tm * d * 4 + 2 * d * tf * 2 + tf * d * 2) + (tm + 2 * halo) * d * 2
            + 2 * (tm + 2 * halo) * tf * 4 + tm * tf * 2 + 8 * min(tm, 256) * max(tf, d) * 4)
    return pl.pallas_call(
        functools.partial(_ffn_kernel, tm=tm, seq=seq, final_norm=final_g is not None, nf=nf),
        out_shape=jax.ShapeDtypeStruct((n, d), F32),
        grid=(n // tm, nf),
        in_specs=[
            pl.BlockSpec((halo, d), lambda i, j: (jnp.maximum(i * r - 1, 0), 0)),
            pl.BlockSpec((tm, d), lambda i, j: (i, 0)),
            pl.BlockSpec((halo, d), lambda i, j: (jnp.minimum((i + 1) * r, nhalo - 1), 0)),
            pl.BlockSpec((1, MOD_ROWS, d), lambda i, j: ((i * tm) // rows_per_mod, 0, 0)),
            pl.BlockSpec((1, d), lambda i, j: (0, 0)),
            pl.BlockSpec((1, d), lambda i, j: (0, 0)),
            pl.BlockSpec((None, d, tf), lambda i, j: (layer, 0, j)),
            pl.BlockSpec((None, d, tf), lambda i, j: (layer, 0, nf + j)),
            pl.BlockSpec((CONV_W, 2 * dff), lambda i, j: (0, 0)),
            pl.BlockSpec((1, 2 * dff), lambda i, j: (0, 0)),
            pl.BlockSpec((None, tf, d), lambda i, j: (layer, j, 0)),
        ],
        out_specs=pl.BlockSpec((tm, d), lambda i, j: (i, 0)),
        scratch_shapes=[
            pltpu.VMEM((tm + 2 * halo, d), BF16),
            pltpu.VMEM((tm + 2 * halo, tf), F32),
            pltpu.VMEM((tm + 2 * halo, tf), F32),
            pltpu.VMEM((tm, tf), BF16),
        ],
        compiler_params=_params(("arbitrary", "arbitrary"), vmem),
        name="conv_ffn",
    )(x, x, x, mod, g.reshape(1, d), fg, up_stack, up_stack, conv_w, conv_b, down_stack)


def _rope_tables(seq):
    half = HEAD_DIM // 2
    rows = seq // GRID_W
    row = jnp.repeat(jnp.arange(rows, dtype=F32), GRID_W)
    col = jnp.tile(jnp.arange(GRID_W, dtype=F32), rows)
    inv = ROPE_BASE ** (-jnp.arange(0, half, 2, dtype=F32) / half)
    ang_r = row[:, None] * inv[None, :]
    ang_c = col[:, None] * inv[None, :]
    zero = jnp.zeros_like(ang_r)
    cos = jnp.concatenate([jnp.cos(ang_r)] * 2 + [jnp.cos(ang_c)] * 2, axis=1)
    sa = jnp.concatenate([-jnp.sin(ang_r), zero, -jnp.sin(ang_c), zero], axis=1)
    sb = jnp.concatenate([zero, jnp.sin(ang_r), zero, jnp.sin(ang_c)], axis=1)
    return cos, sa, sb


def _tile(n, target):
    t = min(n, target)
    while n % t:
        t //= 2
    return t


def kernel(x, c, ctx, c_ctx, ada_w, ada_b, norm_mix_g, norm_ffn_g, ffn_w_up, ffn_conv_w, ffn_conv_b, ffn_w_down,
           attn_w_qkv, attn_sink, attn_w_o, gla_w_in, gla_gf_w1, gla_gf_w2, gla_gf_b, gla_gb_w1, gla_gb_w2,
           gla_gb_b, gla_onorm_g, gla_w_o, final_norm_g):
    batch, seq, d = x.shape
    nctx = ctx.shape[1]
    depth = ada_w.shape[0]
    n_lat, n_ctx = batch * seq, batch * nctx
    dk = gla_gf_w2.shape[2] // GLA_HEADS
    dv = gla_onorm_g.shape[1]
    nq = ATT_HEADS * HEAD_DIM
    assert seq % WINDOW == 0 and nctx % WINDOW == 0 and d % LANES == 0

    n_cond = batch + 1
    pad = (-n_cond) % SUBLANES
    cc = jnp.concatenate([c, c_ctx[None, :], jnp.zeros((pad, d), F32)], axis=0)
    mods = _modulations(cc, ada_w, ada_b)
    mods = mods.reshape(depth, n_cond + pad, N_MOD, d)
    mods = jnp.pad(mods, ((0, 0), (0, 0), (0, MOD_ROWS - N_MOD), (0, 0)))

    rope = _rope_tables(seq)
    xl = x.reshape(n_lat, d)
    xc = ctx.reshape(n_ctx, d)

    tm_l, tm_c = _tile(seq, 512), _tile(nctx, 512)
    tm_pc = _tile(n_ctx, 512)
    tm_fc = _tile(n_ctx, 1024) if nctx <= 256 and _tile(n_ctx, 1024) % nctx == 0 else tm_c
    tq_l = _tile(seq, 512)
    gla_chunk = _tile(nctx, 256)
    tf = 512
    nkv = ATT_KV_HEADS * HEAD_DIM
    hk, hv = GLA_HEADS * dk, GLA_HEADS * dv

    w_up_all, w_down_all = ffn_w_up.astype(BF16), ffn_w_down.astype(BF16)
    w_qkv_all, attn_wo_all = attn_w_qkv.astype(BF16), attn_w_o.astype(BF16)
    gla_qk_all, gla_vr_all = gla_w_in[:, :, :2 * hk].astype(BF16), gla_w_in[:, :, 2 * hk:].astype(BF16)
    gla_wo_all = gla_w_o.astype(BF16)

    for i in range(depth):
        last = i == depth - 1
        mod_l = mods[i, :batch]
        mod_c = mods[i, batch:batch + 1]
        j = i // N_MIXERS
        if i % N_MIXERS == 0:
            w_qkv, w_o = (w_qkv_all, j), (attn_wo_all, j)
            seg = [(nq + 2 * nkv, BF16)]
            qkv_l, = _project(xl, mod_l, norm_mix_g[i], w_qkv, seg, rows_per_mod=seq, tm=tm_l,
                              rope=rope, seq=seq, rope_cols=nq + nkv)
            qkv_c, = _project(xc, mod_c, norm_mix_g[i], w_qkv, seg, rows_per_mod=n_ctx, tm=tm_pc)
            xl = _attention(qkv_l, qkv_c, xl, mod_l, attn_sink[j], w_o, band=True, tq=tq_l, seq=seq, nctx=nctx,
                            rows_per_mod=seq)
            if not last:
                xc = _attention(qkv_c, None, xc, mod_c, attn_sink[j], w_o, band=False, tq=nctx, seq=nctx, nctx=nctx,
                                rows_per_mod=n_ctx)
        else:
            w_qk, w_vr, w_o = (gla_qk_all, j), (gla_vr_all, j), (gla_wo_all, j)
            rank = gla_gf_w1.shape[2]
            w1 = jnp.zeros((d, LANES), F32).at[:, :rank].set(gla_gf_w1[j]).at[:, rank:2 * rank].set(gla_gb_w1[j])
            w2f = jnp.zeros((LANES, hk), F32).at[:rank].set(gla_gf_w2[j])
            w2b = jnp.zeros((LANES, hk), F32).at[rank:2 * rank].set(gla_gb_w2[j])
            gate = (w1.astype(BF16), w2f.astype(BF16), w2b.astype(BF16), gla_gf_b[j], gla_gb_b[j])
            streams = []
            for xs, mod, rpm, tm in ((xc, mod_c, n_ctx, tm_pc), (xl, mod_l, seq, tm_l)):
                qk, gf, gb = _project(xs, mod, norm_mix_g[i], w_qk, [(2 * hk, F32)], rows_per_mod=rpm, tm=tm,
                                      gate=gate)
                v, r = _project(xs, mod, norm_mix_g[i], w_vr, [(hv, BF16), (hv, BF16)], rows_per_mod=rpm, tm=tm)
                streams.append((qk, v, r, gf, gb))
            s0 = jnp.zeros((batch, GLA_HEADS, dk, dv), F32)
            scan = functools.partial(_gla_scan, batch=batch, chunk=gla_chunk, dk=dk, dv=dv)
            (qk_c, v_c, r_c, gf_c, gb_c), (qk_l, v_l, r_l, gf_l, gb_l) = streams
            of_c, ob_c, sf, sb = scan(qk_c, v_c, gf_c, gb_c, s0, s0, length=nctx)
            of_l, ob_l, _, _ = scan(qk_l, v_l, gf_l, gb_l, sf, sb, length=seq)
            xl = _gla_output(of_l, ob_l, r_l, xl, mod_l, gla_onorm_g[j], w_o, tm=tm_l, rows_per_mod=seq, dv=dv)
            if not last:
                xc = _gla_output(of_c, ob_c, r_c, xc, mod_c, gla_onorm_g[j], w_o, tm=tm_c, rows_per_mod=n_ctx, dv=dv)
        w_up, w_down = (w_up_all, i), (w_down_all, i)
        xl = _conv_ffn(xl, mod_l, norm_ffn_g[i], w_up, ffn_conv_w[i], ffn_conv_b[i], w_down,
                       tm=_tile(seq, 1024), tf=tf, seq=seq, rows_per_mod=seq, final_g=final_norm_g if last else None)
        if not last:
            xc = _conv_ffn(xc, mod_c, norm_ffn_g[i], w_up, ffn_conv_w[i], ffn_conv_b[i], w_down,
                           tm=tm_fc, tf=tf, seq=nctx, rows_per_mod=n_ctx)
    return xl.reshape(batch, seq, d)
```

```python
import functools

import jax
import jax.numpy as jnp
from jax import lax
from jax.experimental import pallas as pl
from jax.experimental.pallas import tpu as pltpu

GRID_W = 64
N_MIXERS = 2
ATT_HEADS = 16
ATT_KV_HEADS = 4
ATT_GROUP = ATT_HEADS // ATT_KV_HEADS
HEAD_DIM = 128
WINDOW = 128
ROPE_BASE = 10000.0
GLA_HEADS = 4
GLA_GATE_NORM = 16.0
GLA_BLOCK = 16
CONV_W = 3
EPS = 1e-6
N_MOD = 6
MOD_ROWS = 8

LANES = 128
SUBLANES = 8
VMEM_LIMIT_CAP = 60 * 1024 * 1024

BF16 = jnp.bfloat16
F32 = jnp.float32


def _dot(a, b):
    return jnp.dot(a, b, preferred_element_type=F32)


def _dot_nt(a, b):
    return lax.dot_general(a, b, (((1,), (1,)), ((), ())), preferred_element_type=F32)


def _dot_tn(a, b):
    return lax.dot_general(a, b, (((0,), (0,)), ((), ())), preferred_element_type=F32)


def _params(semantics, vmem_bytes):
    limit = min(VMEM_LIMIT_CAP, int(vmem_bytes * 1.25) + (4 << 20))
    return pltpu.CompilerParams(dimension_semantics=semantics, vmem_limit_bytes=limit)


def _sigmoid(x):
    return 1.0 / (1.0 + jnp.exp(-x))


def _silu(x):
    return x * _sigmoid(x)


def _mod_kernel(c_ref, w_ref, b_ref, o_ref):
    s = _silu(c_ref[...]).astype(BF16)
    o_ref[0] = _dot(s, w_ref[0].astype(BF16)) + b_ref[0]


def _modulations(cc, ada_w, ada_b):
    depth, d, n = ada_w.shape
    r = cc.shape[0]
    tn = 1024
    vmem = 2 * (d * tn * 4 + r * tn * 4 + tn * 4) + r * d * 4 + d * tn * 2
    return pl.pallas_call(
        _mod_kernel,
        out_shape=jax.ShapeDtypeStruct((depth, r, n), F32),
        grid=(depth, n // tn),
        in_specs=[
            pl.BlockSpec((r, d), lambda l, j: (0, 0)),
            pl.BlockSpec((1, d, tn), lambda l, j: (l, 0, j)),
            pl.BlockSpec((1, 1, tn), lambda l, j: (l, 0, j)),
        ],
        out_specs=pl.BlockSpec((1, r, tn), lambda l, j: (l, 0, j)),
        compiler_params=_params(("arbitrary", "arbitrary"), vmem),
        name="adaln_mod",
    )(cc, ada_w, ada_b.reshape(depth, 1, n))


def _log_sigmoid(z):
    return jnp.minimum(z, 0.0) - jnp.log(1.0 + jnp.exp(-jnp.abs(z)))


def _proj_kernel(*refs, segments, rope_cols, gated, col_chunk):
    n_out = len(segments) + (2 if gated else 0)
    x_ref, mod_ref, g_ref, w_ref = refs[:4]
    extra = refs[4:len(refs) - n_out - 2]
    outs = refs[len(refs) - n_out - 2:len(refs) - 2]
    h_even, h_odd = refs[-2:]
    s = pl.program_id(0)

    @pl.when(s == 0)
    def _():
        h_odd[...] = jnp.zeros_like(h_odd)

    def body(h_new, h_old):
        h = h_old[...]
        if gated:
            w1_ref, w2f_ref, w2b_ref, bf_ref, bb_ref = extra
            t = _dot(h, w1_ref[...]).astype(BF16)
            for w2_ref, b_ref, o_ref in ((w2f_ref, bf_ref, outs[-2]), (w2b_ref, bb_ref, outs[-1])):
                o_ref[...] = _dot(t, w2_ref[...]) + b_ref[...]
        if rope_cols:
            cos, sa, sb = extra[0][...], extra[1][...], extra[2][...]
        col = 0
        for (width, _), y_ref in zip(segments, outs):
            for c0 in range(0, width, col_chunk):
                y = _dot(h, w_ref[:, col + c0:col + c0 + col_chunk])
                if col + c0 < rope_cols:
                    parts = []
                    for p in range(col_chunk // HEAD_DIM):
                        yh = y[:, p * HEAD_DIM:(p + 1) * HEAD_DIM]
                        if col + c0 + p * HEAD_DIM < rope_cols:
                            up = pltpu.roll(yh, HEAD_DIM - HEAD_DIM // 4, axis=1)
                            dn = pltpu.roll(yh, HEAD_DIM // 4, axis=1)
                            yh = yh * cos + up * sa + dn * sb
                        parts.append(yh)
                    y = jnp.concatenate(parts, axis=1)
                y_ref[:, c0:c0 + col_chunk] = y.astype(y_ref.dtype)
            col += width
        x = x_ref[...]
        gs = g_ref[...] * (1.0 + mod_ref[0, 1:2, :])
        r = lax.rsqrt(jnp.mean(x * x, axis=-1, keepdims=True) + EPS)
        h_new[...] = ((x * r) * gs + mod_ref[0, 0:1, :]).astype(BF16)

    @pl.when(s % 2 == 0)
    def _():
        body(h_even, h_odd)

    @pl.when(s % 2 == 1)
    def _():
        body(h_odd, h_even)


def _project(x, mod, g, w, segments, *, rows_per_mod, tm, rope=None, seq=None, rope_cols=0, gate=None):
    n, d = x.shape
    w_stack, layer = w
    m = w_stack.shape[2]
    nrow = n // tm
    col_chunk = 1024 if all(wd % 1024 == 0 for wd, _ in segments) else 512

    def cur(s):
        return jnp.minimum(s, nrow - 1)

    def prev(s):
        return jnp.maximum(s - 1, 0)

    in_specs = [
        pl.BlockSpec((tm, d), lambda s: (cur(s), 0)),
        pl.BlockSpec((1, MOD_ROWS, d), lambda s: ((cur(s) * tm) // rows_per_mod, 0, 0)),
        pl.BlockSpec((1, d), lambda s: (0, 0)),
        pl.BlockSpec((None, d, m), lambda s: (layer, 0, 0), pipeline_mode=pl.Buffered(1)),
    ]
    args = [x, mod, g.reshape(1, d), w_stack]
    out_shape = [jax.ShapeDtypeStruct((n, wd), dt) for wd, dt in segments]
    out_specs = [pl.BlockSpec((tm, wd), lambda s: (prev(s), 0)) for wd, _ in segments]
    vmem = (2 * tm * d * 4 + d * m * 2 + 2 * sum(tm * wd * jnp.dtype(dt).itemsize for wd, dt in segments)
            + 2 * tm * d * 2 + 3 * tm * col_chunk * 4 + 2 * tm * d * 4)
    if rope is not None:
        tiles_per_seq = seq // tm
        for tab in rope:
            in_specs.append(pl.BlockSpec((tm, HEAD_DIM), lambda s: (prev(s) % tiles_per_seq, 0)))
            args.append(tab)
        vmem += 6 * tm * HEAD_DIM * 4
    if gate is not None:
        w1, w2f, w2b, bf, bb = gate
        dk = w2f.shape[1]
        in_specs += [
            pl.BlockSpec(w1.shape, lambda s: (0, 0)),
            pl.BlockSpec(w2f.shape, lambda s: (0, 0)),
            pl.BlockSpec(w2b.shape, lambda s: (0, 0)),
            pl.BlockSpec((1, dk), lambda s: (0, 0)),
            pl.BlockSpec((1, dk), lambda s: (0, 0)),
        ]
        args += [w1, w2f, w2b, bf.reshape(1, dk), bb.reshape(1, dk)]
        out_shape += [jax.ShapeDtypeStruct((n, dk), F32)] * 2
        out_specs += [pl.BlockSpec((tm, dk), lambda s: (prev(s), 0))] * 2
        vmem += 6 * tm * dk * 4 + 2 * (d * LANES * 2 + 2 * LANES * dk * 2)
    return pl.pallas_call(
        functools.partial(_proj_kernel, segments=tuple(segments), rope_cols=rope_cols, gated=gate is not None,
                          col_chunk=col_chunk),
        out_shape=out_shape,
        grid=(nrow + 1,),
        in_specs=in_specs,
        out_specs=out_specs,
        scratch_shapes=[pltpu.VMEM((tm, d), BF16), pltpu.VMEM((tm, d), BF16)],
        compiler_params=_params(("arbitrary",), vmem),
        name="proj",
    )(*args)


def _attend(qg, kall, vall, mask, sink_row):
    s = _dot_nt(kall, qg) * (HEAD_DIM ** -0.5)
    if mask is not None:
        s = jnp.where(mask, s, -jnp.inf)
    mx = jnp.maximum(jnp.max(s, axis=0, keepdims=True), sink_row)
    e = jnp.exp(s - mx)
    denom = jnp.sum(e, axis=0, keepdims=True) + jnp.exp(sink_row - mx)
    return _dot_tn(vall, e.astype(BF16)) / denom


def _sink_row(sink_ref, kv, rows):
    cols = [jnp.broadcast_to(sink_ref[0:1, kv * ATT_GROUP + h:kv * ATT_GROUP + h + 1], (1, rows))
            for h in range(ATT_GROUP)]
    return jnp.concatenate(cols, axis=1)


def _attn_kernel(*refs, band, tq, seq, sub):
    if band:
        (q_ref, kp_ref, kc_ref, kn_ref, vp_ref, vc_ref, vn_ref, kx_ref, vx_ref,
         x_ref, mod_ref, sink_ref, wo_ref, o_ref) = refs
    else:
        q_ref, kx_ref, vx_ref, x_ref, mod_ref, sink_ref, wo_ref, o_ref = refs
    blk = WINDOW
    nsub = tq // sub if band else 1
    rows = sub if band else tq
    if band:
        kwin = jnp.concatenate([kp_ref[...], kc_ref[...], kn_ref[...]], axis=0)
        vwin = jnp.concatenate([vp_ref[...], vc_ref[...], vn_ref[...]], axis=0)
        t0 = (pl.program_id(0) * tq) % seq
        nctx = kx_ref.shape[0]
        nwin = sub + 2 * blk
        ncol = nwin + nctx
        kj = lax.broadcasted_iota(jnp.int32, (ncol, ATT_GROUP * sub), 0)
        qi = lax.broadcasted_iota(jnp.int32, (ncol, ATT_GROUP * sub), 1) & (sub - 1)
        in_band = jnp.abs(qi - kj + blk) <= WINDOW
    outs = []
    for sb in range(nsub):
        heads = [None] * ATT_HEADS
        for kv in range(ATT_KV_HEADS):
            lo = kv * HEAD_DIM
            qg = jnp.concatenate(
                [q_ref[sb * rows:(sb + 1) * rows, (kv * ATT_GROUP + h) * HEAD_DIM:(kv * ATT_GROUP + h + 1) * HEAD_DIM]
                 for h in range(ATT_GROUP)], axis=0)
            if band:
                kall = jnp.concatenate([kwin[sb * sub:sb * sub + nwin, lo:lo + HEAD_DIM],
                                        kx_ref[:, lo:lo + HEAD_DIM]], axis=0)
                vall = jnp.concatenate([vwin[sb * sub:sb * sub + nwin, lo:lo + HEAD_DIM],
                                        vx_ref[:, lo:lo + HEAD_DIM]], axis=0)
                kpos = t0 + sb * sub - blk + kj
                mask = (kj >= nwin) | (in_band & (kpos >= 0) & (kpos < seq))
            else:
                kall = kx_ref[:, lo:lo + HEAD_DIM]
                vall = vx_ref[:, lo:lo + HEAD_DIM]
                mask = None
            og = _attend(qg, kall, vall, mask, _sink_row(sink_ref, kv, rows))
            for h in range(ATT_GROUP):
                heads[kv * ATT_GROUP + h] = og[:, h * rows:(h + 1) * rows].T.astype(BF16)
        outs.append(jnp.concatenate(heads, axis=1))
    o = outs[0] if len(outs) == 1 else jnp.concatenate(outs, axis=0)
    o_ref[...] = x_ref[...] + mod_ref[0, 2:3, :] * _dot(o, wo_ref[...])


def _attention(qkv, qkv_ctx, x, mod, sink, w_o, *, band, tq, seq, nctx, rows_per_mod):
    n, d = x.shape
    wo_stack, layer = w_o
    nq = ATT_HEADS * HEAD_DIM
    nkv = ATT_KV_HEADS * HEAD_DIM
    kcol, vcol = nq // nkv, nq // nkv + 1
    blk = WINDOW
    nblk = n // blk
    r = tq // blk
    q_spec = pl.BlockSpec((tq, nq), lambda i: (i, 0))
    tail_specs = [
        pl.BlockSpec((tq, d), lambda i: (i, 0)),
        pl.BlockSpec((1, MOD_ROWS, d), lambda i: ((i * tq) // rows_per_mod, 0, 0)),
        pl.BlockSpec((1, ATT_HEADS), lambda i: (0, 0)),
        pl.BlockSpec((None, nq, d), lambda i: (layer, 0, 0)),
    ]
    tail_args = [x, mod, sink.reshape(1, ATT_HEADS), wo_stack]
    if band:
        def ctx_idx(col):
            return lambda i: ((i * tq) // seq, col)
        in_specs = [
            q_spec,
            pl.BlockSpec((blk, nkv), lambda i: (jnp.maximum(i * r - 1, 0), kcol)),
            pl.BlockSpec((tq, nkv), lambda i: (i, kcol)),
            pl.BlockSpec((blk, nkv), lambda i: (jnp.minimum((i + 1) * r, nblk - 1), kcol)),
            pl.BlockSpec((blk, nkv), lambda i: (jnp.maximum(i * r - 1, 0), vcol)),
            pl.BlockSpec((tq, nkv), lambda i: (i, vcol)),
            pl.BlockSpec((blk, nkv), lambda i: (jnp.minimum((i + 1) * r, nblk - 1), vcol)),
            pl.BlockSpec((nctx, nkv), ctx_idx(kcol)),
            pl.BlockSpec((nctx, nkv), ctx_idx(vcol)),
        ] + tail_specs
        args = [qkv] * 7 + [qkv_ctx] * 2 + tail_args
        ncol = 3 * blk + nctx
    else:
        in_specs = [
            q_spec,
            pl.BlockSpec((tq, nkv), lambda i: (i, kcol)),
            pl.BlockSpec((tq, nkv), lambda i: (i, vcol)),
        ] + tail_specs
        args = [qkv] * 3 + tail_args
        ncol = tq
    vmem = (2 * (tq * nq * 2 + 4 * (tq + 2 * blk + nctx) * nkv * 2 + 2 * tq * d * 4 + nq * d * 2)
            + 6 * ATT_GROUP * blk * ncol * 4 + 2 * tq * d * 4)
    return pl.pallas_call(
        functools.partial(_attn_kernel, band=band, tq=tq, seq=seq, sub=min(tq, WINDOW)),
        out_shape=jax.ShapeDtypeStruct((n, d), F32),
        grid=(n // tq,),
        in_specs=in_specs,
        out_specs=pl.BlockSpec((tq, d), lambda i: (i, 0)),
        compiler_params=_params(("arbitrary",), vmem),
        name="attn_band" if band else "attn_ctx",
    )(*args)


def _gla_scan_kernel(qkf_ref, vf_ref, gf_ref, qkb_ref, vb_ref, gb_ref, s0f_ref, s0b_ref,
                     of_ref, ob_ref, sf_ref, sb_ref, *, chunk, dk, dv, q_scale):
    c = pl.program_id(1)

    @pl.when(c == 0)
    def _():
        sf_ref[...] = s0f_ref[...]
        sb_ref[...] = s0b_ref[...]

    nb = chunk // GLA_BLOCK
    levels = nb.bit_length() - 1
    row = lax.broadcasted_iota(jnp.int32, (chunk, chunk), 0)
    col = lax.broadcasted_iota(jnp.int32, (chunk, chunk), 1)
    blk_xor = (row // GLA_BLOCK) ^ (col // GLA_BLOCK)
    directions = (
        (qkf_ref, vf_ref, gf_ref, of_ref, sf_ref, False),
        (qkb_ref, vb_ref, gb_ref, ob_ref, sb_ref, True),
    )
    for qk_ref, v_ref, g_ref, o_ref, s_ref, reverse in directions:
        tri = (col >= row) if reverse else (col <= row)
        earlier = (col > row) if reverse else (col < row)
        tri_b = tri.astype(BF16)
        g = _log_sigmoid(g_ref[...]) / GLA_GATE_NORM
        g_hi = g.astype(BF16)
        g_lo = (g - g_hi.astype(F32)).astype(BF16)
        b = _dot(tri_b, g_hi) + _dot(tri_b, g_lo)

        hk = b.shape[1]
        end_row = 0 if reverse else GLA_BLOCK - 1
        ends = [b[i * GLA_BLOCK + end_row:i * GLA_BLOCK + end_row + 1] for i in range(nb)]
        zero = jnp.zeros((1, hk), F32)
        if reverse:
            starts = [ends[i + 1] if i + 1 < nb else zero for i in range(nb)]
        else:
            starts = [ends[i - 1] if i > 0 else zero for i in range(nb)]
        b_tot = ends[0] if reverse else ends[nb - 1]

        def group_first(i, size):
            return i | (size - 1) if reverse else i & ~(size - 1)

        def group_last(i, size):
            return i & ~(size - 1) if reverse else i | (size - 1)

        q = qk_ref[:, :hk] * q_scale
        k = qk_ref[:, hk:]
        v = v_ref[...]
        q_lv = [[] for _ in range(levels)]
        k_lv = [[] for _ in range(levels)]
        k_diag, q_state, k_state = [], [], []
        for i in range(nb):
            rs = slice(i * GLA_BLOCK, (i + 1) * GLA_BLOCK)
            q0 = q[rs] * jnp.exp(b[rs] - starts[i])
            k0 = k[rs] * jnp.exp(ends[i] - b[rs])
            k_diag.append(k0 * jnp.exp(starts[i] - ends[i]))
            for lv in range(levels):
                size = 1 << lv
                later = bool(i & size) != reverse
                q_lv[lv].append(q0 * jnp.exp(starts[i] - starts[group_first(i, size)]) if lv and later else q0)
                k_lv[lv].append(k0 * jnp.exp(ends[group_last(i, size)] - ends[i]) if lv and not later else k0)
            q_state.append(q0 * jnp.exp(starts[i]))
            k_state.append(k0 * jnp.exp(b_tot - ends[i]))

        def cat(parts):
            return jnp.concatenate(parts, axis=0).astype(BF16)

        q_lv, k_lv = [cat(p) for p in q_lv], [cat(p) for p in k_lv]
        k_diag, qs, ks = cat(k_diag), cat(q_state), cat(k_state)
        decay = jnp.exp(jnp.broadcast_to(b_tot, (LANES, hk))).T
        same_block = (blk_xor == 0) & tri
        pairs = [((blk_xor >> lv) == 1) & earlier for lv in range(levels)]

        for h in range(GLA_HEADS):
            kc = slice(h * dk, (h + 1) * dk)
            vc = slice(h * dv, (h + 1) * dv)
            a = jnp.where(same_block, _dot_nt(q_lv[0][:, kc], k_diag[:, kc]), 0.0)
            for lv in range(levels):
                a = jnp.where(pairs[lv], _dot_nt(q_lv[lv][:, kc], k_lv[lv][:, kc]), a)
            a = a.astype(BF16)
            st = s_ref[0, h]
            o_ref[:, vc] = (_dot(a, v[:, vc]) + _dot(qs[:, kc], st.astype(BF16))).astype(o_ref.dtype)
            upd = _dot_tn(ks[:, kc], v[:, vc])
            s_ref[0, h] = jnp.concatenate(
                [st[:, t * LANES:(t + 1) * LANES] * decay[kc] for t in range(dv // LANES)], axis=1) + upd


def _gla_scan(qk, v, g_f, g_b, s0_f, s0_b, *, batch, length, chunk, dk, dv):
    nc = length // chunk
    hk, hv = GLA_HEADS * dk, GLA_HEADS * dv

    def fwd(b, c):
        return b * nc + c

    def bwd(b, c):
        return b * nc + nc - 1 - c

    state_spec = pl.BlockSpec((1, GLA_HEADS, dk, dv), lambda b, c: (b, 0, 0, 0))
    in_specs = []
    for rows in (fwd, bwd):
        in_specs += [
            pl.BlockSpec((chunk, 2 * hk), lambda b, c, rows=rows: (rows(b, c), 0)),
            pl.BlockSpec((chunk, hv), lambda b, c, rows=rows: (rows(b, c), 0)),
            pl.BlockSpec((chunk, hk), lambda b, c, rows=rows: (rows(b, c), 0)),
        ]
    in_specs += [state_spec, state_spec]
    state_bytes = GLA_HEADS * dk * dv * 4
    vmem = (2 * 2 * (3 * chunk * hk * 4 + 2 * chunk * hv * 2) + 8 * state_bytes
            + 16 * chunk * hk * 4 + 4 * dk * dv * 4)
    return pl.pallas_call(
        functools.partial(_gla_scan_kernel, chunk=chunk, dk=dk, dv=dv, q_scale=dk ** -0.5),
        out_shape=[jax.ShapeDtypeStruct((batch * length, hv), BF16)] * 2
        + [jax.ShapeDtypeStruct(s0_f.shape, F32)] * 2,
        grid=(batch, nc),
        in_specs=in_specs,
        out_specs=[
            pl.BlockSpec((chunk, hv), lambda b, c: (fwd(b, c), 0)),
            pl.BlockSpec((chunk, hv), lambda b, c: (bwd(b, c), 0)),
            state_spec,
            state_spec,
        ],
        compiler_params=_params(("arbitrary", "arbitrary"), vmem),
        name="gla_scan",
    )(qk, v, g_f, qk, v, g_b, s0_f, s0_b)


def _gla_out_kernel(of_ref, ob_ref, r_ref, x_ref, mod_ref, g_ref, wo_ref, o_ref, *, dv):
    o = of_ref[...].astype(F32) + ob_ref[...].astype(F32)
    parts = []
    for h in range(o.shape[1] // dv):
        oh = o[:, h * dv:(h + 1) * dv]
        parts.append((oh * lax.rsqrt(jnp.mean(oh * oh, axis=-1, keepdims=True) + EPS)) * g_ref[...])
    gated = (jnp.concatenate(parts, axis=1) * _silu(r_ref[...].astype(F32))).astype(BF16)
    o_ref[...] = x_ref[...] + mod_ref[0, 2:3, :] * _dot(gated, wo_ref[...])


def _gla_output(o_f, o_b, r, x, mod, onorm_g, w_o, *, tm, rows_per_mod, dv):
    n, d = x.shape
    wo_stack, layer = w_o
    dvt = o_f.shape[1]
    vmem = 2 * (5 * tm * dvt * 4 + dvt * d * 2) + 4 * tm * dvt * 4
    return pl.pallas_call(
        functools.partial(_gla_out_kernel, dv=dv),
        out_shape=jax.ShapeDtypeStruct((n, d), F32),
        grid=(n // tm,),
        in_specs=[
            pl.BlockSpec((tm, dvt), lambda i: (i, 0)),
            pl.BlockSpec((tm, dvt), lambda i: (i, 0)),
            pl.BlockSpec((tm, dvt), lambda i: (i, 0)),
            pl.BlockSpec((tm, d), lambda i: (i, 0)),
            pl.BlockSpec((1, MOD_ROWS, d), lambda i: ((i * tm) // rows_per_mod, 0, 0)),
            pl.BlockSpec((1, dv), lambda i: (0, 0)),
            pl.BlockSpec((None, dvt, d), lambda i: (layer, 0, 0)),
        ],
        out_specs=pl.BlockSpec((tm, d), lambda i: (i, 0)),
        compiler_params=_params(("arbitrary",), vmem),
        name="gla_out",
    )(o_f, o_b, r, x, mod, onorm_g.reshape(1, dv), wo_stack)


def _ffn_kernel(xp_ref, x_ref, xn_ref, mod_ref, g_ref, fg_ref, wu_ref, cw_ref, cb_ref,
                wd_ref, o_ref, h_scr, u_scr, act_scr, *, tm, seq, final_norm, nf):
    i = pl.program_id(0)
    j = pl.program_id(1)
    halo = SUBLANES
    rows = min(tm, seq, 256)
    assert tm <= seq or (rows == seq and tm % seq == 0)

    @pl.when(j == 0)
    def _():
        shift = mod_ref[0, 3:4, :]
        gs = g_ref[...] * (1.0 + mod_ref[0, 4:5, :])

        def norm_mod(x):
            return (x * lax.rsqrt(jnp.mean(x * x, axis=-1, keepdims=True) + EPS)) * gs + shift

        first = (i * tm) % seq == 0
        last = ((i + 1) * tm) % seq == 0
        h_scr[0:halo, :] = jnp.where(first, 0.0, norm_mod(xp_ref[...])).astype(BF16)
        for r0 in range(0, tm, rows):
            h_scr[halo + r0:halo + r0 + rows, :] = norm_mod(x_ref[r0:r0 + rows, :]).astype(BF16)
        h_scr[halo + tm:2 * halo + tm, :] = jnp.where(last, 0.0, norm_mod(xn_ref[...])).astype(BF16)
        o_ref[...] = jnp.zeros_like(o_ref)

    h = h_scr[...]
    u_scr[...] = _dot(h, wu_ref[...])

    tf = wu_ref.shape[1] // 2
    gate_cols = pl.ds(pl.multiple_of(j * tf, tf), tf)
    val_cols = pl.ds(pl.multiple_of((nf + j) * tf, tf), tf)
    cwg, cbg = cw_ref[:, gate_cols], cb_ref[:, gate_cols]
    cwv, cbv = cw_ref[:, val_cols], cb_ref[:, val_cols]

    def conv(cs, cw, cb, r0):
        prev = u_scr[halo - 1 + r0:halo - 1 + r0 + rows, cs]
        nxt = u_scr[halo + 1 + r0:halo + 1 + r0 + rows, cs]
        if tm > seq:
            rid = lax.broadcasted_iota(jnp.int32, prev.shape, 0)
            prev = jnp.where(rid == 0, 0.0, prev)
            nxt = jnp.where(rid == rows - 1, 0.0, nxt)
        return cb + cw[0:1, :] * prev + cw[1:2, :] * u_scr[halo + r0:halo + r0 + rows, cs] + cw[2:3, :] * nxt

    gate, val = slice(0, tf), slice(tf, 2 * tf)

    for r0 in range(0, tm, rows):
        act_scr[r0:r0 + rows, :] = (_silu(conv(gate, cwg, cbg, r0)) * conv(val, cwv, cbv, r0)).astype(BF16)
    act = act_scr[...]
    for c0 in range(0, o_ref.shape[1], 512):
        o_ref[:, c0:c0 + 512] += _dot(act, wd_ref[:, c0:c0 + 512])

    @pl.when(j == pl.num_programs(1) - 1)
    def _():
        er = min(tm, 128)

        def residual(c, carry):
            sl = pl.ds(pl.multiple_of(c * er, er), er)
            y = x_ref[sl, :] + mod_ref[0, 5:6, :] * o_ref[sl, :]
            if final_norm:
                y = (y * lax.rsqrt(jnp.mean(y * y, axis=-1, keepdims=True) + EPS)) * fg_ref[...]
            o_ref[sl, :] = y
            return carry

        lax.fori_loop(0, tm // er, residual, 0)


def _conv_ffn(x, mod, g, w_up, conv_w, conv_b, w_down, *, tm, tf, seq, rows_per_mod, final_g=None):
    n, d = x.shape
    (up_stack, layer), (down_stack, _) = w_up, w_down
    dff = down_stack.shape[1]
    nf = dff // tf
    halo = SUBLANES
    r = tm // halo
    nhalo = n // halo
    conv_b = conv_b.reshape(1, 2 * dff)
    fg = (g if final_g is None else final_g).reshape(1, d)
    vmem = (2 * (2 * tm * d * 4 + 2 * d * tf * 2 + tf * d * 2) + (tm + 2 * halo) * d * 2
            + 2 * (tm + 2 * halo) * tf * 4 + tm * tf * 2 + 8 * min(tm, 256) * max(tf, d) * 4)
    return pl.pallas_call(
        functools.partial(_ffn_kernel, tm=tm, seq=seq, final_norm=final_g is not None, nf=nf),
        out_shape=jax.ShapeDtypeStruct((n, d), F32),
        grid=(n // tm, nf),
        in_specs=[
            pl.BlockSpec((halo, d), lambda i, j: (jnp.maximum(i * r - 1, 0), 0)),
            pl.BlockSpec((tm, d), lambda i, j: (i, 0)),
            pl.BlockSpec((halo, d), lambda i, j: (jnp.minimum((i + 1) * r, nhalo - 1), 0)),
            pl.BlockSpec((1, MOD_ROWS, d), lambda i, j: ((i * tm) // rows_per_mod, 0, 0)),
            pl.BlockSpec((1, d), lambda i, j: (0, 0)),
            pl.BlockSpec((1, d), lambda i, j: (0, 0)),
            pl.BlockSpec((None, d, 2 * tf), lambda i, j: (layer, 0, j)),
            pl.BlockSpec((CONV_W, 2 * dff), lambda i, j: (0, 0)),
            pl.BlockSpec((1, 2 * dff), lambda i, j: (0, 0)),
            pl.BlockSpec((None, tf, d), lambda i, j: (layer, j, 0)),
        ],
        out_specs=pl.BlockSpec((tm, d), lambda i, j: (i, 0)),
        scratch_shapes=[
            pltpu.VMEM((tm + 2 * halo, d), BF16),
            pltpu.VMEM((tm + 2 * halo, 2 * tf), F32),
            pltpu.VMEM((tm, tf), BF16),
        ],
        compiler_params=_params(("arbitrary", "arbitrary"), vmem),
        name="conv_ffn",
    )(x, x, x, mod, g.reshape(1, d), fg, up_stack, conv_w, conv_b, down_stack)


def _rope_tables(seq):
    half = HEAD_DIM // 2
    rows = seq // GRID_W
    row = jnp.repeat(jnp.arange(rows, dtype=F32), GRID_W)
    col = jnp.tile(jnp.arange(GRID_W, dtype=F32), rows)
    inv = ROPE_BASE ** (-jnp.arange(0, half, 2, dtype=F32) / half)
    ang_r = row[:, None] * inv[None, :]
    ang_c = col[:, None] * inv[None, :]
    zero = jnp.zeros_like(ang_r)
    cos = jnp.concatenate([jnp.cos(ang_r)] * 2 + [jnp.cos(ang_c)] * 2, axis=1)
    sa = jnp.concatenate([-jnp.sin(ang_r), zero, -jnp.sin(ang_c), zero], axis=1)
    sb = jnp.concatenate([zero, jnp.sin(ang_r), zero, jnp.sin(ang_c)], axis=1)
    return cos, sa, sb


def _tile(n, target):
    t = min(n, target)
    while n % t:
        t //= 2
    return t


def kernel(x, c, ctx, c_ctx, ada_w, ada_b, norm_mix_g, norm_ffn_g, ffn_w_up, ffn_conv_w, ffn_conv_b, ffn_w_down,
           attn_w_qkv, attn_sink, attn_w_o, gla_w_in, gla_gf_w1, gla_gf_w2, gla_gf_b, gla_gb_w1, gla_gb_w2,
           gla_gb_b, gla_onorm_g, gla_w_o, final_norm_g):
    batch, seq, d = x.shape
    nctx = ctx.shape[1]
    depth = ada_w.shape[0]
    n_lat, n_ctx = batch * seq, batch * nctx
    dk = gla_gf_w2.shape[2] // GLA_HEADS
    dv = gla_onorm_g.shape[1]
    nq = ATT_HEADS * HEAD_DIM
    assert seq % WINDOW == 0 and nctx % WINDOW == 0 and d % LANES == 0

    n_cond = batch + 1
    pad = (-n_cond) % SUBLANES
    cc = jnp.concatenate([c, c_ctx[None, :], jnp.zeros((pad, d), F32)], axis=0)
    mods = _modulations(cc, ada_w, ada_b)
    mods = mods.reshape(depth, n_cond + pad, N_MOD, d)
    mods = jnp.pad(mods, ((0, 0), (0, 0), (0, MOD_ROWS - N_MOD), (0, 0)))

    rope = _rope_tables(seq)
    xl = x.reshape(n_lat, d)
    xc = ctx.reshape(n_ctx, d)

    tm_l, tm_c = _tile(seq, 512), _tile(nctx, 512)
    tm_pc = _tile(n_ctx, 512)
    tm_fc = _tile(n_ctx, 1024) if nctx <= 256 and _tile(n_ctx, 1024) % nctx == 0 else tm_c
    tq_l = _tile(seq, 512)
    gla_chunk = _tile(nctx, 256)
    tf = 512
    nkv = ATT_KV_HEADS * HEAD_DIM
    hk, hv = GLA_HEADS * dk, GLA_HEADS * dv

    dff = ffn_w_down.shape[1]
    w_up_all = (ffn_w_up.reshape(depth, d, 2, dff // tf, tf).transpose(0, 1, 3, 2, 4)
                .reshape(depth, d, 2 * dff).astype(BF16))
    w_down_all = ffn_w_down.astype(BF16)
    w_qkv_all, attn_wo_all = attn_w_qkv.astype(BF16), attn_w_o.astype(BF16)
    gla_qk_all, gla_vr_all = gla_w_in[:, :, :2 * hk].astype(BF16), gla_w_in[:, :, 2 * hk:].astype(BF16)
    gla_wo_all = gla_w_o.astype(BF16)

    for i in range(depth):
        last = i == depth - 1
        mod_l = mods[i, :batch]
        mod_c = mods[i, batch:batch + 1]
        j = i // N_MIXERS
        if i % N_MIXERS == 0:
            w_qkv, w_o = (w_qkv_all, j), (attn_wo_all, j)
            seg = [(nq + 2 * nkv, BF16)]
            qkv_l, = _project(xl, mod_l, norm_mix_g[i], w_qkv, seg, rows_per_mod=seq, tm=tm_l,
                              rope=rope, seq=seq, rope_cols=nq + nkv)
            qkv_c, = _project(xc, mod_c, norm_mix_g[i], w_qkv, seg, rows_per_mod=n_ctx, tm=tm_pc)
            xl = _attention(qkv_l, qkv_c, xl, mod_l, attn_sink[j], w_o, band=True, tq=tq_l, seq=seq, nctx=nctx,
                            rows_per_mod=seq)
            if not last:
                xc = _attention(qkv_c, None, xc, mod_c, attn_sink[j], w_o, band=False, tq=nctx, seq=nctx, nctx=nctx,
                                rows_per_mod=n_ctx)
        else:
            w_qk, w_vr, w_o = (gla_qk_all, j), (gla_vr_all, j), (gla_wo_all, j)
            rank = gla_gf_w1.shape[2]
            w1 = jnp.zeros((d, LANES), F32).at[:, :rank].set(gla_gf_w1[j]).at[:, rank:2 * rank].set(gla_gb_w1[j])
            w2f = jnp.zeros((LANES, hk), F32).at[:rank].set(gla_gf_w2[j])
            w2b = jnp.zeros((LANES, hk), F32).at[rank:2 * rank].set(gla_gb_w2[j])
            gate = (w1.astype(BF16), w2f.astype(BF16), w2b.astype(BF16), gla_gf_b[j], gla_gb_b[j])
            streams = []
            for xs, mod, rpm, tm in ((xc, mod_c, n_ctx, tm_pc), (xl, mod_l, seq, tm_l)):
                qk, gf, gb = _project(xs, mod, norm_mix_g[i], w_qk, [(2 * hk, F32)], rows_per_mod=rpm, tm=tm,
                                      gate=gate)
                v, r = _project(xs, mod, norm_mix_g[i], w_vr, [(hv, BF16), (hv, BF16)], rows_per_mod=rpm, tm=tm)
                streams.append((qk, v, r, gf, gb))
            s0 = jnp.zeros((batch, GLA_HEADS, dk, dv), F32)
            scan = functools.partial(_gla_scan, batch=batch, chunk=gla_chunk, dk=dk, dv=dv)
            (qk_c, v_c, r_c, gf_c, gb_c), (qk_l, v_l, r_l, gf_l, gb_l) = streams
            of_c, ob_c, sf, sb = scan(qk_c, v_c, gf_c, gb_c, s0, s0, length=nctx)
            of_l, ob_l, _, _ = scan(qk_l, v_l, gf_l, gb_l, sf, sb, length=seq)
            xl = _gla_output(of_l, ob_l, r_l, xl, mod_l, gla_onorm_g[j], w_o, tm=tm_l, rows_per_mod=seq, dv=dv)
            if not last:
                xc = _gla_output(of_c, ob_c, r_c, xc, mod_c, gla_onorm_g[j], w_o, tm=tm_c, rows_per_mod=n_ctx, dv=dv)
        w_up, w_down = (w_up_all, i), (w_down_all, i)
        xl = _conv_ffn(xl, mod_l, norm_ffn_g[i], w_up, ffn_conv_w[i], ffn_conv_b[i], w_down,
                       tm=_tile(seq, 1024), tf=tf, seq=seq, rows_per_mod=seq, final_g=final_norm_g if last else None)
        if not last:
            xc = _conv_ffn(xc, mod_c, norm_ffn_g[i], w_up, ffn_conv_w[i], ffn_conv_b[i], w_down,
                           tm=tm_fc, tf=tf, seq=nctx, rows_per_mod=n_ctx)
    return xl.reshape(batch, seq, d)
```

```python
import functools

import jax
import jax.numpy as jnp
from jax import lax
from jax.experimental import pallas as pl
from jax.experimental.pallas import tpu as pltpu

GRID_W = 64
N_MIXERS = 2
ATT_HEADS = 16
ATT_KV_HEADS = 4
ATT_GROUP = ATT_HEADS // ATT_KV_HEADS
HEAD_DIM = 128
WINDOW = 128
ROPE_BASE = 10000.0
GLA_HEADS = 4
GLA_GATE_NORM = 16.0
GLA_BLOCK = 16
CONV_W = 3
EPS = 1e-6
N_MOD = 6
MOD_ROWS = 8

LANES = 128
SUBLANES = 8
VMEM_LIMIT_CAP = 60 * 1024 * 1024

BF16 = jnp.bfloat16
F32 = jnp.float32


def _dot(a, b):
    return jnp.dot(a, b, preferred_element_type=F32)


def _dot_nt(a, b):
    return lax.dot_general(a, b, (((1,), (1,)), ((), ())), preferred_element_type=F32)


def _dot_tn(a, b):
    return lax.dot_general(a, b, (((0,), (0,)), ((), ())), preferred_element_type=F32)


def _params(semantics, vmem_bytes):
    limit = min(VMEM_LIMIT_CAP, int(vmem_bytes * 1.25) + (4 << 20))
    return pltpu.CompilerParams(dimension_semantics=semantics, vmem_limit_bytes=limit)


def _sigmoid(x):
    return 1.0 / (1.0 + jnp.exp(-x))


def _silu(x):
    return x * _sigmoid(x)


def _mod_kernel(c_ref, w_ref, b_ref, o_ref):
    s = _silu(c_ref[...]).astype(BF16)
    o_ref[0] = _dot(s, w_ref[0].astype(BF16)) + b_ref[0]


def _modulations(cc, ada_w, ada_b):
    depth, d, n = ada_w.shape
    r = cc.shape[0]
    tn = 1024
    vmem = 2 * (d * tn * 4 + r * tn * 4 + tn * 4) + r * d * 4 + d * tn * 2
    return pl.pallas_call(
        _mod_kernel,
        out_shape=jax.ShapeDtypeStruct((depth, r, n), F32),
        grid=(depth, n // tn),
        in_specs=[
            pl.BlockSpec((r, d), lambda l, j: (0, 0)),
            pl.BlockSpec((1, d, tn), lambda l, j: (l, 0, j)),
            pl.BlockSpec((1, 1, tn), lambda l, j: (l, 0, j)),
        ],
        out_specs=pl.BlockSpec((1, r, tn), lambda l, j: (l, 0, j)),
        compiler_params=_params(("arbitrary", "arbitrary"), vmem),
        name="adaln_mod",
    )(cc, ada_w, ada_b.reshape(depth, 1, n))


def _log_sigmoid(z):
    return jnp.minimum(z, 0.0) - jnp.log(1.0 + jnp.exp(-jnp.abs(z)))


def _proj_kernel(*refs, segments, rope_cols, gated, col_chunk):
    n_out = len(segments) + (2 if gated else 0)
    x_ref, mod_ref, g_ref, w_ref = refs[:4]
    extra = refs[4:len(refs) - n_out - 2]
    outs = refs[len(refs) - n_out - 2:len(refs) - 2]
    h_even, h_odd = refs[-2:]
    s = pl.program_id(0)

    @pl.when(s == 0)
    def _():
        h_odd[...] = jnp.zeros_like(h_odd)

    def body(h_new, h_old):
        h = h_old[...]
        if gated:
            w1_ref, w2f_ref, w2b_ref, bf_ref, bb_ref = extra
            t = _dot(h, w1_ref[...]).astype(BF16)
            for w2_ref, b_ref, o_ref in ((w2f_ref, bf_ref, outs[-2]), (w2b_ref, bb_ref, outs[-1])):
                o_ref[...] = _dot(t, w2_ref[...]) + b_ref[...]
        if rope_cols:
            cos, sa, sb = extra[0][...], extra[1][...], extra[2][...]
        col = 0
        for (width, _), y_ref in zip(segments, outs):
            for c0 in range(0, width, col_chunk):
                y = _dot(h, w_ref[:, col + c0:col + c0 + col_chunk])
                if col + c0 < rope_cols:
                    parts = []
                    for p in range(col_chunk // HEAD_DIM):
                        yh = y[:, p * HEAD_DIM:(p + 1) * HEAD_DIM]
                        if col + c0 + p * HEAD_DIM < rope_cols:
                            up = pltpu.roll(yh, HEAD_DIM - HEAD_DIM // 4, axis=1)
                            dn = pltpu.roll(yh, HEAD_DIM // 4, axis=1)
                            yh = yh * cos + up * sa + dn * sb
                        parts.append(yh)
                    y = jnp.concatenate(parts, axis=1)
                y_ref[:, c0:c0 + col_chunk] = y.astype(y_ref.dtype)
            col += width
        x = x_ref[...]
        gs = g_ref[...] * (1.0 + mod_ref[0, 1:2, :])
        r = lax.rsqrt(jnp.mean(x * x, axis=-1, keepdims=True) + EPS)
        h_new[...] = ((x * r) * gs + mod_ref[0, 0:1, :]).astype(BF16)

    @pl.when(s % 2 == 0)
    def _():
        body(h_even, h_odd)

    @pl.when(s % 2 == 1)
    def _():
        body(h_odd, h_even)


def _project(x, mod, g, w, segments, *, rows_per_mod, tm, rope=None, seq=None, rope_cols=0, gate=None):
    n, d = x.shape
    w_stack, layer = w
    m = sum(wd for wd, _ in segments)
    nrow = n // tm
    col_chunk = 1024 if all(wd % 1024 == 0 for wd, _ in segments) else 512

    def cur(s):
        return jnp.minimum(s, nrow - 1)

    def prev(s):
        return jnp.maximum(s - 1, 0)

    in_specs = [
        pl.BlockSpec((tm, d), lambda s: (cur(s), 0)),
        pl.BlockSpec((1, MOD_ROWS, d), lambda s: ((cur(s) * tm) // rows_per_mod, 0, 0)),
        pl.BlockSpec((1, d), lambda s: (0, 0)),
        pl.BlockSpec((None, d, m), lambda s: (layer, 0, 0), pipeline_mode=pl.Buffered(1)),
    ]
    args = [x, mod, g.reshape(1, d), w_stack]
    out_shape = [jax.ShapeDtypeStruct((n, wd), dt) for wd, dt in segments]
    out_specs = [pl.BlockSpec((tm, wd), lambda s: (prev(s), 0)) for wd, _ in segments]
    vmem = (2 * tm * d * 4 + d * m * 2 + 2 * sum(tm * wd * jnp.dtype(dt).itemsize for wd, dt in segments)
            + 2 * tm * d * 2 + 3 * tm * col_chunk * 4 + 2 * tm * d * 4)
    if rope is not None:
        tiles_per_seq = seq // tm
        for tab in rope:
            in_specs.append(pl.BlockSpec((tm, HEAD_DIM), lambda s: (prev(s) % tiles_per_seq, 0)))
            args.append(tab)
        vmem += 6 * tm * HEAD_DIM * 4
    if gate is not None:
        w1, w2f, w2b, bf, bb = gate
        dk = w2f.shape[1]
        in_specs += [
            pl.BlockSpec(w1.shape, lambda s: (0, 0)),
            pl.BlockSpec(w2f.shape, lambda s: (0, 0)),
            pl.BlockSpec(w2b.shape, lambda s: (0, 0)),
            pl.BlockSpec((1, dk), lambda s: (0, 0)),
            pl.BlockSpec((1, dk), lambda s: (0, 0)),
        ]
        args += [w1, w2f, w2b, bf.reshape(1, dk), bb.reshape(1, dk)]
        out_shape += [jax.ShapeDtypeStruct((n, dk), F32)] * 2
        out_specs += [pl.BlockSpec((tm, dk), lambda s: (prev(s), 0))] * 2
        vmem += 6 * tm * dk * 4 + 2 * (d * LANES * 2 + 2 * LANES * dk * 2)
    return pl.pallas_call(
        functools.partial(_proj_kernel, segments=tuple(segments), rope_cols=rope_cols, gated=gate is not None,
                          col_chunk=col_chunk),
        out_shape=out_shape,
        grid=(nrow + 1,),
        in_specs=in_specs,
        out_specs=out_specs,
        scratch_shapes=[pltpu.VMEM((tm, d), BF16), pltpu.VMEM((tm, d), BF16)],
        compiler_params=_params(("arbitrary",), vmem),
        name="proj",
    )(*args)


def _attend(qg, kall, vall, mask, sink_row):
    s = _dot_nt(kall, qg) * (HEAD_DIM ** -0.5)
    if mask is not None:
        s = jnp.where(mask, s, -jnp.inf)
    mx = jnp.maximum(jnp.max(s, axis=0, keepdims=True), sink_row)
    e = jnp.exp(s - mx)
    denom = jnp.sum(e, axis=0, keepdims=True) + jnp.exp(sink_row - mx)
    return _dot_tn(vall, e.astype(BF16)) / denom


def _sink_row(sink_ref, kv, rows):
    cols = [jnp.broadcast_to(sink_ref[0:1, kv * ATT_GROUP + h:kv * ATT_GROUP + h + 1], (1, rows))
            for h in range(ATT_GROUP)]
    return jnp.concatenate(cols, axis=1)


def _attn_kernel(*refs, band, tq, seq, sub):
    if band:
        (q_ref, kp_ref, kc_ref, kn_ref, vp_ref, vc_ref, vn_ref, kx_ref, vx_ref,
         x_ref, mod_ref, sink_ref, wo_ref, o_ref) = refs
    else:
        q_ref, kx_ref, vx_ref, x_ref, mod_ref, sink_ref, wo_ref, o_ref = refs
    blk = WINDOW
    nsub = tq // sub if band else 1
    rows = sub if band else tq
    if band:
        kwin = jnp.concatenate([kp_ref[...], kc_ref[...], kn_ref[...]], axis=0)
        vwin = jnp.concatenate([vp_ref[...], vc_ref[...], vn_ref[...]], axis=0)
        t0 = (pl.program_id(0) * tq) % seq
        nctx = kx_ref.shape[0]
        nwin = sub + 2 * blk
        ncol = nwin + nctx
        kj = lax.broadcasted_iota(jnp.int32, (ncol, ATT_GROUP * sub), 0)
        qi = lax.broadcasted_iota(jnp.int32, (ncol, ATT_GROUP * sub), 1) & (sub - 1)
        in_band = jnp.abs(qi - kj + blk) <= WINDOW
    outs = []
    for sb in range(nsub):
        heads = [None] * ATT_HEADS
        for kv in range(ATT_KV_HEADS):
            lo = kv * HEAD_DIM
            qg = jnp.concatenate(
                [q_ref[sb * rows:(sb + 1) * rows, (kv * ATT_GROUP + h) * HEAD_DIM:(kv * ATT_GROUP + h + 1) * HEAD_DIM]
                 for h in range(ATT_GROUP)], axis=0)
            if band:
                kall = jnp.concatenate([kwin[sb * sub:sb * sub + nwin, lo:lo + HEAD_DIM],
                                        kx_ref[:, lo:lo + HEAD_DIM]], axis=0)
                vall = jnp.concatenate([vwin[sb * sub:sb * sub + nwin, lo:lo + HEAD_DIM],
                                        vx_ref[:, lo:lo + HEAD_DIM]], axis=0)
                kpos = t0 + sb * sub - blk + kj
                mask = (kj >= nwin) | (in_band & (kpos >= 0) & (kpos < seq))
            else:
                kall = kx_ref[:, lo:lo + HEAD_DIM]
                vall = vx_ref[:, lo:lo + HEAD_DIM]
                mask = None
            og = _attend(qg, kall, vall, mask, _sink_row(sink_ref, kv, rows))
            for h in range(ATT_GROUP):
                heads[kv * ATT_GROUP + h] = og[:, h * rows:(h + 1) * rows].T.astype(BF16)
        outs.append(jnp.concatenate(heads, axis=1))
    o = outs[0] if len(outs) == 1 else jnp.concatenate(outs, axis=0)
    o_ref[...] = x_ref[...] + mod_ref[0, 2:3, :] * _dot(o, wo_ref[...])


def _attention(qkv, qkv_ctx, x, mod, sink, w_o, *, band, tq, seq, nctx, rows_per_mod):
    n, d = x.shape
    wo_stack, layer = w_o
    nq = ATT_HEADS * HEAD_DIM
    nkv = ATT_KV_HEADS * HEAD_DIM
    kcol, vcol = nq // nkv, nq // nkv + 1
    blk = WINDOW
    nblk = n // blk
    r = tq // blk
    q_spec = pl.BlockSpec((tq, nq), lambda i: (i, 0))
    tail_specs = [
        pl.BlockSpec((tq, d), lambda i: (i, 0)),
        pl.BlockSpec((1, MOD_ROWS, d), lambda i: ((i * tq) // rows_per_mod, 0, 0)),
        pl.BlockSpec((1, ATT_HEADS), lambda i: (0, 0)),
        pl.BlockSpec((None, nq, d), lambda i: (layer, 0, 0)),
    ]
    tail_args = [x, mod, sink.reshape(1, ATT_HEADS), wo_stack]
    if band:
        def ctx_idx(col):
            return lambda i: ((i * tq) // seq, col)
        in_specs = [
            q_spec,
            pl.BlockSpec((blk, nkv), lambda i: (jnp.maximum(i * r - 1, 0), kcol)),
            pl.BlockSpec((tq, nkv), lambda i: (i, kcol)),
            pl.BlockSpec((blk, nkv), lambda i: (jnp.minimum((i + 1) * r, nblk - 1), kcol)),
            pl.BlockSpec((blk, nkv), lambda i: (jnp.maximum(i * r - 1, 0), vcol)),
            pl.BlockSpec((tq, nkv), lambda i: (i, vcol)),
            pl.BlockSpec((blk, nkv), lambda i: (jnp.minimum((i + 1) * r, nblk - 1), vcol)),
            pl.BlockSpec((nctx, nkv), ctx_idx(kcol)),
            pl.BlockSpec((nctx, nkv), ctx_idx(vcol)),
        ] + tail_specs
        args = [qkv] * 7 + [qkv_ctx] * 2 + tail_args
        ncol = 3 * blk + nctx
    else:
        in_specs = [
            q_spec,
            pl.BlockSpec((tq, nkv), lambda i: (i, kcol)),
            pl.BlockSpec((tq, nkv), lambda i: (i, vcol)),
        ] + tail_specs
        args = [qkv] * 3 + tail_args
        ncol = tq
    vmem = (2 * (tq * nq * 2 + 4 * (tq + 2 * blk + nctx) * nkv * 2 + 2 * tq * d * 4 + nq * d * 2)
            + 6 * ATT_GROUP * blk * ncol * 4 + 2 * tq * d * 4)
    return pl.pallas_call(
        functools.partial(_attn_kernel, band=band, tq=tq, seq=seq, sub=min(tq, WINDOW)),
        out_shape=jax.ShapeDtypeStruct((n, d), F32),
        grid=(n // tq,),
        in_specs=in_specs,
        out_specs=pl.BlockSpec((tq, d), lambda i: (i, 0)),
        compiler_params=_params(("arbitrary",), vmem),
        name="attn_band" if band else "attn_ctx",
    )(*args)


def _gla_scan_kernel(qkf_ref, vf_ref, gf_ref, qkb_ref, vb_ref, gb_ref, s0f_ref, s0b_ref,
                     of_ref, ob_ref, sf_ref, sb_ref, *, chunk, dk, dv, q_scale):
    c = pl.program_id(1)

    @pl.when(c == 0)
    def _():
        sf_ref[...] = s0f_ref[...]
        sb_ref[...] = s0b_ref[...]

    nb = chunk // GLA_BLOCK
    levels = nb.bit_length() - 1
    row = lax.broadcasted_iota(jnp.int32, (chunk, chunk), 0)
    col = lax.broadcasted_iota(jnp.int32, (chunk, chunk), 1)
    blk_xor = (row // GLA_BLOCK) ^ (col // GLA_BLOCK)
    directions = (
        (qkf_ref, vf_ref, gf_ref, of_ref, sf_ref, False),
        (qkb_ref, vb_ref, gb_ref, ob_ref, sb_ref, True),
    )
    for qk_ref, v_ref, g_ref, o_ref, s_ref, reverse in directions:
        tri = (col >= row) if reverse else (col <= row)
        earlier = (col > row) if reverse else (col < row)
        tri_b = tri.astype(BF16)
        g = _log_sigmoid(g_ref[...]) / GLA_GATE_NORM
        g_hi = g.astype(BF16)
        g_lo = (g - g_hi.astype(F32)).astype(BF16)
        b = _dot(tri_b, g_hi) + _dot(tri_b, g_lo)

        hk = b.shape[1]
        end_row = 0 if reverse else GLA_BLOCK - 1
        ends = [b[i * GLA_BLOCK + end_row:i * GLA_BLOCK + end_row + 1] for i in range(nb)]
        zero = jnp.zeros((1, hk), F32)
        if reverse:
            starts = [ends[i + 1] if i + 1 < nb else zero for i in range(nb)]
        else:
            starts = [ends[i - 1] if i > 0 else zero for i in range(nb)]
        b_tot = ends[0] if reverse else ends[nb - 1]

        def group_first(i, size):
            return i | (size - 1) if reverse else i & ~(size - 1)

        def group_last(i, size):
            return i & ~(size - 1) if reverse else i | (size - 1)

        q = qk_ref[:, :hk] * q_scale
        k = qk_ref[:, hk:]
        v = v_ref[...]
        q_lv = [[] for _ in range(levels)]
        k_lv = [[] for _ in range(levels)]
        k_diag, q_state, k_state = [], [], []
        for i in range(nb):
            rs = slice(i * GLA_BLOCK, (i + 1) * GLA_BLOCK)
            q0 = q[rs] * jnp.exp(b[rs] - starts[i])
            k0 = k[rs] * jnp.exp(ends[i] - b[rs])
            k_diag.append(k0 * jnp.exp(starts[i] - ends[i]))
            for lv in range(levels):
                size = 1 << lv
                later = bool(i & size) != reverse
                q_lv[lv].append(q0 * jnp.exp(starts[i] - starts[group_first(i, size)]) if lv and later else q0)
                k_lv[lv].append(k0 * jnp.exp(ends[group_last(i, size)] - ends[i]) if lv and not later else k0)
            q_state.append(q0 * jnp.exp(starts[i]))
            k_state.append(k0 * jnp.exp(b_tot - ends[i]))

        def cat(parts):
            return jnp.concatenate(parts, axis=0).astype(BF16)

        q_lv, k_lv = [cat(p) for p in q_lv], [cat(p) for p in k_lv]
        k_diag, qs, ks = cat(k_diag), cat(q_state), cat(k_state)
        decay = jnp.exp(jnp.broadcast_to(b_tot, (LANES, hk))).T
        same_block = (blk_xor == 0) & tri
        pairs = [((blk_xor >> lv) == 1) & earlier for lv in range(levels)]

        for h in range(GLA_HEADS):
            kc = slice(h * dk, (h + 1) * dk)
            vc = slice(h * dv, (h + 1) * dv)
            a = jnp.where(same_block, _dot_nt(q_lv[0][:, kc], k_diag[:, kc]), 0.0)
            for lv in range(levels):
                a = jnp.where(pairs[lv], _dot_nt(q_lv[lv][:, kc], k_lv[lv][:, kc]), a)
            a = a.astype(BF16)
            st = s_ref[0, h]
            o_ref[:, vc] = (_dot(a, v[:, vc]) + _dot(qs[:, kc], st.astype(BF16))).astype(o_ref.dtype)
            upd = _dot_tn(ks[:, kc], v[:, vc])
            s_ref[0, h] = jnp.concatenate(
                [st[:, t * LANES:(t + 1) * LANES] * decay[kc] for t in range(dv // LANES)], axis=1) + upd


def _gla_scan(qk, v, g_f, g_b, s0_f, s0_b, *, batch, length, chunk, dk, dv):
    nc = length // chunk
    hk, hv = GLA_HEADS * dk, GLA_HEADS * dv

    def fwd(b, c):
        return b * nc + c

    def bwd(b, c):
        return b * nc + nc - 1 - c

    state_spec = pl.BlockSpec((1, GLA_HEADS, dk, dv), lambda b, c: (b, 0, 0, 0))
    in_specs = []
    for rows in (fwd, bwd):
        in_specs += [
            pl.BlockSpec((chunk, 2 * hk), lambda b, c, rows=rows: (rows(b, c), 0)),
            pl.BlockSpec((chunk, hv), lambda b, c, rows=rows: (rows(b, c), 0)),
            pl.BlockSpec((chunk, hk), lambda b, c, rows=rows: (rows(b, c), 0)),
        ]
    in_specs += [state_spec, state_spec]
    state_bytes = GLA_HEADS * dk * dv * 4
    vmem = (2 * 2 * (3 * chunk * hk * 4 + 2 * chunk * hv * 2) + 8 * state_bytes
            + 16 * chunk * hk * 4 + 4 * dk * dv * 4)
    return pl.pallas_call(
        functools.partial(_gla_scan_kernel, chunk=chunk, dk=dk, dv=dv, q_scale=dk ** -0.5),
        out_shape=[jax.ShapeDtypeStruct((batch * length, hv), BF16)] * 2
        + [jax.ShapeDtypeStruct(s0_f.shape, F32)] * 2,
        grid=(batch, nc),
        in_specs=in_specs,
        out_specs=[
            pl.BlockSpec((chunk, hv), lambda b, c: (fwd(b, c), 0)),
            pl.BlockSpec((chunk, hv), lambda b, c: (bwd(b, c), 0)),
            state_spec,
            state_spec,
        ],
        compiler_params=_params(("arbitrary", "arbitrary"), vmem),
        name="gla_scan",
    )(qk, v, g_f, qk, v, g_b, s0_f, s0_b)


def _gla_out_kernel(of_ref, ob_ref, r_ref, x_ref, mod_ref, g_ref, wo_ref, o_ref, *, dv):
    o = of_ref[...].astype(F32) + ob_ref[...].astype(F32)
    parts = []
    for h in range(o.shape[1] // dv):
        oh = o[:, h * dv:(h + 1) * dv]
        parts.append((oh * lax.rsqrt(jnp.mean(oh * oh, axis=-1, keepdims=True) + EPS)) * g_ref[...])
    gated = (jnp.concatenate(parts, axis=1) * _silu(r_ref[...].astype(F32))).astype(BF16)
    o_ref[...] = x_ref[...] + mod_ref[0, 2:3, :] * _dot(gated, wo_ref[...])


def _gla_output(o_f, o_b, r, x, mod, onorm_g, w_o, *, tm, rows_per_mod, dv):
    n, d = x.shape
    wo_stack, layer = w_o
    dvt = o_f.shape[1]
    vmem = 2 * (5 * tm * dvt * 4 + dvt * d * 2) + 4 * tm * dvt * 4
    return pl.pallas_call(
        functools.partial(_gla_out_kernel, dv=dv),
        out_shape=jax.ShapeDtypeStruct((n, d), F32),
        grid=(n // tm,),
        in_specs=[
            pl.BlockSpec((tm, dvt), lambda i: (i, 0)),
            pl.BlockSpec((tm, dvt), lambda i: (i, 0)),
            pl.BlockSpec((tm, dvt), lambda i: (i, 0)),
            pl.BlockSpec((tm, d), lambda i: (i, 0)),
            pl.BlockSpec((1, MOD_ROWS, d), lambda i: ((i * tm) // rows_per_mod, 0, 0)),
            pl.BlockSpec((1, dv), lambda i: (0, 0)),
            pl.BlockSpec((None, dvt, d), lambda i: (layer, 0, 0)),
        ],
        out_specs=pl.BlockSpec((tm, d), lambda i: (i, 0)),
        compiler_params=_params(("arbitrary",), vmem),
        name="gla_out",
    )(o_f, o_b, r, x, mod, onorm_g.reshape(1, dv), wo_stack)


def _ffn_kernel(xp_ref, x_ref, xn_ref, mod_ref, g_ref, fg_ref, wg_ref, wv_ref, cw_ref, cb_ref,
                wd_ref, o_ref, h_scr, ug_scr, uv_scr, act_scr, *, tm, seq, final_norm, nf):
    i = pl.program_id(0)
    j = pl.program_id(1)
    halo = SUBLANES
    rows = min(tm, seq, 256)
    assert tm <= seq or (rows == seq and tm % seq == 0)

    @pl.when(j == 0)
    def _():
        shift = mod_ref[0, 3:4, :]
        gs = g_ref[...] * (1.0 + mod_ref[0, 4:5, :])

        def norm_mod(x):
            return (x * lax.rsqrt(jnp.mean(x * x, axis=-1, keepdims=True) + EPS)) * gs + shift

        first = (i * tm) % seq == 0
        last = ((i + 1) * tm) % seq == 0
        h_scr[0:halo, :] = jnp.where(first, 0.0, norm_mod(xp_ref[...])).astype(BF16)
        for r0 in range(0, tm, rows):
            h_scr[halo + r0:halo + r0 + rows, :] = norm_mod(x_ref[r0:r0 + rows, :]).astype(BF16)
        h_scr[halo + tm:2 * halo + tm, :] = jnp.where(last, 0.0, norm_mod(xn_ref[...])).astype(BF16)
        o_ref[...] = jnp.zeros_like(o_ref)

    h = h_scr[...]
    ug_scr[...] = _dot(h, wg_ref[...])
    uv_scr[...] = _dot(h, wv_ref[...])

    tf = wg_ref.shape[1]
    gate_cols = pl.ds(pl.multiple_of(j * tf, tf), tf)
    val_cols = pl.ds(pl.multiple_of((nf + j) * tf, tf), tf)
    cwg, cbg = cw_ref[:, gate_cols], cb_ref[:, gate_cols]
    cwv, cbv = cw_ref[:, val_cols], cb_ref[:, val_cols]

    def conv(u_scr, cw, cb, r0):
        prev = u_scr[halo - 1 + r0:halo - 1 + r0 + rows, :]
        nxt = u_scr[halo + 1 + r0:halo + 1 + r0 + rows, :]
        if tm > seq:
            rid = lax.broadcasted_iota(jnp.int32, prev.shape, 0)
            prev = jnp.where(rid == 0, 0.0, prev)
            nxt = jnp.where(rid == rows - 1, 0.0, nxt)
        return cb + cw[0:1, :] * prev + cw[1:2, :] * u_scr[halo + r0:halo + r0 + rows, :] + cw[2:3, :] * nxt

    for r0 in range(0, tm, rows):
        act_scr[r0:r0 + rows, :] = (_silu(conv(ug_scr, cwg, cbg, r0)) * conv(uv_scr, cwv, cbv, r0)).astype(BF16)
    act = act_scr[...]
    for c0 in range(0, o_ref.shape[1], 512):
        o_ref[:, c0:c0 + 512] += _dot(act, wd_ref[:, c0:c0 + 512])

    @pl.when(j == pl.num_programs(1) - 1)
    def _():
        er = min(tm, 128)

        def residual(c, carry):
            sl = pl.ds(pl.multiple_of(c * er, er), er)
            y = x_ref[sl, :] + mod_ref[0, 5:6, :] * o_ref[sl, :]
            if final_norm:
                y = (y * lax.rsqrt(jnp.mean(y * y, axis=-1, keepdims=True) + EPS)) * fg_ref[...]
            o_ref[sl, :] = y
            return carry

        lax.fori_loop(0, tm // er, residual, 0)


def _conv_ffn(x, mod, g, w_up, conv_w, conv_b, w_down, *, tm, tf, seq, rows_per_mod, final_g=None):
    n, d = x.shape
    (up_stack, layer), (down_stack, _) = w_up, w_down
    dff = down_stack.shape[1]
    nf = dff // tf
    halo = SUBLANES
    r = tm // halo
    nhalo = n // halo
    conv_b = conv_b.reshape(1, 2 * dff)
    fg = (g if final_g is None else final_g).reshape(1, d)
    vmem = (2 * (2 * tm * d * 4 + 2 * d * tf * 2 + tf * d * 2) + (tm + 2 * halo) * d * 2
            + 2 * (tm + 2 * halo) * tf * 4 + tm * tf * 2 + 8 * min(tm, 256) * max(tf, d) * 4)
    return pl.pallas_call(
        functools.partial(_ffn_kernel, tm=tm, seq=seq, final_norm=final_g is not None, nf=nf),
        out_shape=jax.ShapeDtypeStruct((n, d), F32),
        grid=(n // tm, nf),
        in_specs=[
            pl.BlockSpec((halo, d), lambda i, j: (jnp.maximum(i * r - 1, 0), 0)),
            pl.BlockSpec((tm, d), lambda i, j: (i, 0)),
            pl.BlockSpec((halo, d), lambda i, j: (jnp.minimum((i + 1) * r, nhalo - 1), 0)),
            pl.BlockSpec((1, MOD_ROWS, d), lambda i, j: ((i * tm) // rows_per_mod, 0, 0)),
            pl.BlockSpec((1, d), lambda i, j: (0, 0)),
            pl.BlockSpec((1, d), lambda i, j: (0, 0)),
            pl.BlockSpec((None, d, tf), lambda i, j: (layer, 0, j)),
            pl.BlockSpec((None, d, tf), lambda i, j: (layer, 0, nf + j)),
            pl.BlockSpec((CONV_W, 2 * dff), lambda i, j: (0, 0)),
            pl.BlockSpec((1, 2 * dff), lambda i, j: (0, 0)),
            pl.BlockSpec((None, tf, d), lambda i, j: (layer, j, 0)),
        ],
        out_specs=pl.BlockSpec((tm, d), lambda i, j: (i, 0)),
        scratch_shapes=[
            pltpu.VMEM((tm + 2 * halo, d), BF16),
            pltpu.VMEM((tm + 2 * halo, tf), F32),
            pltpu.VMEM((tm + 2 * halo, tf), F32),
            pltpu.VMEM((tm, tf), BF16),
        ],
        compiler_params=_params(("arbitrary", "arbitrary"), vmem),
        name="conv_ffn",
    )(x, x, x, mod, g.reshape(1, d), fg, up_stack, up_stack, conv_w, conv_b, down_stack)


def _rope_tables(seq):
    half = HEAD_DIM // 2
    rows = seq // GRID_W
    row = jnp.repeat(jnp.arange(rows, dtype=F32), GRID_W)
    col = jnp.tile(jnp.arange(GRID_W, dtype=F32), rows)
    inv = ROPE_BASE ** (-jnp.arange(0, half, 2, dtype=F32) / half)
    ang_r = row[:, None] * inv[None, :]
    ang_c = col[:, None] * inv[None, :]
    zero = jnp.zeros_like(ang_r)
    cos = jnp.concatenate([jnp.cos(ang_r)] * 2 + [jnp.cos(ang_c)] * 2, axis=1)
    sa = jnp.concatenate([-jnp.sin(ang_r), zero, -jnp.sin(ang_c), zero], axis=1)
    sb = jnp.concatenate([zero, jnp.sin(ang_r), zero, jnp.sin(ang_c)], axis=1)
    return cos, sa, sb


def _tile(n, target):
    t = min(n, target)
    while n % t:
        t //= 2
    return t


def kernel(x, c, ctx, c_ctx, ada_w, ada_b, norm_mix_g, norm_ffn_g, ffn_w_up, ffn_conv_w, ffn_conv_b, ffn_w_down,
           attn_w_qkv, attn_sink, attn_w_o, gla_w_in, gla_gf_w1, gla_gf_w2, gla_gf_b, gla_gb_w1, gla_gb_w2,
           gla_gb_b, gla_onorm_g, gla_w_o, final_norm_g):
    batch, seq, d = x.shape
    nctx = ctx.shape[1]
    depth = ada_w.shape[0]
    n_lat, n_ctx = batch * seq, batch * nctx
    dk = gla_gf_w2.shape[2] // GLA_HEADS
    dv = gla_onorm_g.shape[1]
    nq = ATT_HEADS * HEAD_DIM
    assert seq % WINDOW == 0 and nctx % WINDOW == 0 and d % LANES == 0

    n_cond = batch + 1
    pad = (-n_cond) % SUBLANES
    cc = jnp.concatenate([c, c_ctx[None, :], jnp.zeros((pad, d), F32)], axis=0)
    mods = _modulations(cc, ada_w, ada_b)
    mods = mods.reshape(depth, n_cond + pad, N_MOD, d)
    mods = jnp.pad(mods, ((0, 0), (0, 0), (0, MOD_ROWS - N_MOD), (0, 0)))

    rope = _rope_tables(seq)
    xl = x.reshape(n_lat, d)
    xc = ctx.reshape(n_ctx, d)

    tm_l, tm_c = _tile(seq, 512), _tile(nctx, 512)
    tm_pc = _tile(n_ctx, 512)
    tm_fc = _tile(n_ctx, 1024) if nctx <= 256 and _tile(n_ctx, 1024) % nctx == 0 else tm_c
    tq_l = _tile(seq, 512)
    gla_chunk = _tile(nctx, 256)
    tf = 512
    nkv = ATT_KV_HEADS * HEAD_DIM
    hk, hv = GLA_HEADS * dk, GLA_HEADS * dv

    w_up_all, w_down_all = ffn_w_up.astype(BF16), ffn_w_down.astype(BF16)
    w_qkv_all, attn_wo_all = attn_w_qkv.astype(BF16), attn_w_o.astype(BF16)
    gla_qk_all, gla_vr_all = gla_w_in[:, :, :2 * hk].astype(BF16), gla_w_in[:, :, 2 * hk:].astype(BF16)
    gla_wo_all = gla_w_o.astype(BF16)

    for i in range(depth):
        last = i == depth - 1
        mod_l = mods[i, :batch]
        mod_c = mods[i, batch:batch + 1]
        j = i // N_MIXERS
        if i % N_MIXERS == 0:
            w_qkv, w_o = (w_qkv_all, j), (attn_wo_all, j)
            seg = [(nq + 2 * nkv, BF16)]
            qkv_l, = _project(xl, mod_l, norm_mix_g[i], w_qkv, seg, rows_per_mod=seq, tm=tm_l,
                              rope=rope, seq=seq, rope_cols=nq + nkv)
            qkv_c, = _project(xc, mod_c, norm_mix_g[i], w_qkv, seg, rows_per_mod=n_ctx, tm=tm_pc)
            xl = _attention(qkv_l, qkv_c, xl, mod_l, attn_sink[j], w_o, band=True, tq=tq_l, seq=seq, nctx=nctx,
                            rows_per_mod=seq)
            if not last:
                xc = _attention(qkv_c, None, xc, mod_c, attn_sink[j], w_o, band=False, tq=nctx, seq=nctx, nctx=nctx,
                                rows_per_mod=n_ctx)
        else:
            w_qk, w_vr, w_o = (gla_qk_all, j), (gla_vr_all, j), (gla_wo_all, j)
            rank = gla_gf_w1.shape[2]
            w1 = jnp.zeros((d, LANES), F32).at[:, :rank].set(gla_gf_w1[j]).at[:, rank:2 * rank].set(gla_gb_w1[j])
            w2f = jnp.zeros((LANES, hk), F32).at[:rank].set(gla_gf_w2[j])
            w2b = jnp.zeros((LANES, hk), F32).at[rank:2 * rank].set(gla_gb_w2[j])
            gate = (w1.astype(BF16), w2f.astype(BF16), w2b.astype(BF16), gla_gf_b[j], gla_gb_b[j])
            streams = []
            for xs, mod, rpm, tm in ((xc, mod_c, n_ctx, tm_pc), (xl, mod_l, seq, tm_l)):
                qk, gf, gb = _project(xs, mod, norm_mix_g[i], w_qk, [(2 * hk, F32)], rows_per_mod=rpm, tm=tm,
                                      gate=gate)
                need_r = xs is xl or not last
                v, *r = _project(xs, mod, norm_mix_g[i], w_vr, [(hv, BF16)] * (2 if need_r else 1), rows_per_mod=rpm,
                                 tm=tm)
                r = r[0] if need_r else None
                streams.append((qk, v, r, gf, gb))
            s0 = jnp.zeros((batch, GLA_HEADS, dk, dv), F32)
            scan = functools.partial(_gla_scan, batch=batch, chunk=gla_chunk, dk=dk, dv=dv)
            (qk_c, v_c, r_c, gf_c, gb_c), (qk_l, v_l, r_l, gf_l, gb_l) = streams
            of_c, ob_c, sf, sb = scan(qk_c, v_c, gf_c, gb_c, s0, s0, length=nctx)
            of_l, ob_l, _, _ = scan(qk_l, v_l, gf_l, gb_l, sf, sb, length=seq)
            xl = _gla_output(of_l, ob_l, r_l, xl, mod_l, gla_onorm_g[j], w_o, tm=tm_l, rows_per_mod=seq, dv=dv)
            if not last:
                xc = _gla_output(of_c, ob_c, r_c, xc, mod_c, gla_onorm_g[j], w_o, tm=tm_c, rows_per_mod=n_ctx, dv=dv)
        w_up, w_down = (w_up_all, i), (w_down_all, i)
        xl = _conv_ffn(xl, mod_l, norm_ffn_g[i], w_up, ffn_conv_w[i], ffn_conv_b[i], w_down,
                       tm=_tile(seq, 1024), tf=tf, seq=seq, rows_per_mod=seq, final_g=final_norm_g if last else None)
        if not last:
            xc = _conv_ffn(xc, mod_c, norm_ffn_g[i], w_up, ffn_conv_w[i], ffn_conv_b[i], w_down,
                           tm=tm_fc, tf=tf, seq=nctx, rows_per_mod=n_ctx)
    return xl.reshape(batch, seq, d)
```
